```python
import jax, jax.numpy as jnp
from jax import lax
import numpy as np

D_MODEL = 2048
BATCH = 16
SEQ = 256
DEPTH = 2
DEC_BATCH = 4
DEC_SEQ = 4096
PAST_LEN = 512

GRID_W = 64
GLA_HEADS = 8
GLA_DK = 64
GLA_DV = 128
GLA_RANK = 16
GLA_TAU = 16.0
GLA_CHUNK = 64
ATT_HEADS = 8
ATT_KV_HEADS = 2
HEAD_DIM = 128
ROPE_THETA = 10000.0
Q_BLOCK = 128
MIX_W = GLA_HEADS * GLA_DV + ATT_HEADS * HEAD_DIM
IN_SPLITS = (GLA_HEADS * GLA_DK, GLA_HEADS * GLA_DK, GLA_HEADS * GLA_DV, 2 * GLA_RANK,
             GLA_HEADS * GLA_DV, ATT_HEADS * HEAD_DIM, ATT_KV_HEADS * HEAD_DIM, ATT_KV_HEADS * HEAD_DIM)
IN_COLS = sum(IN_SPLITS)
PEER_HEADS = 8
PEER_QDIM = 256
N_KEYS = 128
N_EXPERTS = N_KEYS * N_KEYS
PEER_TOPK = 16
PEER_TOKEN_BLOCK = 128
EPS = 1e-6

kernel_name = "hybrid_gla_gqa_peer_diffusion_step"


def rms_norm(x, g):
    xf = x.astype(jnp.float32)
    y = xf * lax.rsqrt(jnp.mean(xf * xf, axis=-1, keepdims=True) + EPS)
    return (y * g.astype(jnp.float32)).astype(x.dtype)


def rope_axis(x, ang):
    cos = jnp.cos(ang)
    sin = jnp.sin(ang)
    cos = jnp.concatenate([cos, cos], axis=-1).astype(x.dtype)
    sin = jnp.concatenate([sin, sin], axis=-1).astype(x.dtype)
    x1, x2 = jnp.split(x, 2, axis=-1)
    return x * cos + jnp.concatenate([-x2, x1], axis=-1) * sin


def axial_rope(x):
    n_tok = x.shape[2]
    rows = n_tok // GRID_W
    row = jnp.repeat(jnp.arange(rows), GRID_W).astype(jnp.float32)
    col = jnp.tile(jnp.arange(GRID_W), rows).astype(jnp.float32)
    half = HEAD_DIM // 2
    inv_freq = ROPE_THETA ** (-jnp.arange(0, half, 2, dtype=jnp.float32) / half)
    xr = rope_axis(x[..., :half], row[:, None] * inv_freq[None, :])
    xc = rope_axis(x[..., half:], col[:, None] * inv_freq[None, :])
    return jnp.concatenate([xr, xc], axis=-1)


def gla_chunked(q, k, v, log_a, s0):
    out_dtype = v.dtype
    B, H, T, _ = q.shape
    dv = v.shape[-1]
    n = T // GLA_CHUNK
    f32 = jnp.float32
    chunk = lambda t: t.astype(f32).reshape(B, H, n, GLA_CHUNK, t.shape[-1])
    q, k, v, log_a = chunk(q), chunk(k), chunk(v), chunk(log_a)
    b = jnp.cumsum(log_a, axis=3)
    b_last = b[:, :, :, -1:, :]
    b_mid = b[:, :, :, GLA_CHUNK // 2 - 1:GLA_CHUNK // 2, :]
    scores = jnp.einsum('bhnid,bhnjd->bhnij', q * jnp.exp(b - b_mid), k * jnp.exp(b_mid - b))
    lower = jnp.tril(jnp.ones((GLA_CHUNK, GLA_CHUNK), dtype=bool))
    scores = jnp.where(lower, scores, 0.0)
    o_intra = jnp.einsum('bhnij,bhnjv->bhniv', scores, v)
    upd = jnp.einsum('bhncd,bhncv->nbhdv', k * jnp.exp(b_last - b), v)
    decay = jnp.exp(b_last[:, :, :, 0, :]).transpose(2, 0, 1, 3)

    def step(s, inp):
        d, u = inp
        return d[..., None] * s + u, s

    s_final, s_prev = lax.scan(step, s0.astype(f32), (decay, upd))
    o_inter = jnp.einsum('bhncd,nbhdv->bhncv', q * jnp.exp(b), s_prev)
    o = (o_intra + o_inter).reshape(B, H, T, dv)
    return o.astype(out_dtype), s_final.astype(s0.dtype)


def gla_bidirectional(q, k, v, la_fwd, la_bwd, s0_fwd, s0_bwd):
    o_f, s_f = gla_chunked(q, k, v, la_fwd, s0_fwd)
    flip = lambda t: jnp.flip(t, axis=2)
    o_b, s_b = gla_chunked(flip(q), flip(k), flip(v), flip(la_bwd), s0_bwd)
    return o_f + flip(o_b), s_f, s_b


def block_attention(q, k, v):
    B, H, T, hd = q.shape
    G = H // ATT_KV_HEADS
    nb = T // Q_BLOCK
    qb = jnp.moveaxis(q.reshape(B, ATT_KV_HEADS, G, nb, Q_BLOCK, hd), 3, 0)
    scale = hd ** -0.5

    def one_block(qblk):
        s = jnp.einsum('bkgqd,bksd->bkgqs', qblk, k).astype(jnp.float32) * scale
        p = jax.nn.softmax(s, axis=-1).astype(v.dtype)
        return jnp.einsum('bkgqs,bksd->bkgqd', p, v)

    o = lax.map(one_block, qb)
    return jnp.moveaxis(o, 0, 3).reshape(B, H, T, hd)


def token_mixer(h, w_in, w_gate, b_gate, gla_norm, q_norm, k_norm, w_out, gla_init, ctx_k, ctx_v):
    B, T, _ = h.shape
    latent = ctx_k is not None
    proj = h @ w_in
    split_at = [int(i) for i in np.cumsum(IN_SPLITS)[:-1]]
    qg, kg, vg, glr, og, qa, ka, va = jnp.split(proj, split_at, axis=-1)
    heads = lambda t, n: t.reshape(B, T, n, -1).transpose(0, 2, 1, 3)
    qg = heads(qg, GLA_HEADS) * (GLA_DK ** -0.5)
    kg = heads(kg, GLA_HEADS)
    vg = heads(vg, GLA_HEADS)
    glr = glr.reshape(B, T, 2, GLA_RANK)
    z = jnp.einsum('btzr,zrk->zbtk', glr, w_gate) + b_gate[:, None, None, :]
    log_a = jax.nn.log_sigmoid(z.astype(jnp.float32)) / GLA_TAU
    log_a = log_a.reshape(2, B, T, GLA_HEADS, GLA_DK).transpose(0, 1, 3, 2, 4)
    o_g, s_f, s_b = gla_bidirectional(qg, kg, vg, log_a[0], log_a[1], gla_init[:, 0], gla_init[:, 1])
    o_g = rms_norm(o_g, gla_norm) * jax.nn.silu(heads(og, GLA_HEADS))
    o_g = o_g.transpose(0, 2, 1, 3).reshape(B, T, GLA_HEADS * GLA_DV)
    qa = rms_norm(heads(qa, ATT_HEADS), q_norm)
    ka = rms_norm(heads(ka, ATT_KV_HEADS), k_norm)
    va = heads(va, ATT_KV_HEADS)
    if latent:
        keys = jnp.concatenate([ctx_k, axial_rope(ka)], axis=2)
        vals = jnp.concatenate([ctx_v, va], axis=2)
        o_a = block_attention(axial_rope(qa), keys, vals)
    else:
        o_a = block_attention(qa, ka, va)
    o_a = o_a.transpose(0, 2, 1, 3).reshape(B, T, ATT_HEADS * HEAD_DIM)
    out = jnp.concatenate([o_g, o_a], axis=-1) @ w_out
    return out, jnp.stack([s_f, s_b], axis=1), ka, va


def peer(h, wq, keys, u_tab, v_tab):
    B, T, D = h.shape
    ht = h.reshape(B * T, D)
    n_tok = B * T
    q = (ht @ wq).reshape(n_tok, PEER_HEADS, 2, PEER_QDIM // 2)
    s = jnp.einsum('nhpd,hpkd->nhpk', q, keys).astype(jnp.float32)
    v1, i1 = lax.top_k(s[:, :, 0], PEER_TOPK)
    v2, i2 = lax.top_k(s[:, :, 1], PEER_TOPK)
    cand = (v1[..., :, None] + v2[..., None, :]).reshape(n_tok, PEER_HEADS, PEER_TOPK * PEER_TOPK)
    cidx = (i1[..., :, None] * N_KEYS + i2[..., None, :]).reshape(n_tok, PEER_HEADS, PEER_TOPK * PEER_TOPK)
    best, pos = lax.top_k(cand, PEER_TOPK)
    eidx = jnp.take_along_axis(cidx, pos, axis=-1)
    gates = jax.nn.softmax(best, axis=-1).astype(h.dtype)
    nblk = n_tok // PEER_TOKEN_BLOCK
    k_sel = PEER_HEADS * PEER_TOPK
    xb = ht.reshape(nblk, PEER_TOKEN_BLOCK, D)
    ib = eidx.reshape(nblk, PEER_TOKEN_BLOCK, k_sel)
    gb = gates.reshape(nblk, PEER_TOKEN_BLOCK, k_sel)

    def one_block(args):
        x_blk, i_blk, g_blk = args
        act = jax.nn.gelu(jnp.einsum('td,tkd->tk', x_blk, u_tab[i_blk])) * g_blk
        return jnp.einsum('tk,tkd->td', act, v_tab[i_blk])

    out = lax.map(one_block, (xb, ib, gb))
    return out.reshape(B, T, D)


def trunk_layer(x, cvec, w_mod, b_mod, norm_mix, norm_ffn, w_in, w_gate, b_gate, gla_norm,
                q_norm, k_norm, w_out, peer_wq, peer_keys, peer_u, peer_v, gla_init, ctx_k, ctx_v):
    mod = (jax.nn.silu(cvec) @ w_mod + b_mod)[:, None, :]
    sh1, sc1, g1, sh2, sc2, g2 = jnp.split(mod, 6, axis=-1)
    h = rms_norm(x, norm_mix) * (1.0 + sc1) + sh1
    mix, gla_state, k, v = token_mixer(h, w_in, w_gate, b_gate, gla_norm, q_norm, k_norm, w_out,
                                       gla_init, ctx_k, ctx_v)
    x = x + g1 * mix
    h = rms_norm(x, norm_ffn) * (1.0 + sc2) + sh2
    x = x + g2 * peer(h, peer_wq, peer_keys, peer_u, peer_v)
    return x, gla_state, k, v


def setup_inputs(seed: int = 0) -> dict:
    key = jax.random.key(seed)
    ks = jax.random.split(key, 24)
    nrm = lambda k, shape, s: jax.random.normal(k, shape, jnp.float32) * s
    return {
        "x_prompt": nrm(ks[0], (BATCH, SEQ, D_MODEL), 1.0),
        "x_sample": nrm(ks[1], (DEC_BATCH, DEC_SEQ, D_MODEL), 1.0),
        "cache_k": nrm(ks[2], (DEC_BATCH, DEPTH, ATT_KV_HEADS, PAST_LEN, HEAD_DIM), 1.0),
        "cache_v": nrm(ks[3], (DEC_BATCH, DEPTH, ATT_KV_HEADS, PAST_LEN, HEAD_DIM), 1.0),
        "state_gla": nrm(ks[4], (DEC_BATCH, DEPTH, 2, GLA_HEADS, GLA_DK, GLA_DV), 0.3),
        "c": nrm(ks[5], (DEC_BATCH, D_MODEL), 1.0),
        "c_ctx": nrm(ks[6], (D_MODEL,), 1.0),
        "w_mod": nrm(ks[7], (DEPTH, D_MODEL, 6 * D_MODEL), 0.5 * D_MODEL ** -0.5),
        "b_mod": nrm(ks[8], (DEPTH, 6 * D_MODEL), 0.02),
        "norm_mix": 1.0 + nrm(ks[9], (DEPTH, D_MODEL), 0.1),
        "norm_ffn": 1.0 + nrm(ks[10], (DEPTH, D_MODEL), 0.1),
        "w_in": nrm(ks[11], (DEPTH, D_MODEL, IN_COLS), D_MODEL ** -0.5),
        "w_gate": nrm(ks[12], (DEPTH, 2, GLA_RANK, GLA_HEADS * GLA_DK), GLA_RANK ** -0.5),
        "b_gate": nrm(ks[13], (DEPTH, 2, GLA_HEADS * GLA_DK), 0.1),
        "gla_norm": 1.0 + nrm(ks[14], (DEPTH, GLA_DV), 0.1),
        "q_norm": 1.0 + nrm(ks[15], (DEPTH, HEAD_DIM), 0.1),
        "k_norm": 1.0 + nrm(ks[16], (DEPTH, HEAD_DIM), 0.1),
        "w_out": nrm(ks[17], (DEPTH, MIX_W, D_MODEL), MIX_W ** -0.5),
        "peer_wq": nrm(ks[18], (DEPTH, D_MODEL, PEER_HEADS * PEER_QDIM), D_MODEL ** -0.5),
        "peer_keys": nrm(ks[19], (DEPTH, PEER_HEADS, 2, N_KEYS, PEER_QDIM // 2), (PEER_QDIM // 2) ** -0.5),
        "peer_u": nrm(ks[20], (DEPTH, N_EXPERTS, D_MODEL), D_MODEL ** -0.5),
        "peer_v": nrm(ks[21], (DEPTH, N_EXPERTS, D_MODEL), 1.0),
    }


def reference(x_prompt, x_sample, cache_k, cache_v, state_gla, c, c_ctx, w_mod, b_mod, norm_mix,
              norm_ffn, w_in, w_gate, b_gate, gla_norm, q_norm, k_norm, w_out, peer_wq, peer_keys,
              peer_u, peer_v):
    y_prompt = x_prompt
    y_sample = x_sample
    c_ctx_b = c_ctx[None, :]
    zero_state = jnp.zeros((x_prompt.shape[0], 2, GLA_HEADS, GLA_DK, GLA_DV), x_prompt.dtype)
    ks_out, vs_out, ss_out = [], [], []
    for l in range(DEPTH):
        lw = (w_mod[l], b_mod[l], norm_mix[l], norm_ffn[l], w_in[l], w_gate[l], b_gate[l], gla_norm[l],
              q_norm[l], k_norm[l], w_out[l], peer_wq[l], peer_keys[l], peer_u[l], peer_v[l])
        y_prompt, s_l, k_l, v_l = trunk_layer(y_prompt, c_ctx_b, *lw, zero_state, None, None)
        ss_out.append(s_l)
        ks_out.append(k_l)
        vs_out.append(v_l)
        y_sample, _, _, _ = trunk_layer(y_sample, c, *lw, state_gla[:, l], cache_k[:, l], cache_v[:, l])
    new_cache_k = jnp.stack(ks_out, axis=1)
    new_cache_v = jnp.stack(vs_out, axis=1)
    new_state_gla = jnp.stack(ss_out, axis=1)
    return (y_prompt, y_sample, new_cache_k, new_cache_v, new_state_gla)
```

```python
import functools

import jax
import jax.numpy as jnp
import numpy as np
from jax import lax
from jax.experimental import pallas as pl
from jax.experimental.pallas import tpu as pltpu

F32 = jnp.float32
BF16 = jnp.bfloat16

D_MODEL = 2048
DEPTH = 2
GRID_W = 64
GLA_HEADS = 8
GLA_DK = 64
GLA_DV = 128
GLA_RANK = 16
GLA_TAU = 16.0
GLA_CHUNK = 64
ATT_HEADS = 8
ATT_KV_HEADS = 2
ATT_GROUP = ATT_HEADS // ATT_KV_HEADS
HEAD_DIM = 128
ROPE_THETA = 10000.0
PEER_HEADS = 8
PEER_QDIM = 256
N_KEYS = 128
PEER_TOPK = 16
PEER_SEL = PEER_HEADS * PEER_TOPK
EPS = 1e-6

LANES = 128
MOD_ROWS = 8
VMEM_LIMIT = 56 * 1024 * 1024

COL_QK = 0
COL_V = COL_QK + GLA_HEADS * LANES
COL_OG = COL_V + GLA_HEADS * GLA_DV
COL_QA = COL_OG + GLA_HEADS * GLA_DV
COL_KV = COL_QA + ATT_HEADS * HEAD_DIM
PROJ_COLS = COL_KV + 2 * ATT_KV_HEADS * HEAD_DIM

NT_DIMS = (((1,), (1,)), ((), ()))
TN_DIMS = (((0,), (0,)), ((), ()))


def _params(*sem):
    return pltpu.CompilerParams(dimension_semantics=sem, vmem_limit_bytes=VMEM_LIMIT)


def _dot(a, b):
    return jnp.dot(a, b, preferred_element_type=F32)


def _dot_exact01(a01, x):
    hi = x.astype(BF16)
    r1 = x - hi.astype(F32)
    mid = r1.astype(BF16)
    lo = (r1 - mid.astype(F32)).astype(BF16)
    return _dot(a01, hi) + _dot(a01, mid) + _dot(a01, lo)


def _dot_exact01_rhs(x, b01):
    hi = x.astype(BF16)
    r1 = x - hi.astype(F32)
    mid = r1.astype(BF16)
    lo = (r1 - mid.astype(F32)).astype(BF16)
    return _dot(hi, b01) + _dot(mid, b01) + _dot(lo, b01)


def _rms(x):
    return x * lax.rsqrt(jnp.mean(x * x, axis=-1, keepdims=True) + EPS)


def _mod_kernel(cv_ref, w_ref, b_ref, o_ref):
    a = jax.nn.silu(cv_ref[...]).astype(BF16)
    o_ref[0] = _dot(a, w_ref[0].astype(BF16)) + b_ref[0]


def _modulation(cvec, w_mod, b_mod):
    tn = 1536
    n_out = 6 * D_MODEL
    out = pl.pallas_call(
        _mod_kernel,
        grid=(DEPTH, n_out // tn),
        in_specs=[pl.BlockSpec((MOD_ROWS, D_MODEL), lambda l, j: (0, 0)),
                  pl.BlockSpec((1, D_MODEL, tn), lambda l, j: (l, 0, j)),
                  pl.BlockSpec((1, 1, tn), lambda l, j: (l, 0, j))],
        out_specs=pl.BlockSpec((1, MOD_ROWS, tn), lambda l, j: (l, 0, j)),
        out_shape=jax.ShapeDtypeStruct((DEPTH, MOD_ROWS, n_out), F32),
        compiler_params=_params("parallel", "parallel"),
        name="modulation",
    )(cvec, w_mod, b_mod.reshape(DEPTH, 1, n_out))
    return out.reshape(DEPTH, MOD_ROWS, 6, D_MODEL)


def _mod_row(tile_rows, n_ctx, t_lat):
    def row(i):
        start = i * tile_rows
        return jnp.where(start < n_ctx, 0, 1 + (start - n_ctx) // t_lat)
    return row


def _proj_in_kernel(x_ref, mod_ref, nw_ref, w_ref, wg_ref, o_ref, glr_ref, h_scr):
    @pl.when(pl.program_id(1) == 0)
    def _():
        y = _rms(x_ref[...]) * nw_ref[...]
        h = (y * (1.0 + mod_ref[0, 1:2, :]) + mod_ref[0, 0:1, :]).astype(BF16)
        h_scr[...] = h
        glr_ref[...] = _dot(h, wg_ref[...])

    o_ref[...] = _dot(h_scr[...], w_ref[...])


def _proj_in(x, mod, norm_w, w_main, w_glr, row):
    n = x.shape[0]
    tm, tn = 512, 1536
    return pl.pallas_call(
        _proj_in_kernel,
        grid=(n // tm, PROJ_COLS // tn),
        in_specs=[pl.BlockSpec((tm, D_MODEL), lambda i, j: (i, 0)),
                  pl.BlockSpec((1, 6, D_MODEL), lambda i, j: (row(i), 0, 0)),
                  pl.BlockSpec((1, D_MODEL), lambda i, j: (0, 0)),
                  pl.BlockSpec((D_MODEL, tn), lambda i, j: (0, j)),
                  pl.BlockSpec((D_MODEL, LANES), lambda i, j: (0, 0))],
        out_specs=[pl.BlockSpec((tm, tn), lambda i, j: (i, j)),
                   pl.BlockSpec((tm, LANES), lambda i, j: (i, 0))],
        out_shape=[jax.ShapeDtypeStruct((n, PROJ_COLS), F32),
                   jax.ShapeDtypeStruct((n, LANES), F32)],
        scratch_shapes=[pltpu.VMEM((tm, D_MODEL), BF16)],
        compiler_params=_params("parallel", "arbitrary"),
        name="proj_in",
    )(x, mod, norm_w, w_main, w_glr)


GLA_HB = 4


def _gla_kernel(qk_ref, v_ref, glr_ref, wg_ref, bg_ref, init_ref, o_ref, st_ref, st_scr, *, n_chunks):
    d = pl.program_id(2)
    t = pl.program_id(3)
    fwd = d == 0

    @pl.when(t == 0)
    def _():
        st_scr[...] = init_ref[0, 0]

    c = GLA_CHUNK
    ri = lax.broadcasted_iota(jnp.int32, (c, c), 0)
    ci = lax.broadcasted_iota(jnp.int32, (c, c), 1)
    tri = (ri - ci) * jnp.where(fwd, 1, -1) >= 0
    tri01 = jnp.where(tri, 1.0, 0.0).astype(BF16)
    qmask = lax.broadcasted_iota(jnp.int32, (1, LANES), 1) < GLA_DK
    wg = wg_ref[0]
    bg = bg_ref[0]

    def chunk(step, carry):
        cidx = jnp.where(fwd, step, n_chunks - 1 - step)
        rows = pl.ds(pl.multiple_of(cidx * c, c), c)
        z = _dot(glr_ref[rows, :].astype(BF16), wg) + bg
        la = jax.nn.log_sigmoid(z) / GLA_TAU
        for h in range(GLA_HB):
            hl = slice(h * LANES, (h + 1) * LANES)
            qk = qk_ref[rows, hl]
            v = v_ref[rows, hl].astype(BF16)
            b = _dot_exact01(tri01, la[:, hl])
            b_mid = jnp.where(fwd, b[c // 2 - 1:c // 2, :], b[c // 2:c // 2 + 1, :])
            b_end = jnp.where(fwd, b[c - 1:c, :], b[0:1, :])
            qk1 = qk * jnp.exp(jnp.where(qmask, b - b_mid, b_mid - b))
            qk2 = qk * jnp.exp(jnp.where(qmask, b, b_end - b))
            q1 = jnp.where(qmask, qk1, 0.0).astype(BF16)
            k1 = jnp.where(qmask, pltpu.roll(qk1, GLA_DK, 1), 0.0).astype(BF16)
            s = lax.dot_general(q1, k1, NT_DIMS, preferred_element_type=F32)
            s = jnp.where(tri, s, 0.0).astype(BF16)
            o = _dot(s, v)
            st = st_scr[h]
            q2 = jnp.where(qmask, qk2, 0.0).astype(BF16)
            o = o + lax.dot_general(q2, st.astype(BF16), NT_DIMS, preferred_element_type=F32)
            k2 = jnp.where(qmask, pltpu.roll(qk2, GLA_DK, 1), 0.0).astype(BF16)
            upd = lax.dot_general(v, k2, TN_DIMS, preferred_element_type=F32)
            st_scr[h] = st * jnp.exp(b_end) + upd
            o_ref[0, rows, hl] = o
        return carry

    lax.fori_loop(0, n_chunks, chunk, 0)

    @pl.when(t == pl.num_programs(3) - 1)
    def _():
        st_ref[0, 0] = st_scr[...]


def _gla(proj, glr, wg, bg, init_t, *, row0, batch, seq, tile):
    nt = seq // tile
    rb0 = row0 // tile
    hw = GLA_HB * LANES

    def rblk(b, t, d):
        return rb0 + b * nt + jnp.where(d == 0, t, nt - 1 - t)

    return pl.pallas_call(
        functools.partial(_gla_kernel, n_chunks=tile // GLA_CHUNK),
        grid=(batch, GLA_HEADS // GLA_HB, 2, nt),
        in_specs=[pl.BlockSpec((tile, hw), lambda b, g, d, t: (rblk(b, t, d), COL_QK // hw + g)),
                  pl.BlockSpec((tile, hw), lambda b, g, d, t: (rblk(b, t, d), COL_V // hw + g)),
                  pl.BlockSpec((tile, LANES), lambda b, g, d, t: (rblk(b, t, d), 0)),
                  pl.BlockSpec((1, LANES, hw), lambda b, g, d, t: (d, 0, g)),
                  pl.BlockSpec((1, 1, hw), lambda b, g, d, t: (d, 0, g)),
                  pl.BlockSpec((1, 1, GLA_HB, LANES, LANES), lambda b, g, d, t: (b, d, g, 0, 0))],
        out_specs=[pl.BlockSpec((1, tile, hw), lambda b, g, d, t: (d, rblk(b, t, d) - rb0, g)),
                   pl.BlockSpec((1, 1, GLA_HB, LANES, LANES), lambda b, g, d, t: (b, d, g, 0, 0))],
        out_shape=[jax.ShapeDtypeStruct((2, batch * seq, GLA_HEADS * GLA_DV), F32),
                   jax.ShapeDtypeStruct((batch, 2, GLA_HEADS, LANES, LANES), F32)],
        scratch_shapes=[pltpu.VMEM((GLA_HB, LANES, LANES), F32)],
        compiler_params=_params("parallel", "parallel", "arbitrary", "arbitrary"),
        name="gla",
    )(proj, proj, glr, wg, bg, init_t)


def _prep_kernel(*refs, rope, cache):
    qa_ref, kv_ref, qw_ref, kw_ref = refs[:4]
    refs = refs[4:]
    if rope:
        cos_ref, sa_ref, sb_ref = refs[:3]
        refs = refs[3:]
    qn_ref, kn_ref, vb_ref = refs[:3]
    if cache:
        ck_ref, cv_ref = refs[3:5]

    def rot(y):
        if not rope:
            return y
        return (y * cos_ref[...] + pltpu.roll(y, LANES - 32, 1) * sa_ref[...]
                + pltpu.roll(y, 32, 1) * sb_ref[...])

    for h in range(ATT_HEADS):
        hl = slice(h * HEAD_DIM, (h + 1) * HEAD_DIM)
        qn_ref[:, hl] = rot(_rms(qa_ref[:, hl]) * qw_ref[...]).astype(BF16)
    for h in range(ATT_KV_HEADS):
        hl = slice(h * HEAD_DIM, (h + 1) * HEAD_DIM)
        vl = slice((ATT_KV_HEADS + h) * HEAD_DIM, (ATT_KV_HEADS + h + 1) * HEAD_DIM)
        kn = _rms(kv_ref[:, hl]) * kw_ref[...]
        v = kv_ref[:, vl]
        kn_ref[:, hl] = rot(kn).astype(BF16)
        vb_ref[:, hl] = v.astype(BF16)
        if cache:
            ck_ref[0, h] = kn
            cv_ref[0, h] = v


def _prep(proj, qw, kw, rope_tabs, *, row0, batch, seq, cache):
    tm = 256
    n = batch * seq
    rb0 = row0 // tm
    per_req = seq // tm
    kvw = ATT_KV_HEADS * HEAD_DIM
    rope = rope_tabs is not None
    in_specs = [pl.BlockSpec((tm, ATT_HEADS * HEAD_DIM), lambda i: (rb0 + i, COL_QA // (ATT_HEADS * HEAD_DIM))),
                pl.BlockSpec((tm, 2 * kvw), lambda i: (rb0 + i, COL_KV // (2 * kvw))),
                pl.BlockSpec((1, HEAD_DIM), lambda i: (0, 0)),
                pl.BlockSpec((1, HEAD_DIM), lambda i: (0, 0))]
    args = [proj, proj, qw, kw]
    if rope:
        in_specs += [pl.BlockSpec((tm, HEAD_DIM), lambda i: (i % per_req, 0))] * 3
        args += list(rope_tabs)
    out_specs = [pl.BlockSpec((tm, ATT_HEADS * HEAD_DIM), lambda i: (i, 0)),
                 pl.BlockSpec((tm, kvw), lambda i: (i, 0)),
                 pl.BlockSpec((tm, kvw), lambda i: (i, 0))]
    out_shape = [jax.ShapeDtypeStruct((n, ATT_HEADS * HEAD_DIM), BF16),
                 jax.ShapeDtypeStruct((n, kvw), BF16),
                 jax.ShapeDtypeStruct((n, kvw), BF16)]
    if cache:
        assert seq == tm
        out_specs += [pl.BlockSpec((1, ATT_KV_HEADS, seq, HEAD_DIM), lambda i: (i, 0, 0, 0))] * 2
        out_shape += [jax.ShapeDtypeStruct((batch, ATT_KV_HEADS, seq, HEAD_DIM), F32)] * 2
    return pl.pallas_call(
        functools.partial(_prep_kernel, rope=rope, cache=cache),
        grid=(n // tm,),
        in_specs=in_specs, out_specs=out_specs, out_shape=out_shape,
        compiler_params=_params("parallel"),
        name="attn_prep",
    )(*args)


def _rope_tables(seq):
    half = HEAD_DIM // 2
    tok = jnp.arange(seq)
    row = (tok // GRID_W).astype(F32)
    col = (tok % GRID_W).astype(F32)
    inv_freq = ROPE_THETA ** (-jnp.arange(0, half, 2, dtype=F32) / half)
    ang_r = row[:, None] * inv_freq[None, :]
    ang_c = col[:, None] * inv_freq[None, :]
    ang = jnp.concatenate([ang_r, ang_r, ang_c, ang_c], axis=-1)
    first = (jnp.arange(HEAD_DIM) % half) < half // 2
    sin = jnp.sin(ang)
    return jnp.cos(ang), jnp.where(first, -sin, 0.0), jnp.where(first, 0.0, sin)


def _attn_kernel(*refs, ctx):
    if ctx:
        q_ref, k_ref, v_ref, ck_ref, cv_ref, o_ref = refs
        ck = ck_ref[0, 0, 0].astype(BF16)
        cv = cv_ref[0, 0, 0].astype(BF16)
    else:
        q_ref, k_ref, v_ref, o_ref = refs
    k = k_ref[...]
    v = v_ref[...]
    scale = HEAD_DIM ** -0.5
    for g in range(ATT_GROUP):
        gl = slice(g * HEAD_DIM, (g + 1) * HEAD_DIM)
        q = q_ref[:, gl]
        s = lax.dot_general(q, k, NT_DIMS, preferred_element_type=F32)
        m = jnp.max(s, axis=-1, keepdims=True)
        if ctx:
            s2 = lax.dot_general(q, ck, NT_DIMS, preferred_element_type=F32)
            m = jnp.maximum(m, jnp.max(s2, axis=-1, keepdims=True))
        p = jnp.exp((s - m) * scale)
        l = jnp.sum(p, axis=-1, keepdims=True)
        acc = _dot(p.astype(BF16), v)
        if ctx:
            p2 = jnp.exp((s2 - m) * scale)
            l = l + jnp.sum(p2, axis=-1, keepdims=True)
            acc = acc + _dot(p2.astype(BF16), cv)
        o_ref[:, gl] = (acc / l).astype(BF16)


def _attention(qn, kn, vb, ctx_kv, *, batch, seq, tq):
    n = batch * seq
    nq = seq // tq
    gw = ATT_GROUP * HEAD_DIM
    in_specs = [pl.BlockSpec((tq, gw), lambda b, h, t: (b * nq + t, h)),
                pl.BlockSpec((seq, HEAD_DIM), lambda b, h, t: (b, h)),
                pl.BlockSpec((seq, HEAD_DIM), lambda b, h, t: (b, h))]
    args = [qn, kn, vb]
    if ctx_kv is not None:
        ck, cv, layer = ctx_kv
        past = ck.shape[3]
        spec = pl.BlockSpec((1, 1, 1, past, HEAD_DIM), lambda b, h, t: (b, layer, h, 0, 0))
        in_specs += [spec, spec]
        args += [ck, cv]
    return pl.pallas_call(
        functools.partial(_attn_kernel, ctx=ctx_kv is not None),
        grid=(batch, ATT_KV_HEADS, nq),
        in_specs=in_specs,
        out_specs=pl.BlockSpec((tq, gw), lambda b, h, t: (b * nq + t, h)),
        out_shape=jax.ShapeDtypeStruct((n, ATT_HEADS * HEAD_DIM), BF16),
        compiler_params=_params("parallel", "parallel", "arbitrary"),
        name="attention",
    )(*args)


def _out_proj_kernel(x_ref, od_ref, og_ref, oa_ref, gn_ref, w_ref, mod_ref, nw_ref, x1_ref, h2_ref):
    o = od_ref[0] + od_ref[1]
    parts = []
    for h in range(GLA_HEADS):
        hl = slice(h * GLA_DV, (h + 1) * GLA_DV)
        parts.append((_rms(o[:, hl]) * gn_ref[...] * jax.nn.silu(og_ref[:, hl])).astype(BF16))
    og = jnp.concatenate(parts, axis=-1)
    gw = GLA_HEADS * GLA_DV
    mix = _dot(og, w_ref[0:gw, :]) + _dot(oa_ref[...], w_ref[gw:, :])
    x1 = x_ref[...] + mod_ref[0, 2:3, :] * mix
    x1_ref[...] = x1
    h2_ref[...] = _rms(x1) * nw_ref[...] * (1.0 + mod_ref[0, 4:5, :]) + mod_ref[0, 3:4, :]


def _out_proj(x, odir, proj, oa, gla_norm, w_out, mod, norm_w, row):
    n = x.shape[0]
    tm = 256
    gw = GLA_HEADS * GLA_DV
    return pl.pallas_call(
        _out_proj_kernel,
        grid=(n // tm,),
        in_specs=[pl.BlockSpec((tm, D_MODEL), lambda i: (i, 0)),
                  pl.BlockSpec((2, tm, gw), lambda i: (0, i, 0)),
                  pl.BlockSpec((tm, gw), lambda i: (i, COL_OG // gw)),
                  pl.BlockSpec((tm, ATT_HEADS * HEAD_DIM), lambda i: (i, 0)),
                  pl.BlockSpec((1, GLA_DV), lambda i: (0, 0)),
                  pl.BlockSpec((gw + ATT_HEADS * HEAD_DIM, D_MODEL), lambda i: (0, 0)),
                  pl.BlockSpec((1, 6, D_MODEL), lambda i: (row(i), 0, 0)),
                  pl.BlockSpec((1, D_MODEL), lambda i: (0, 0))],
        out_specs=[pl.BlockSpec((tm, D_MODEL), lambda i: (i, 0)),
                   pl.BlockSpec((tm, D_MODEL), lambda i: (i, 0))],
        out_shape=[jax.ShapeDtypeStruct((n, D_MODEL), F32),
                   jax.ShapeDtypeStruct((n, D_MODEL), F32)],
        compiler_params=_params("parallel"),
        name="out_proj",
    )(x, odir, proj, oa, gla_norm, w_out, mod, norm_w)


NEG_INF = float("-inf")
N_CAND = PEER_TOPK * PEER_TOPK


def _peer_topk_kernel(h_ref, wq_ref, keys_ref, e1_ref, e2_ref, gsum_ref, gfirst_ref,
                      idx_ref, gate_ref, s_scr, v_scr, i_scr, cand_scr, cidx_scr, best_scr, eidx_scr):
    tm = h_ref.shape[0]
    q = _dot(h_ref[...].astype(BF16), wq_ref[...]).astype(BF16)
    n_half = 2 * PEER_HEADS
    for a in range(n_half):
        h, p = divmod(a, 2)
        s_scr[a] = lax.dot_general(q[:, a * LANES:(a + 1) * LANES], keys_ref[h, p], NT_DIMS,
                                   preferred_element_type=F32)
    lane = lax.broadcasted_iota(jnp.int32, (1, LANES), 1).astype(F32)
    v_scr[...] = jnp.zeros_like(v_scr)
    i_scr[...] = jnp.zeros_like(i_scr)

    def sub_topk(r, carry):
        rf = jnp.asarray(r, jnp.int32).astype(F32)
        for a in range(n_half):
            s = s_scr[a]
            m = jnp.max(s, axis=-1, keepdims=True)
            i = jnp.min(jnp.where(s == m, lane, float(LANES)), axis=-1, keepdims=True)
            s_scr[a] = jnp.where(lane == i, NEG_INF, s)
            slot = lane == rf
            v_scr[a] = jnp.where(slot, m, v_scr[a])
            i_scr[a] = jnp.where(slot, i, i_scr[a])
        return carry

    lax.fori_loop(0, PEER_TOPK, sub_topk, 0)

    e1 = e1_ref[...]
    e2 = e2_ref[...]
    for h in range(PEER_HEADS):
        cand_scr[h] = _dot_exact01_rhs(v_scr[2 * h], e1) + _dot_exact01_rhs(v_scr[2 * h + 1], e2)
        cidx_scr[h] = (_dot(i_scr[2 * h].astype(BF16), e1) * float(N_KEYS)
                       + _dot(i_scr[2 * h + 1].astype(BF16), e2))
    pos = lax.broadcasted_iota(jnp.int32, (1, N_CAND), 1).astype(F32)
    best_scr[...] = jnp.zeros_like(best_scr)
    eidx_scr[...] = jnp.zeros_like(eidx_scr)

    def cand_topk(r, carry):
        rf = jnp.asarray(r, jnp.int32).astype(F32)
        for h in range(PEER_HEADS):
            cnd = cand_scr[h]
            m = jnp.max(cnd, axis=-1, keepdims=True)
            p = jnp.min(jnp.where(cnd == m, pos, float(N_CAND)), axis=-1, keepdims=True)
            sel = pos == p
            e = jnp.max(jnp.where(sel, cidx_scr[h], -1.0), axis=-1, keepdims=True)
            cand_scr[h] = jnp.where(sel, NEG_INF, cnd)
            slot = lane == rf + float(h * PEER_TOPK)
            best_scr[...] = jnp.where(slot, m, best_scr[...])
            eidx_scr[...] = jnp.where(slot, e, eidx_scr[...])
        return carry

    lax.fori_loop(0, PEER_TOPK, cand_topk, 0)

    best = best_scr[...]
    ex = jnp.exp(best - _dot_exact01_rhs(best, gfirst_ref[...]))
    gate_ref[...] = ex / _dot_exact01_rhs(ex, gsum_ref[...])
    idx_ref[...] = eidx_scr[...].astype(jnp.int32)
    del tm


def _peer_consts():
    c = np.arange(N_CAND)
    e1 = np.zeros((LANES, N_CAND), np.float32)
    e2 = np.zeros((LANES, N_CAND), np.float32)
    e1[c // PEER_TOPK, c] = 1.0
    e2[c % PEER_TOPK, c] = 1.0
    l = np.arange(LANES)
    gsum = (l[:, None] // PEER_TOPK == l[None, :] // PEER_TOPK).astype(np.float32)
    gfirst = (l[:, None] == (l[None, :] // PEER_TOPK) * PEER_TOPK).astype(np.float32)
    return tuple(jnp.asarray(m, BF16) for m in (e1, e2, gsum, gfirst))


def _peer_topk(h2, wq, keys):
    n = h2.shape[0]
    tm = 256
    qw = PEER_HEADS * PEER_QDIM
    consts = _peer_consts()
    full = lambda shape: pl.BlockSpec(shape, lambda i: (0,) * len(shape))
    return pl.pallas_call(
        _peer_topk_kernel,
        grid=(n // tm,),
        in_specs=[pl.BlockSpec((tm, D_MODEL), lambda i: (i, 0)),
                  full((D_MODEL, qw)),
                  full((PEER_HEADS, 2, N_KEYS, PEER_QDIM // 2)),
                  full((LANES, N_CAND)), full((LANES, N_CAND)),
                  full((LANES, LANES)), full((LANES, LANES))],
        out_specs=[pl.BlockSpec((tm, PEER_SEL), lambda i: (i, 0)),
                   pl.BlockSpec((tm, PEER_SEL), lambda i: (i, 0))],
        out_shape=[jax.ShapeDtypeStruct((n, PEER_SEL), jnp.int32),
                   jax.ShapeDtypeStruct((n, PEER_SEL), F32)],
        scratch_shapes=[pltpu.VMEM((2 * PEER_HEADS, tm, LANES), F32),
                        pltpu.VMEM((2 * PEER_HEADS, tm, LANES), F32),
                        pltpu.VMEM((2 * PEER_HEADS, tm, LANES), F32),
                        pltpu.VMEM((PEER_HEADS, tm, N_CAND), F32),
                        pltpu.VMEM((PEER_HEADS, tm, N_CAND), F32),
                        pltpu.VMEM((tm, PEER_SEL), F32),
                        pltpu.VMEM((tm, PEER_SEL), F32)],
        compiler_params=_params("parallel"),
        name="peer_topk",
    )(h2, wq, keys, *consts)


PEER_TB = 64
PEER_NBUF = 4


def _peer_mix_kernel(idx_ref, gate_ref, h_ref, x_ref, mod_ref, uv_ref, o_ref, buf, sem):
    tb = h_ref.shape[0]

    def row_copy(t, k, slot):
        e = idx_ref[t, k]
        return pltpu.make_async_copy(uv_ref.at[pl.ds(e, 1), :], buf.at[slot, pl.ds(k, 1), :], sem.at[slot])

    def issue(t, slot):
        for k in range(PEER_SEL):
            row_copy(t, k, slot).start()

    def wait(slot):
        pltpu.make_async_copy(uv_ref.at[pl.ds(0, PEER_SEL), :], buf.at[slot], sem.at[slot]).wait()

    for t0 in range(PEER_NBUF - 1):
        issue(t0, t0)

    gate_t = gate_ref[...].T
    tok = lax.broadcasted_iota(jnp.int32, (1, tb), 1)
    g2 = mod_ref[0, 5:6, :]

    def token(t, carry):
        t = jnp.asarray(t, jnp.int32)
        slot = t % PEER_NBUF
        nxt = t + PEER_NBUF - 1

        @pl.when(nxt < tb)
        def _():
            issue(nxt, nxt % PEER_NBUF)

        wait(slot)
        x = h_ref[pl.ds(t, 1), :]
        u = buf[slot, :, 0:D_MODEL]
        s = jnp.sum(u * x, axis=-1, keepdims=True)
        g = jnp.sum(jnp.where(tok == t, gate_t, 0.0), axis=-1, keepdims=True)
        act = jax.nn.gelu(s) * g
        v = buf[slot, :, D_MODEL:2 * D_MODEL]
        o = jnp.sum(act * v, axis=0, keepdims=True)
        o_ref[pl.ds(t, 1), :] = x_ref[pl.ds(t, 1), :] + g2 * o
        return carry

    lax.fori_loop(0, tb, token, 0)


def _peer_mix(idx, gates, h2, x1, mod, uv, row):
    n = h2.shape[0]
    tb = PEER_TB
    return pl.pallas_call(
        _peer_mix_kernel,
        grid=(n // tb,),
        in_specs=[pl.BlockSpec((tb, PEER_SEL), lambda i: (i, 0), memory_space=pltpu.SMEM),
                  pl.BlockSpec((tb, PEER_SEL), lambda i: (i, 0)),
                  pl.BlockSpec((tb, D_MODEL), lambda i: (i, 0)),
                  pl.BlockSpec((tb, D_MODEL), lambda i: (i, 0)),
                  pl.BlockSpec((1, 6, D_MODEL), lambda i: (row(i), 0, 0)),
                  pl.BlockSpec(memory_space=pl.ANY)],
        out_specs=pl.BlockSpec((tb, D_MODEL), lambda i: (i, 0)),
        out_shape=jax.ShapeDtypeStruct((n, D_MODEL), F32),
        scratch_shapes=[pltpu.VMEM((PEER_NBUF, PEER_SEL, 2 * D_MODEL), F32),
                        pltpu.SemaphoreType.DMA((PEER_NBUF,))],
        compiler_params=_params("arbitrary"),
        name="peer_mix",
    )(idx, gates, h2, x1, mod, uv)


def _layer_weights(w_in, w_gate, b_gate, w_out, peer_wq, peer_keys, peer_u, peer_v):
    o = np.cumsum((0, GLA_HEADS * GLA_DK, GLA_HEADS * GLA_DK, GLA_HEADS * GLA_DV, 2 * GLA_RANK,
                   GLA_HEADS * GLA_DV, ATT_HEADS * HEAD_DIM, ATT_KV_HEADS * HEAD_DIM, ATT_KV_HEADS * HEAD_DIM))
    wq_g = w_in[:, o[0]:o[1]].reshape(D_MODEL, GLA_HEADS, GLA_DK) * (GLA_DK ** -0.5)
    wk_g = w_in[:, o[1]:o[2]].reshape(D_MODEL, GLA_HEADS, GLA_DK)
    w_qk = jnp.concatenate([wq_g, wk_g], axis=-1).reshape(D_MODEL, GLA_HEADS * LANES)
    w_main = jnp.concatenate([w_qk, w_in[:, o[2]:o[3]], w_in[:, o[4]:o[5]], w_in[:, o[5]:o[6]],
                              w_in[:, o[6]:o[7]], w_in[:, o[7]:o[8]]], axis=-1).astype(BF16)
    w_glr = jnp.pad(w_in[:, o[3]:o[4]], ((0, 0), (0, LANES - 2 * GLA_RANK))).astype(BF16)
    wg = w_gate.reshape(2, GLA_RANK, GLA_HEADS, GLA_DK)
    wg = jnp.concatenate([wg, wg], axis=-1).reshape(2, GLA_RANK, GLA_HEADS * LANES)
    wg = jnp.stack([jnp.pad(wg[0], ((0, LANES - GLA_RANK), (0, 0))),
                    jnp.pad(wg[1], ((GLA_RANK, LANES - 2 * GLA_RANK), (0, 0)))]).astype(BF16)
    bg = b_gate.reshape(2, GLA_HEADS, GLA_DK)
    bg = jnp.concatenate([bg, bg], axis=-1).reshape(2, 1, GLA_HEADS * LANES)
    uv = jnp.concatenate([peer_u, peer_v], axis=-1)
    return dict(w_main=w_main, w_glr=w_glr, wg=wg, bg=bg, w_out=w_out.astype(BF16),
                wq=peer_wq.astype(BF16), keys=peer_keys.astype(BF16), uv=uv)


def _state_to_kernel(s):
    st = jnp.swapaxes(s, -1, -2)
    return jnp.pad(st, [(0, 0)] * (st.ndim - 1) + [(0, LANES - GLA_DK)])


def _state_from_kernel(st):
    return jnp.swapaxes(st[..., :GLA_DK], -1, -2)


def kernel(x_prompt, x_sample, cache_k, cache_v, state_gla, c, c_ctx, w_mod, b_mod, norm_mix, norm_ffn,
           w_in, w_gate, b_gate, gla_norm, q_norm, k_norm, w_out, peer_wq, peer_keys, peer_u, peer_v):
    bc, tc, _ = x_prompt.shape
    bl, tl, _ = x_sample.shape
    n_ctx, n_lat = bc * tc, bl * tl
    assert bl + 1 <= MOD_ROWS

    cvec = jnp.concatenate([c_ctx[None, :], c, jnp.zeros((MOD_ROWS - 1 - bl, D_MODEL), F32)], axis=0)
    mod_all = _modulation(cvec, w_mod, b_mod)
    rope_tabs = _rope_tables(tl)
    x = jnp.concatenate([x_prompt.reshape(n_ctx, D_MODEL), x_sample.reshape(n_lat, D_MODEL)], axis=0)
    zero_state = jnp.zeros((bc, 2, GLA_HEADS, LANES, LANES), F32)

    ks, vs, ss = [], [], []
    for l in range(DEPTH):
        w = _layer_weights(w_in[l], w_gate[l], b_gate[l], w_out[l], peer_wq[l], peer_keys[l],
                           peer_u[l], peer_v[l])
        mod = mod_all[l]
        proj, glr = _proj_in(x, mod, norm_mix[l][None, :], w["w_main"], w["w_glr"], _mod_row(512, n_ctx, tl))

        od_c, st_c = _gla(proj, glr, w["wg"], w["bg"], zero_state, row0=0, batch=bc, seq=tc, tile=tc)
        od_l, _ = _gla(proj, glr, w["wg"], w["bg"], _state_to_kernel(state_gla[:, l]),
                       row0=n_ctx, batch=bl, seq=tl, tile=512)
        odir = jnp.concatenate([od_c, od_l], axis=1)
        ss.append(_state_from_kernel(st_c))

        qw, kw = q_norm[l][None, :], k_norm[l][None, :]
        qn_c, kn_c, vb_c, ck, cv = _prep(proj, qw, kw, None, row0=0, batch=bc, seq=tc, cache=True)
        qn_l, kn_l, vb_l = _prep(proj, qw, kw, rope_tabs, row0=n_ctx, batch=bl, seq=tl, cache=False)
        ks.append(ck)
        vs.append(cv)
        oa_c = _attention(qn_c, kn_c, vb_c, None, batch=bc, seq=tc, tq=tc)
        oa_l = _attention(qn_l, kn_l, vb_l, (cache_k, cache_v, l), batch=bl, seq=tl, tq=128)
        oa = jnp.concatenate([oa_c, oa_l], axis=0)

        x1, h2 = _out_proj(x, odir, proj, oa, gla_norm[l][None, :], w["w_out"], mod, norm_ffn[l][None, :],
                           _mod_row(256, n_ctx, tl))
        idx, gates = _peer_topk(h2, w["wq"], w["keys"])
        x = _peer_mix(idx, gates, h2, x1, mod, w["uv"], _mod_row(PEER_TB, n_ctx, tl))

    y_prompt = x[:n_ctx].reshape(bc, tc, D_MODEL)
    y_sample = x[n_ctx:].reshape(bl, tl, D_MODEL)
    return (y_prompt, y_sample, jnp.stack(ks, axis=1), jnp.stack(vs, axis=1), jnp.stack(ss, axis=1))
```

```python
import functools

import jax
import jax.numpy as jnp
import numpy as np
from jax import lax
from jax.experimental import pallas as pl
from jax.experimental.pallas import tpu as pltpu

F32 = jnp.float32
BF16 = jnp.bfloat16

D_MODEL = 2048
DEPTH = 2
GRID_W = 64
GLA_HEADS = 8
GLA_DK = 64
GLA_DV = 128
GLA_RANK = 16
GLA_TAU = 16.0
GLA_CHUNK = 64
ATT_HEADS = 8
ATT_KV_HEADS = 2
ATT_GROUP = ATT_HEADS // ATT_KV_HEADS
HEAD_DIM = 128
ROPE_THETA = 10000.0
PEER_HEADS = 8
PEER_QDIM = 256
N_KEYS = 128
PEER_TOPK = 16
PEER_SEL = PEER_HEADS * PEER_TOPK
EPS = 1e-6

LANES = 128
MOD_ROWS = 8
VMEM_LIMIT = 56 * 1024 * 1024

COL_QK = 0
COL_V = COL_QK + GLA_HEADS * LANES
COL_OG = COL_V + GLA_HEADS * GLA_DV
COL_QA = COL_OG + GLA_HEADS * GLA_DV
COL_KV = COL_QA + ATT_HEADS * HEAD_DIM
PROJ_COLS = COL_KV + 2 * ATT_KV_HEADS * HEAD_DIM

NT_DIMS = (((1,), (1,)), ((), ()))
TN_DIMS = (((0,), (0,)), ((), ()))


def _params(*sem):
    return pltpu.CompilerParams(dimension_semantics=sem, vmem_limit_bytes=VMEM_LIMIT)


def _dot(a, b):
    return jnp.dot(a, b, preferred_element_type=F32)


def _dot_exact01(a01, x):
    hi = x.astype(BF16)
    r1 = x - hi.astype(F32)
    mid = r1.astype(BF16)
    lo = (r1 - mid.astype(F32)).astype(BF16)
    return _dot(a01, hi) + _dot(a01, mid) + _dot(a01, lo)


def _dot_exact01_rhs(x, b01):
    hi = x.astype(BF16)
    r1 = x - hi.astype(F32)
    mid = r1.astype(BF16)
    lo = (r1 - mid.astype(F32)).astype(BF16)
    return _dot(hi, b01) + _dot(mid, b01) + _dot(lo, b01)


def _rms(x):
    return x * lax.rsqrt(jnp.mean(x * x, axis=-1, keepdims=True) + EPS)


def _mod_kernel(cv_ref, w_ref, b_ref, o_ref):
    a = jax.nn.silu(cv_ref[...]).astype(BF16)
    o_ref[0] = _dot(a, w_ref[0].astype(BF16)) + b_ref[0]


def _modulation(cvec, w_mod, b_mod):
    tn = 1536
    n_out = 6 * D_MODEL
    out = pl.pallas_call(
        _mod_kernel,
        grid=(DEPTH, n_out // tn),
        in_specs=[pl.BlockSpec((MOD_ROWS, D_MODEL), lambda l, j: (0, 0)),
                  pl.BlockSpec((1, D_MODEL, tn), lambda l, j: (l, 0, j)),
                  pl.BlockSpec((1, 1, tn), lambda l, j: (l, 0, j))],
        out_specs=pl.BlockSpec((1, MOD_ROWS, tn), lambda l, j: (l, 0, j)),
        out_shape=jax.ShapeDtypeStruct((DEPTH, MOD_ROWS, n_out), F32),
        compiler_params=_params("parallel", "parallel"),
        name="modulation",
    )(cvec, w_mod, b_mod.reshape(DEPTH, 1, n_out))
    return out.reshape(DEPTH, MOD_ROWS, 6, D_MODEL)


def _mod_row(tile_rows, n_ctx, t_lat):
    def row(i):
        start = i * tile_rows
        return jnp.where(start < n_ctx, 0, 1 + (start - n_ctx) // t_lat)
    return row


def _proj_in_kernel(x_ref, mod_ref, nw_ref, w_ref, wg_ref, o_ref, glr_ref, h_scr):
    @pl.when(pl.program_id(1) == 0)
    def _():
        y = _rms(x_ref[...]) * nw_ref[...]
        h = (y * (1.0 + mod_ref[0, 1:2, :]) + mod_ref[0, 0:1, :]).astype(BF16)
        h_scr[...] = h
        glr_ref[...] = _dot(h, wg_ref[...])

    o_ref[...] = _dot(h_scr[...], w_ref[...])


def _proj_in(x, mod, norm_w, w_main, w_glr, row):
    n = x.shape[0]
    tm, tn = 512, 1536
    return pl.pallas_call(
        _proj_in_kernel,
        grid=(n // tm, PROJ_COLS // tn),
        in_specs=[pl.BlockSpec((tm, D_MODEL), lambda i, j: (i, 0)),
                  pl.BlockSpec((1, 6, D_MODEL), lambda i, j: (row(i), 0, 0)),
                  pl.BlockSpec((1, D_MODEL), lambda i, j: (0, 0)),
                  pl.BlockSpec((D_MODEL, tn), lambda i, j: (0, j)),
                  pl.BlockSpec((D_MODEL, LANES), lambda i, j: (0, 0))],
        out_specs=[pl.BlockSpec((tm, tn), lambda i, j: (i, j)),
                   pl.BlockSpec((tm, LANES), lambda i, j: (i, 0))],
        out_shape=[jax.ShapeDtypeStruct((n, PROJ_COLS), F32),
                   jax.ShapeDtypeStruct((n, LANES), F32)],
        scratch_shapes=[pltpu.VMEM((tm, D_MODEL), BF16)],
        compiler_params=_params("parallel", "arbitrary"),
        name="proj_in",
    )(x, mod, norm_w, w_main, w_glr)


GLA_HB = 4


def _gla_kernel(qk_ref, v_ref, glr_ref, wg_ref, bg_ref, init_ref, o_ref, st_ref, st_scr, *, n_chunks):
    d = pl.program_id(2)
    t = pl.program_id(3)
    fwd = d == 0

    @pl.when(t == 0)
    def _():
        st_scr[...] = init_ref[0, 0]

    c = GLA_CHUNK
    ri = lax.broadcasted_iota(jnp.int32, (c, c), 0)
    ci = lax.broadcasted_iota(jnp.int32, (c, c), 1)
    tri = (ri - ci) * jnp.where(fwd, 1, -1) >= 0
    tri01 = jnp.where(tri, 1.0, 0.0).astype(BF16)
    qmask = lax.broadcasted_iota(jnp.int32, (1, LANES), 1) < GLA_DK
    wg = wg_ref[0]
    bg = bg_ref[0]

    def chunk(step, carry):
        cidx = jnp.where(fwd, step, n_chunks - 1 - step)
        rows = pl.ds(pl.multiple_of(cidx * c, c), c)
        z = _dot(glr_ref[rows, :].astype(BF16), wg) + bg
        la = jax.nn.log_sigmoid(z) / GLA_TAU
        for h in range(GLA_HB):
            hl = slice(h * LANES, (h + 1) * LANES)
            qk = qk_ref[rows, hl]
            v = v_ref[rows, hl].astype(BF16)
            b = _dot_exact01(tri01, la[:, hl])
            b_mid = jnp.where(fwd, b[c // 2 - 1:c // 2, :], b[c // 2:c // 2 + 1, :])
            b_end = jnp.where(fwd, b[c - 1:c, :], b[0:1, :])
            qk1 = qk * jnp.exp(jnp.where(qmask, b - b_mid, b_mid - b))
            qk2 = qk * jnp.exp(jnp.where(qmask, b, b_end - b))
            q1 = jnp.where(qmask, qk1, 0.0).astype(BF16)
            k1 = jnp.where(qmask, pltpu.roll(qk1, GLA_DK, 1), 0.0).astype(BF16)
            s = lax.dot_general(q1, k1, NT_DIMS, preferred_element_type=F32)
            s = jnp.where(tri, s, 0.0).astype(BF16)
            o = _dot(s, v)
            st = st_scr[h]
            q2 = jnp.where(qmask, qk2, 0.0).astype(BF16)
            o = o + lax.dot_general(q2, st.astype(BF16), NT_DIMS, preferred_element_type=F32)
            k2 = jnp.where(qmask, pltpu.roll(qk2, GLA_DK, 1), 0.0).astype(BF16)
            upd = lax.dot_general(v, k2, TN_DIMS, preferred_element_type=F32)
            st_scr[h] = st * jnp.exp(b_end) + upd
            o_ref[0, rows, hl] = o
        return carry

    lax.fori_loop(0, n_chunks, chunk, 0)

    @pl.when(t == pl.num_programs(3) - 1)
    def _():
        st_ref[0, 0] = st_scr[...]


def _gla(proj, glr, wg, bg, init_t, *, row0, batch, seq, tile):
    nt = seq // tile
    rb0 = row0 // tile
    hw = GLA_HB * LANES

    def rblk(b, t, d):
        return rb0 + b * nt + jnp.where(d == 0, t, nt - 1 - t)

    return pl.pallas_call(
        functools.partial(_gla_kernel, n_chunks=tile // GLA_CHUNK),
        grid=(batch, GLA_HEADS // GLA_HB, 2, nt),
        in_specs=[pl.BlockSpec((tile, hw), lambda b, g, d, t: (rblk(b, t, d), COL_QK // hw + g)),
                  pl.BlockSpec((tile, hw), lambda b, g, d, t: (rblk(b, t, d), COL_V // hw + g)),
                  pl.BlockSpec((tile, LANES), lambda b, g, d, t: (rblk(b, t, d), 0)),
                  pl.BlockSpec((1, LANES, hw), lambda b, g, d, t: (d, 0, g)),
                  pl.BlockSpec((1, 1, hw), lambda b, g, d, t: (d, 0, g)),
                  pl.BlockSpec((1, 1, GLA_HB, LANES, LANES), lambda b, g, d, t: (b, d, g, 0, 0))],
        out_specs=[pl.BlockSpec((1, tile, hw), lambda b, g, d, t: (d, rblk(b, t, d) - rb0, g)),
                   pl.BlockSpec((1, 1, GLA_HB, LANES, LANES), lambda b, g, d, t: (b, d, g, 0, 0))],
        out_shape=[jax.ShapeDtypeStruct((2, batch * seq, GLA_HEADS * GLA_DV), F32),
                   jax.ShapeDtypeStruct((batch, 2, GLA_HEADS, LANES, LANES), F32)],
        scratch_shapes=[pltpu.VMEM((GLA_HB, LANES, LANES), F32)],
        compiler_params=_params("parallel", "parallel", "arbitrary", "arbitrary"),
        name="gla",
    )(proj, proj, glr, wg, bg, init_t)


def _prep_kernel(*refs, rope, cache):
    qa_ref, kv_ref, qw_ref, kw_ref = refs[:4]
    refs = refs[4:]
    if rope:
        cos_ref, sa_ref, sb_ref = refs[:3]
        refs = refs[3:]
    qn_ref, kn_ref, vb_ref = refs[:3]
    if cache:
        ck_ref, cv_ref = refs[3:5]

    def rot(y):
        if not rope:
            return y
        return (y * cos_ref[...] + pltpu.roll(y, LANES - 32, 1) * sa_ref[...]
                + pltpu.roll(y, 32, 1) * sb_ref[...])

    for h in range(ATT_HEADS):
        hl = slice(h * HEAD_DIM, (h + 1) * HEAD_DIM)
        qn_ref[:, hl] = rot(_rms(qa_ref[:, hl]) * qw_ref[...]).astype(BF16)
    for h in range(ATT_KV_HEADS):
        hl = slice(h * HEAD_DIM, (h + 1) * HEAD_DIM)
        vl = slice((ATT_KV_HEADS + h) * HEAD_DIM, (ATT_KV_HEADS + h + 1) * HEAD_DIM)
        kn = _rms(kv_ref[:, hl]) * kw_ref[...]
        v = kv_ref[:, vl]
        kn_ref[:, hl] = rot(kn).astype(BF16)
        vb_ref[:, hl] = v.astype(BF16)
        if cache:
            ck_ref[0, h] = kn
            cv_ref[0, h] = v


def _prep(proj, qw, kw, rope_tabs, *, row0, batch, seq, cache):
    tm = 256
    n = batch * seq
    rb0 = row0 // tm
    per_req = seq // tm
    kvw = ATT_KV_HEADS * HEAD_DIM
    rope = rope_tabs is not None
    in_specs = [pl.BlockSpec((tm, ATT_HEADS * HEAD_DIM), lambda i: (rb0 + i, COL_QA // (ATT_HEADS * HEAD_DIM))),
                pl.BlockSpec((tm, 2 * kvw), lambda i: (rb0 + i, COL_KV // (2 * kvw))),
                pl.BlockSpec((1, HEAD_DIM), lambda i: (0, 0)),
                pl.BlockSpec((1, HEAD_DIM), lambda i: (0, 0))]
    args = [proj, proj, qw, kw]
    if rope:
        in_specs += [pl.BlockSpec((tm, HEAD_DIM), lambda i: (i % per_req, 0))] * 3
        args += list(rope_tabs)
    out_specs = [pl.BlockSpec((tm, ATT_HEADS * HEAD_DIM), lambda i: (i, 0)),
                 pl.BlockSpec((tm, kvw), lambda i: (i, 0)),
                 pl.BlockSpec((tm, kvw), lambda i: (i, 0))]
    out_shape = [jax.ShapeDtypeStruct((n, ATT_HEADS * HEAD_DIM), BF16),
                 jax.ShapeDtypeStruct((n, kvw), BF16),
                 jax.ShapeDtypeStruct((n, kvw), BF16)]
    if cache:
        assert seq == tm
        out_specs += [pl.BlockSpec((1, ATT_KV_HEADS, seq, HEAD_DIM), lambda i: (i, 0, 0, 0))] * 2
        out_shape += [jax.ShapeDtypeStruct((batch, ATT_KV_HEADS, seq, HEAD_DIM), F32)] * 2
    return pl.pallas_call(
        functools.partial(_prep_kernel, rope=rope, cache=cache),
        grid=(n // tm,),
        in_specs=in_specs, out_specs=out_specs, out_shape=out_shape,
        compiler_params=_params("parallel"),
        name="attn_prep",
    )(*args)


def _rope_tables(seq):
    half = HEAD_DIM // 2
    tok = jnp.arange(seq)
    row = (tok // GRID_W).astype(F32)
    col = (tok % GRID_W).astype(F32)
    inv_freq = ROPE_THETA ** (-jnp.arange(0, half, 2, dtype=F32) / half)
    ang_r = row[:, None] * inv_freq[None, :]
    ang_c = col[:, None] * inv_freq[None, :]
    ang = jnp.concatenate([ang_r, ang_r, ang_c, ang_c], axis=-1)
    first = (jnp.arange(HEAD_DIM) % half) < half // 2
    sin = jnp.sin(ang)
    return jnp.cos(ang), jnp.where(first, -sin, 0.0), jnp.where(first, 0.0, sin)


def _attn_kernel(*refs, ctx):
    if ctx:
        q_ref, k_ref, v_ref, ck_ref, cv_ref, o_ref = refs
        ck = ck_ref[0, 0, 0].astype(BF16)
        cv = cv_ref[0, 0, 0].astype(BF16)
    else:
        q_ref, k_ref, v_ref, o_ref = refs
    k = k_ref[...]
    v = v_ref[...]
    scale = HEAD_DIM ** -0.5
    for g in range(ATT_GROUP):
        gl = slice(g * HEAD_DIM, (g + 1) * HEAD_DIM)
        q = q_ref[:, gl]
        s = lax.dot_general(q, k, NT_DIMS, preferred_element_type=F32)
        m = jnp.max(s, axis=-1, keepdims=True)
        if ctx:
            s2 = lax.dot_general(q, ck, NT_DIMS, preferred_element_type=F32)
            m = jnp.maximum(m, jnp.max(s2, axis=-1, keepdims=True))
        p = jnp.exp((s - m) * scale)
        l = jnp.sum(p, axis=-1, keepdims=True)
        acc = _dot(p.astype(BF16), v)
        if ctx:
            p2 = jnp.exp((s2 - m) * scale)
            l = l + jnp.sum(p2, axis=-1, keepdims=True)
            acc = acc + _dot(p2.astype(BF16), cv)
        o_ref[:, gl] = (acc / l).astype(BF16)


def _attention(qn, kn, vb, ctx_kv, *, batch, seq, tq):
    n = batch * seq
    nq = seq // tq
    gw = ATT_GROUP * HEAD_DIM
    in_specs = [pl.BlockSpec((tq, gw), lambda b, h, t: (b * nq + t, h)),
                pl.BlockSpec((seq, HEAD_DIM), lambda b, h, t: (b, h)),
                pl.BlockSpec((seq, HEAD_DIM), lambda b, h, t: (b, h))]
    args = [qn, kn, vb]
    if ctx_kv is not None:
        ck, cv, layer = ctx_kv
        past = ck.shape[3]
        spec = pl.BlockSpec((1, 1, 1, past, HEAD_DIM), lambda b, h, t: (b, layer, h, 0, 0))
        in_specs += [spec, spec]
        args += [ck, cv]
    return pl.pallas_call(
        functools.partial(_attn_kernel, ctx=ctx_kv is not None),
        grid=(batch, ATT_KV_HEADS, nq),
        in_specs=in_specs,
        out_specs=pl.BlockSpec((tq, gw), lambda b, h, t: (b * nq + t, h)),
        out_shape=jax.ShapeDtypeStruct((n, ATT_HEADS * HEAD_DIM), BF16),
        compiler_params=_params("parallel", "parallel", "arbitrary"),
        name="attention",
    )(*args)


def _out_proj_kernel(x_ref, od_ref, og_ref, oa_ref, gn_ref, w_ref, mod_ref, nw_ref, x1_ref, h2_ref):
    o = od_ref[0] + od_ref[1]
    parts = []
    for h in range(GLA_HEADS):
        hl = slice(h * GLA_DV, (h + 1) * GLA_DV)
        parts.append((_rms(o[:, hl]) * gn_ref[...] * jax.nn.silu(og_ref[:, hl])).astype(BF16))
    og = jnp.concatenate(parts, axis=-1)
    gw = GLA_HEADS * GLA_DV
    mix = _dot(og, w_ref[0:gw, :]) + _dot(oa_ref[...], w_ref[gw:, :])
    x1 = x_ref[...] + mod_ref[0, 2:3, :] * mix
    x1_ref[...] = x1
    h2_ref[...] = _rms(x1) * nw_ref[...] * (1.0 + mod_ref[0, 4:5, :]) + mod_ref[0, 3:4, :]


def _out_proj(x, odir, proj, oa, gla_norm, w_out, mod, norm_w, row):
    n = x.shape[0]
    tm = 256
    gw = GLA_HEADS * GLA_DV
    return pl.pallas_call(
        _out_proj_kernel,
        grid=(n // tm,),
        in_specs=[pl.BlockSpec((tm, D_MODEL), lambda i: (i, 0)),
                  pl.BlockSpec((2, tm, gw), lambda i: (0, i, 0)),
                  pl.BlockSpec((tm, gw), lambda i: (i, COL_OG // gw)),
                  pl.BlockSpec((tm, ATT_HEADS * HEAD_DIM), lambda i: (i, 0)),
                  pl.BlockSpec((1, GLA_DV), lambda i: (0, 0)),
                  pl.BlockSpec((gw + ATT_HEADS * HEAD_DIM, D_MODEL), lambda i: (0, 0)),
                  pl.BlockSpec((1, 6, D_MODEL), lambda i: (row(i), 0, 0)),
                  pl.BlockSpec((1, D_MODEL), lambda i: (0, 0))],
        out_specs=[pl.BlockSpec((tm, D_MODEL), lambda i: (i, 0)),
                   pl.BlockSpec((tm, D_MODEL), lambda i: (i, 0))],
        out_shape=[jax.ShapeDtypeStruct((n, D_MODEL), F32),
                   jax.ShapeDtypeStruct((n, D_MODEL), F32)],
        compiler_params=_params("parallel"),
        name="out_proj",
    )(x, odir, proj, oa, gla_norm, w_out, mod, norm_w)


NEG_INF = float("-inf")
N_CAND = PEER_TOPK * PEER_TOPK


_CAND_GROUPS = ((0, 0), (0, 8), (1, 0), (2, 0), (3, 0), (4, 0), (5, 0), (6, 0), (7, 0))
N_CAND_ROWS = 8 * (len(_CAND_GROUPS) + 1)


def _cand_consts():
    pos = np.zeros((N_CAND_ROWS, 1), np.float32)
    off = np.zeros((N_CAND_ROWS, 1), np.float32)
    for g, (a, b0) in enumerate(_CAND_GROUPS):
        for j in range(8):
            pos[8 * g + j] = a * PEER_TOPK + b0 + j
            off[8 * g + j] = 0.0 if (a + 1) * (b0 + j + 1) <= PEER_TOPK else NEG_INF
    for j in range(8):
        pos[N_CAND_ROWS - 8 + j] = (8 + j) * PEER_TOPK
    return (jnp.asarray(np.broadcast_to(pos, (N_CAND_ROWS, LANES))),
            jnp.asarray(np.broadcast_to(off, (N_CAND_ROWS, LANES))))


def _pair_rows(first, second):
    rows = [first[a:a + 1, :] + second[b0:b0 + 8, :] for a, b0 in _CAND_GROUPS]
    rows.append(first[8:16, :] + second[0:1, :])
    return jnp.concatenate(rows, axis=0)


def _top16_rows(s, row_id):
    rank = lax.broadcasted_iota(jnp.int32, (PEER_TOPK, s.shape[1]), 0)
    vals = jnp.zeros((PEER_TOPK, s.shape[1]), F32)
    ids = jnp.zeros((PEER_TOPK, s.shape[1]), F32)
    for r in range(PEER_TOPK):
        m = jnp.max(s, axis=0, keepdims=True)
        i = jnp.min(jnp.where(s == m, row_id, float(N_CAND)), axis=0, keepdims=True)
        s = jnp.where(row_id == i, NEG_INF, s)
        vals = jnp.where(rank == r, m, vals)
        ids = jnp.where(rank == r, i, ids)
    return vals, ids


def _peer_topk_kernel(h_ref, wq_ref, keys_ref, pos_ref, off_ref, idx_ref, gate_ref, q_scr):
    tm = h_ref.shape[0]
    q = _dot(h_ref[...].astype(BF16), wq_ref[...]).astype(BF16)
    for a in range(2 * PEER_HEADS):
        q_scr[a] = q[:, a * LANES:(a + 1) * LANES]
    key_id = lax.broadcasted_iota(jnp.int32, (N_KEYS, LANES), 0).astype(F32)
    rank = lax.broadcasted_iota(jnp.int32, (PEER_TOPK, LANES), 0)
    pos = pos_ref[...]
    off = off_ref[...]

    def head(h, carry):
        out_rows = pl.ds(pl.multiple_of(h * PEER_TOPK, PEER_TOPK), PEER_TOPK)
        for c in range(tm // LANES):
            cols = slice(c * LANES, (c + 1) * LANES)
            s1 = lax.dot_general(keys_ref[h, 0], q_scr[2 * h, cols, :], NT_DIMS, preferred_element_type=F32)
            s2 = lax.dot_general(keys_ref[h, 1], q_scr[2 * h + 1, cols, :], NT_DIMS, preferred_element_type=F32)
            v1, i1 = _top16_rows(s1, key_id)
            v2, i2 = _top16_rows(s2, key_id)
            cand = _pair_rows(v1, v2) + off
            cidx = _pair_rows(i1 * float(N_KEYS), i2)
            best = jnp.zeros((PEER_TOPK, LANES), F32)
            eidx = jnp.zeros((PEER_TOPK, LANES), F32)
            for r in range(PEER_TOPK):
                m = jnp.max(cand, axis=0, keepdims=True)
                p = jnp.min(jnp.where(cand == m, pos, float(N_CAND)), axis=0, keepdims=True)
                sel = pos == p
                e = jnp.max(jnp.where(sel, cidx, -1.0), axis=0, keepdims=True)
                cand = jnp.where(sel, NEG_INF, cand)
                best = jnp.where(rank == r, m, best)
                eidx = jnp.where(rank == r, e, eidx)
            ex = jnp.exp(best - best[0:1, :])
            gate_ref[out_rows, cols] = ex / jnp.sum(ex, axis=0, keepdims=True)
            idx_ref[out_rows, cols] = eidx.astype(jnp.int32)
        return carry

    lax.fori_loop(0, PEER_HEADS, head, 0)


def _peer_topk(h2, wq, keys):
    n = h2.shape[0]
    tm = 256
    qw = PEER_HEADS * PEER_QDIM
    full = lambda shape: pl.BlockSpec(shape, lambda i: (0,) * len(shape))
    return pl.pallas_call(
        _peer_topk_kernel,
        grid=(n // tm,),
        in_specs=[pl.BlockSpec((tm, D_MODEL), lambda i: (i, 0)),
                  full((D_MODEL, qw)),
                  full((PEER_HEADS, 2, N_KEYS, PEER_QDIM // 2)),
                  full((N_CAND_ROWS, LANES)), full((N_CAND_ROWS, LANES))],
        out_specs=[pl.BlockSpec((PEER_SEL, tm), lambda i: (0, i)),
                   pl.BlockSpec((PEER_SEL, tm), lambda i: (0, i))],
        out_shape=[jax.ShapeDtypeStruct((PEER_SEL, n), jnp.int32),
                   jax.ShapeDtypeStruct((PEER_SEL, n), F32)],
        scratch_shapes=[pltpu.VMEM((2 * PEER_HEADS, tm, LANES), BF16)],
        compiler_params=_params("parallel"),
        name="peer_topk",
    )(h2, wq, keys, *_cand_consts())


PEER_TB = 128
PEER_NBUF = 8


def _peer_mix_kernel(idx_ref, gate_ref, h_ref, x_ref, mod_ref, uv_ref, o_ref, *scratch):
    bufs, sem = scratch[:PEER_NBUF], scratch[PEER_NBUF]
    tb = h_ref.shape[0]
    n_groups = tb // PEER_NBUF

    def issue(t, slot):
        for k in range(PEER_SEL):
            pltpu.make_async_copy(uv_ref.at[idx_ref[k, t]], bufs[slot].at[pl.ds(k, 1), :], sem.at[slot]).start()

    def wait(slot):
        pltpu.make_async_copy(bufs[slot], bufs[slot], sem.at[slot]).wait()

    gate_t = gate_ref[...]
    tok = lax.broadcasted_iota(jnp.int32, (1, tb), 1)
    g2 = mod_ref[0, 5:6, :]

    def token(t, slot, prefetch):
        wait(slot)
        if prefetch:
            issue(t + PEER_NBUF - 1, (slot + PEER_NBUF - 1) % PEER_NBUF)
        x = h_ref[pl.ds(t, 1), :]
        s = jnp.sum(bufs[slot][:, 0:D_MODEL] * x, axis=-1, keepdims=True)
        g = jnp.sum(jnp.where(tok == t, gate_t, 0.0), axis=-1, keepdims=True)
        act = jax.nn.gelu(s) * g
        o = jnp.sum(act * bufs[slot][:, D_MODEL:2 * D_MODEL], axis=0, keepdims=True)
        o_ref[pl.ds(t, 1), :] = x_ref[pl.ds(t, 1), :] + g2 * o

    for t0 in range(PEER_NBUF - 1):
        issue(t0, t0)

    def group(g, carry):
        for j in range(PEER_NBUF):
            token(g * PEER_NBUF + j, j, True)
        return carry

    lax.fori_loop(0, n_groups - 1, group, 0)
    for j in range(PEER_NBUF):
        token((n_groups - 1) * PEER_NBUF + j, j, j == 0)


def _peer_mix(idx_t, gates_t, h2, x1, mod, uv, row):
    n = h2.shape[0]
    tb = PEER_TB
    return pl.pallas_call(
        _peer_mix_kernel,
        grid=(n // tb,),
        in_specs=[pl.BlockSpec((PEER_SEL, tb), lambda i: (0, i), memory_space=pltpu.SMEM),
                  pl.BlockSpec((PEER_SEL, tb), lambda i: (0, i)),
                  pl.BlockSpec((tb, D_MODEL), lambda i: (i, 0)),
                  pl.BlockSpec((tb, D_MODEL), lambda i: (i, 0)),
                  pl.BlockSpec((1, 6, D_MODEL), lambda i: (row(i), 0, 0)),
                  pl.BlockSpec(memory_space=pl.ANY)],
        out_specs=pl.BlockSpec((tb, D_MODEL), lambda i: (i, 0)),
        out_shape=jax.ShapeDtypeStruct((n, D_MODEL), F32),
        scratch_shapes=[pltpu.VMEM((PEER_SEL, 2 * D_MODEL), F32)] * PEER_NBUF
                       + [pltpu.SemaphoreType.DMA((PEER_NBUF,))],
        compiler_params=_params("arbitrary"),
        name="peer_mix",
    )(idx_t, gates_t, h2, x1, mod, uv)


def _layer_weights(w_in, w_gate, b_gate, w_out, peer_wq, peer_keys, peer_u, peer_v):
    o = np.cumsum((0, GLA_HEADS * GLA_DK, GLA_HEADS * GLA_DK, GLA_HEADS * GLA_DV, 2 * GLA_RANK,
                   GLA_HEADS * GLA_DV, ATT_HEADS * HEAD_DIM, ATT_KV_HEADS * HEAD_DIM, ATT_KV_HEADS * HEAD_DIM))
    wq_g = w_in[:, o[0]:o[1]].reshape(D_MODEL, GLA_HEADS, GLA_DK) * (GLA_DK ** -0.5)
    wk_g = w_in[:, o[1]:o[2]].reshape(D_MODEL, GLA_HEADS, GLA_DK)
    w_qk = jnp.concatenate([wq_g, wk_g], axis=-1).reshape(D_MODEL, GLA_HEADS * LANES)
    w_main = jnp.concatenate([w_qk, w_in[:, o[2]:o[3]], w_in[:, o[4]:o[5]], w_in[:, o[5]:o[6]],
                              w_in[:, o[6]:o[7]], w_in[:, o[7]:o[8]]], axis=-1).astype(BF16)
    w_glr = jnp.pad(w_in[:, o[3]:o[4]], ((0, 0), (0, LANES - 2 * GLA_RANK))).astype(BF16)
    wg = w_gate.reshape(2, GLA_RANK, GLA_HEADS, GLA_DK)
    wg = jnp.concatenate([wg, wg], axis=-1).reshape(2, GLA_RANK, GLA_HEADS * LANES)
    wg = jnp.stack([jnp.pad(wg[0], ((0, LANES - GLA_RANK), (0, 0))),
                    jnp.pad(wg[1], ((GLA_RANK, LANES - 2 * GLA_RANK), (0, 0)))]).astype(BF16)
    bg = b_gate.reshape(2, GLA_HEADS, GLA_DK)
    bg = jnp.concatenate([bg, bg], axis=-1).reshape(2, 1, GLA_HEADS * LANES)
    uv = jnp.concatenate([peer_u, peer_v], axis=-1).reshape(peer_u.shape[0], 1, 2 * D_MODEL)
    return dict(w_main=w_main, w_glr=w_glr, wg=wg, bg=bg, w_out=w_out.astype(BF16),
                wq=peer_wq.astype(BF16), keys=peer_keys.astype(BF16), uv=uv)


def _state_to_kernel(s):
    st = jnp.swapaxes(s, -1, -2)
    return jnp.pad(st, [(0, 0)] * (st.ndim - 1) + [(0, LANES - GLA_DK)])


def _state_from_kernel(st):
    return jnp.swapaxes(st[..., :GLA_DK], -1, -2)


def kernel(x_prompt, x_sample, cache_k, cache_v, state_gla, c, c_ctx, w_mod, b_mod, norm_mix, norm_ffn,
           w_in, w_gate, b_gate, gla_norm, q_norm, k_norm, w_out, peer_wq, peer_keys, peer_u, peer_v):
    bc, tc, _ = x_prompt.shape
    bl, tl, _ = x_sample.shape
    n_ctx, n_lat = bc * tc, bl * tl
    assert bl + 1 <= MOD_ROWS

    cvec = jnp.concatenate([c_ctx[None, :], c, jnp.zeros((MOD_ROWS - 1 - bl, D_MODEL), F32)], axis=0)
    mod_all = _modulation(cvec, w_mod, b_mod)
    rope_tabs = _rope_tables(tl)
    x = jnp.concatenate([x_prompt.reshape(n_ctx, D_MODEL), x_sample.reshape(n_lat, D_MODEL)], axis=0)
    zero_state = jnp.zeros((bc, 2, GLA_HEADS, LANES, LANES), F32)

    ks, vs, ss = [], [], []
    for l in range(DEPTH):
        w = _layer_weights(w_in[l], w_gate[l], b_gate[l], w_out[l], peer_wq[l], peer_keys[l],
                           peer_u[l], peer_v[l])
        mod = mod_all[l]
        proj, glr = _proj_in(x, mod, norm_mix[l][None, :], w["w_main"], w["w_glr"], _mod_row(512, n_ctx, tl))

        od_c, st_c = _gla(proj, glr, w["wg"], w["bg"], zero_state, row0=0, batch=bc, seq=tc, tile=tc)
        od_l, _ = _gla(proj, glr, w["wg"], w["bg"], _state_to_kernel(state_gla[:, l]),
                       row0=n_ctx, batch=bl, seq=tl, tile=512)
        odir = jnp.concatenate([od_c, od_l], axis=1)
        ss.append(_state_from_kernel(st_c))

        qw, kw = q_norm[l][None, :], k_norm[l][None, :]
        qn_c, kn_c, vb_c, ck, cv = _prep(proj, qw, kw, None, row0=0, batch=bc, seq=tc, cache=True)
        qn_l, kn_l, vb_l = _prep(proj, qw, kw, rope_tabs, row0=n_ctx, batch=bl, seq=tl, cache=False)
        ks.append(ck)
        vs.append(cv)
        oa_c = _attention(qn_c, kn_c, vb_c, None, batch=bc, seq=tc, tq=tc)
        oa_l = _attention(qn_l, kn_l, vb_l, (cache_k, cache_v, l), batch=bl, seq=tl, tq=128)
        oa = jnp.concatenate([oa_c, oa_l], axis=0)

        x1, h2 = _out_proj(x, odir, proj, oa, gla_norm[l][None, :], w["w_out"], mod, norm_ffn[l][None, :],
                           _mod_row(256, n_ctx, tl))
        idx, gates = _peer_topk(h2, w["wq"], w["keys"])
        x = _peer_mix(idx, gates, h2, x1, mod, w["uv"], _mod_row(PEER_TB, n_ctx, tl))

    y_prompt = x[:n_ctx].reshape(bc, tc, D_MODEL)
    y_sample = x[n_ctx:].reshape(bl, tl, D_MODEL)
    return (y_prompt, y_sample, jnp.stack(ks, axis=1), jnp.stack(vs, axis=1), jnp.stack(ss, axis=1))
```

```python
import functools

import jax
import jax.numpy as jnp
import numpy as np
from jax import lax
from jax.experimental import pallas as pl
from jax.experimental.pallas import tpu as pltpu

F32 = jnp.float32
BF16 = jnp.bfloat16

D_MODEL = 2048
DEPTH = 2
GRID_W = 64
GLA_HEADS = 8
GLA_DK = 64
GLA_DV = 128
GLA_RANK = 16
GLA_TAU = 16.0
GLA_CHUNK = 64
ATT_HEADS = 8
ATT_KV_HEADS = 2
ATT_GROUP = ATT_HEADS // ATT_KV_HEADS
HEAD_DIM = 128
ROPE_THETA = 10000.0
PEER_HEADS = 8
PEER_QDIM = 256
N_KEYS = 128
PEER_TOPK = 16
PEER_SEL = PEER_HEADS * PEER_TOPK
EPS = 1e-6

LANES = 128
MOD_ROWS = 8
VMEM_LIMIT = 56 * 1024 * 1024

COL_QK = 0
COL_V = COL_QK + GLA_HEADS * LANES
COL_OG = COL_V + GLA_HEADS * GLA_DV
COL_QA = COL_OG + GLA_HEADS * GLA_DV
COL_KV = COL_QA + ATT_HEADS * HEAD_DIM
PROJ_COLS = COL_KV + 2 * ATT_KV_HEADS * HEAD_DIM

NT_DIMS = (((1,), (1,)), ((), ()))
TN_DIMS = (((0,), (0,)), ((), ()))


def _params(*sem):
    return pltpu.CompilerParams(dimension_semantics=sem, vmem_limit_bytes=VMEM_LIMIT)


def _dot(a, b):
    return jnp.dot(a, b, preferred_element_type=F32)


def _dot_exact01(a01, x):
    hi = x.astype(BF16)
    r1 = x - hi.astype(F32)
    mid = r1.astype(BF16)
    lo = (r1 - mid.astype(F32)).astype(BF16)
    return _dot(a01, hi) + _dot(a01, mid) + _dot(a01, lo)


def _dot_exact01_rhs(x, b01):
    hi = x.astype(BF16)
    r1 = x - hi.astype(F32)
    mid = r1.astype(BF16)
    lo = (r1 - mid.astype(F32)).astype(BF16)
    return _dot(hi, b01) + _dot(mid, b01) + _dot(lo, b01)


def _rms(x):
    return x * lax.rsqrt(jnp.mean(x * x, axis=-1, keepdims=True) + EPS)


def _mod_kernel(cv_ref, w_ref, b_ref, o_ref):
    a = jax.nn.silu(cv_ref[...]).astype(BF16)
    o_ref[0] = _dot(a, w_ref[0].astype(BF16)) + b_ref[0]


def _modulation(cvec, w_mod, b_mod):
    tn = 1536
    n_out = 6 * D_MODEL
    out = pl.pallas_call(
        _mod_kernel,
        grid=(DEPTH, n_out // tn),
        in_specs=[pl.BlockSpec((MOD_ROWS, D_MODEL), lambda l, j: (0, 0)),
                  pl.BlockSpec((1, D_MODEL, tn), lambda l, j: (l, 0, j)),
                  pl.BlockSpec((1, 1, tn), lambda l, j: (l, 0, j))],
        out_specs=pl.BlockSpec((1, MOD_ROWS, tn), lambda l, j: (l, 0, j)),
        out_shape=jax.ShapeDtypeStruct((DEPTH, MOD_ROWS, n_out), F32),
        compiler_params=_params("parallel", "parallel"),
        name="modulation",
    )(cvec, w_mod, b_mod.reshape(DEPTH, 1, n_out))
    return out.reshape(DEPTH, MOD_ROWS, 6, D_MODEL)


def _mod_row(tile_rows, n_ctx, t_lat):
    def row(i):
        start = i * tile_rows
        return jnp.where(start < n_ctx, 0, 1 + (start - n_ctx) // t_lat)
    return row


def _proj_in_kernel(x_ref, mod_ref, nw_ref, w_ref, wg_ref, o_ref, glr_ref, h_scr):
    @pl.when(pl.program_id(1) == 0)
    def _():
        y = _rms(x_ref[...]) * nw_ref[...]
        h = (y * (1.0 + mod_ref[0, 1:2, :]) + mod_ref[0, 0:1, :]).astype(BF16)
        h_scr[...] = h
        glr_ref[...] = _dot(h, wg_ref[...])

    o_ref[...] = _dot(h_scr[...], w_ref[...])


def _proj_in(x, mod, norm_w, w_main, w_glr, row):
    n = x.shape[0]
    tm, tn = 512, 1536
    return pl.pallas_call(
        _proj_in_kernel,
        grid=(n // tm, PROJ_COLS // tn),
        in_specs=[pl.BlockSpec((tm, D_MODEL), lambda i, j: (i, 0)),
                  pl.BlockSpec((1, 6, D_MODEL), lambda i, j: (row(i), 0, 0)),
                  pl.BlockSpec((1, D_MODEL), lambda i, j: (0, 0)),
                  pl.BlockSpec((D_MODEL, tn), lambda i, j: (0, j)),
                  pl.BlockSpec((D_MODEL, LANES), lambda i, j: (0, 0))],
        out_specs=[pl.BlockSpec((tm, tn), lambda i, j: (i, j)),
                   pl.BlockSpec((tm, LANES), lambda i, j: (i, 0))],
        out_shape=[jax.ShapeDtypeStruct((n, PROJ_COLS), F32),
                   jax.ShapeDtypeStruct((n, LANES), F32)],
        scratch_shapes=[pltpu.VMEM((tm, D_MODEL), BF16)],
        compiler_params=_params("parallel", "arbitrary"),
        name="proj_in",
    )(x, mod, norm_w, w_main, w_glr)


GLA_HB = 4


def _gla_kernel(qk_ref, v_ref, glr_ref, wg_ref, bg_ref, init_ref, o_ref, st_ref, st_scr, *, n_chunks):
    d = pl.program_id(2)
    t = pl.program_id(3)
    fwd = d == 0

    @pl.when(t == 0)
    def _():
        st_scr[...] = init_ref[0, 0]

    c = GLA_CHUNK
    ri = lax.broadcasted_iota(jnp.int32, (c, c), 0)
    ci = lax.broadcasted_iota(jnp.int32, (c, c), 1)
    tri = (ri - ci) * jnp.where(fwd, 1, -1) >= 0
    tri01 = jnp.where(tri, 1.0, 0.0).astype(BF16)
    qmask = lax.broadcasted_iota(jnp.int32, (1, LANES), 1) < GLA_DK
    wg = wg_ref[0]
    bg = bg_ref[0]

    def chunk(step, carry):
        cidx = jnp.where(fwd, step, n_chunks - 1 - step)
        rows = pl.ds(pl.multiple_of(cidx * c, c), c)
        z = _dot(glr_ref[rows, :].astype(BF16), wg) + bg
        la = jax.nn.log_sigmoid(z) / GLA_TAU
        for h in range(GLA_HB):
            hl = slice(h * LANES, (h + 1) * LANES)
            qk = qk_ref[rows, hl]
            v = v_ref[rows, hl].astype(BF16)
            b = _dot_exact01(tri01, la[:, hl])
            b_mid = jnp.where(fwd, b[c // 2 - 1:c // 2, :], b[c // 2:c // 2 + 1, :])
            b_end = jnp.where(fwd, b[c - 1:c, :], b[0:1, :])
            qk1 = qk * jnp.exp(jnp.where(qmask, b - b_mid, b_mid - b))
            qk2 = qk * jnp.exp(jnp.where(qmask, b, b_end - b))
            q1 = jnp.where(qmask, qk1, 0.0).astype(BF16)
            k1 = jnp.where(qmask, pltpu.roll(qk1, GLA_DK, 1), 0.0).astype(BF16)
            s = lax.dot_general(q1, k1, NT_DIMS, preferred_element_type=F32)
            s = jnp.where(tri, s, 0.0).astype(BF16)
            o = _dot(s, v)
            st = st_scr[h]
            q2 = jnp.where(qmask, qk2, 0.0).astype(BF16)
            o = o + lax.dot_general(q2, st.astype(BF16), NT_DIMS, preferred_element_type=F32)
            k2 = jnp.where(qmask, pltpu.roll(qk2, GLA_DK, 1), 0.0).astype(BF16)
            upd = lax.dot_general(v, k2, TN_DIMS, preferred_element_type=F32)
            st_scr[h] = st * jnp.exp(b_end) + upd
            o_ref[0, rows, hl] = o
        return carry

    lax.fori_loop(0, n_chunks, chunk, 0)

    @pl.when(t == pl.num_programs(3) - 1)
    def _():
        st_ref[0, 0] = st_scr[...]


def _gla(proj, glr, wg, bg, init_t, *, row0, batch, seq, tile):
    nt = seq // tile
    rb0 = row0 // tile
    hw = GLA_HB * LANES

    def rblk(b, t, d):
        return rb0 + b * nt + jnp.where(d == 0, t, nt - 1 - t)

    return pl.pallas_call(
        functools.partial(_gla_kernel, n_chunks=tile // GLA_CHUNK),
        grid=(batch, GLA_HEADS // GLA_HB, 2, nt),
        in_specs=[pl.BlockSpec((tile, hw), lambda b, g, d, t: (rblk(b, t, d), COL_QK // hw + g)),
                  pl.BlockSpec((tile, hw), lambda b, g, d, t: (rblk(b, t, d), COL_V // hw + g)),
                  pl.BlockSpec((tile, LANES), lambda b, g, d, t: (rblk(b, t, d), 0)),
                  pl.BlockSpec((1, LANES, hw), lambda b, g, d, t: (d, 0, g)),
                  pl.BlockSpec((1, 1, hw), lambda b, g, d, t: (d, 0, g)),
                  pl.BlockSpec((1, 1, GLA_HB, LANES, LANES), lambda b, g, d, t: (b, d, g, 0, 0))],
        out_specs=[pl.BlockSpec((1, tile, hw), lambda b, g, d, t: (d, rblk(b, t, d) - rb0, g)),
                   pl.BlockSpec((1, 1, GLA_HB, LANES, LANES), lambda b, g, d, t: (b, d, g, 0, 0))],
        out_shape=[jax.ShapeDtypeStruct((2, batch * seq, GLA_HEADS * GLA_DV), F32),
                   jax.ShapeDtypeStruct((batch, 2, GLA_HEADS, LANES, LANES), F32)],
        scratch_shapes=[pltpu.VMEM((GLA_HB, LANES, LANES), F32)],
        compiler_params=_params("parallel", "parallel", "arbitrary", "arbitrary"),
        name="gla",
    )(proj, proj, glr, wg, bg, init_t)


def _prep_kernel(*refs, rope, cache):
    qa_ref, kv_ref, qw_ref, kw_ref = refs[:4]
    refs = refs[4:]
    if rope:
        cos_ref, sa_ref, sb_ref = refs[:3]
        refs = refs[3:]
    qn_ref, kn_ref, vb_ref = refs[:3]
    if cache:
        ck_ref, cv_ref = refs[3:5]

    def rot(y):
        if not rope:
            return y
        return (y * cos_ref[...] + pltpu.roll(y, LANES - 32, 1) * sa_ref[...]
                + pltpu.roll(y, 32, 1) * sb_ref[...])

    for h in range(ATT_HEADS):
        hl = slice(h * HEAD_DIM, (h + 1) * HEAD_DIM)
        qn_ref[:, hl] = rot(_rms(qa_ref[:, hl]) * qw_ref[...]).astype(BF16)
    for h in range(ATT_KV_HEADS):
        hl = slice(h * HEAD_DIM, (h + 1) * HEAD_DIM)
        vl = slice((ATT_KV_HEADS + h) * HEAD_DIM, (ATT_KV_HEADS + h + 1) * HEAD_DIM)
        kn = _rms(kv_ref[:, hl]) * kw_ref[...]
        v = kv_ref[:, vl]
        kn_ref[:, hl] = rot(kn).astype(BF16)
        vb_ref[:, hl] = v.astype(BF16)
        if cache:
            ck_ref[0, h] = kn
            cv_ref[0, h] = v


def _prep(proj, qw, kw, rope_tabs, *, row0, batch, seq, cache):
    tm = 256
    n = batch * seq
    rb0 = row0 // tm
    per_req = seq // tm
    kvw = ATT_KV_HEADS * HEAD_DIM
    rope = rope_tabs is not None
    in_specs = [pl.BlockSpec((tm, ATT_HEADS * HEAD_DIM), lambda i: (rb0 + i, COL_QA // (ATT_HEADS * HEAD_DIM))),
                pl.BlockSpec((tm, 2 * kvw), lambda i: (rb0 + i, COL_KV // (2 * kvw))),
                pl.BlockSpec((1, HEAD_DIM), lambda i: (0, 0)),
                pl.BlockSpec((1, HEAD_DIM), lambda i: (0, 0))]
    args = [proj, proj, qw, kw]
    if rope:
        in_specs += [pl.BlockSpec((tm, HEAD_DIM), lambda i: (i % per_req, 0))] * 3
        args += list(rope_tabs)
    out_specs = [pl.BlockSpec((tm, ATT_HEADS * HEAD_DIM), lambda i: (i, 0)),
                 pl.BlockSpec((tm, kvw), lambda i: (i, 0)),
                 pl.BlockSpec((tm, kvw), lambda i: (i, 0))]
    out_shape = [jax.ShapeDtypeStruct((n, ATT_HEADS * HEAD_DIM), BF16),
                 jax.ShapeDtypeStruct((n, kvw), BF16),
                 jax.ShapeDtypeStruct((n, kvw), BF16)]
    if cache:
        assert seq == tm
        out_specs += [pl.BlockSpec((1, ATT_KV_HEADS, seq, HEAD_DIM), lambda i: (i, 0, 0, 0))] * 2
        out_shape += [jax.ShapeDtypeStruct((batch, ATT_KV_HEADS, seq, HEAD_DIM), F32)] * 2
    return pl.pallas_call(
        functools.partial(_prep_kernel, rope=rope, cache=cache),
        grid=(n // tm,),
        in_specs=in_specs, out_specs=out_specs, out_shape=out_shape,
        compiler_params=_params("parallel"),
        name="attn_prep",
    )(*args)


def _rope_tables(seq):
    half = HEAD_DIM // 2
    tok = jnp.arange(seq)
    row = (tok // GRID_W).astype(F32)
    col = (tok % GRID_W).astype(F32)
    inv_freq = ROPE_THETA ** (-jnp.arange(0, half, 2, dtype=F32) / half)
    ang_r = row[:, None] * inv_freq[None, :]
    ang_c = col[:, None] * inv_freq[None, :]
    ang = jnp.concatenate([ang_r, ang_r, ang_c, ang_c], axis=-1)
    first = (jnp.arange(HEAD_DIM) % half) < half // 2
    sin = jnp.sin(ang)
    return jnp.cos(ang), jnp.where(first, -sin, 0.0), jnp.where(first, 0.0, sin)


def _attn_kernel(*refs, ctx):
    if ctx:
        q_ref, k_ref, v_ref, ck_ref, cv_ref, o_ref = refs
        ck = ck_ref[0, 0, 0].astype(BF16)
        cv = cv_ref[0, 0, 0].astype(BF16)
    else:
        q_ref, k_ref, v_ref, o_ref = refs
    k = k_ref[...]
    v = v_ref[...]
    scale = HEAD_DIM ** -0.5
    for g in range(ATT_GROUP):
        gl = slice(g * HEAD_DIM, (g + 1) * HEAD_DIM)
        q = q_ref[:, gl]
        s = lax.dot_general(q, k, NT_DIMS, preferred_element_type=F32)
        m = jnp.max(s, axis=-1, keepdims=True)
        if ctx:
            s2 = lax.dot_general(q, ck, NT_DIMS, preferred_element_type=F32)
            m = jnp.maximum(m, jnp.max(s2, axis=-1, keepdims=True))
        p = jnp.exp((s - m) * scale)
        l = jnp.sum(p, axis=-1, keepdims=True)
        acc = _dot(p.astype(BF16), v)
        if ctx:
            p2 = jnp.exp((s2 - m) * scale)
            l = l + jnp.sum(p2, axis=-1, keepdims=True)
            acc = acc + _dot(p2.astype(BF16), cv)
        o_ref[:, gl] = (acc / l).astype(BF16)


def _attention(qn, kn, vb, ctx_kv, *, batch, seq, tq):
    n = batch * seq
    nq = seq // tq
    gw = ATT_GROUP * HEAD_DIM
    in_specs = [pl.BlockSpec((tq, gw), lambda b, h, t: (b * nq + t, h)),
                pl.BlockSpec((seq, HEAD_DIM), lambda b, h, t: (b, h)),
                pl.BlockSpec((seq, HEAD_DIM), lambda b, h, t: (b, h))]
    args = [qn, kn, vb]
    if ctx_kv is not None:
        ck, cv, layer = ctx_kv
        past = ck.shape[3]
        spec = pl.BlockSpec((1, 1, 1, past, HEAD_DIM), lambda b, h, t: (b, layer, h, 0, 0))
        in_specs += [spec, spec]
        args += [ck, cv]
    return pl.pallas_call(
        functools.partial(_attn_kernel, ctx=ctx_kv is not None),
        grid=(batch, ATT_KV_HEADS, nq),
        in_specs=in_specs,
        out_specs=pl.BlockSpec((tq, gw), lambda b, h, t: (b * nq + t, h)),
        out_shape=jax.ShapeDtypeStruct((n, ATT_HEADS * HEAD_DIM), BF16),
        compiler_params=_params("parallel", "parallel", "arbitrary"),
        name="attention",
    )(*args)


def _out_proj_kernel(x_ref, od_ref, og_ref, oa_ref, gn_ref, w_ref, mod_ref, nw_ref, x1_ref, h2_ref):
    o = od_ref[0] + od_ref[1]
    parts = []
    for h in range(GLA_HEADS):
        hl = slice(h * GLA_DV, (h + 1) * GLA_DV)
        parts.append((_rms(o[:, hl]) * gn_ref[...] * jax.nn.silu(og_ref[:, hl])).astype(BF16))
    og = jnp.concatenate(parts, axis=-1)
    gw = GLA_HEADS * GLA_DV
    mix = _dot(og, w_ref[0:gw, :]) + _dot(oa_ref[...], w_ref[gw:, :])
    x1 = x_ref[...] + mod_ref[0, 2:3, :] * mix
    x1_ref[...] = x1
    h2_ref[...] = _rms(x1) * nw_ref[...] * (1.0 + mod_ref[0, 4:5, :]) + mod_ref[0, 3:4, :]


def _out_proj(x, odir, proj, oa, gla_norm, w_out, mod, norm_w, row):
    n = x.shape[0]
    tm = 256
    gw = GLA_HEADS * GLA_DV
    return pl.pallas_call(
        _out_proj_kernel,
        grid=(n // tm,),
        in_specs=[pl.BlockSpec((tm, D_MODEL), lambda i: (i, 0)),
                  pl.BlockSpec((2, tm, gw), lambda i: (0, i, 0)),
                  pl.BlockSpec((tm, gw), lambda i: (i, COL_OG // gw)),
                  pl.BlockSpec((tm, ATT_HEADS * HEAD_DIM), lambda i: (i, 0)),
                  pl.BlockSpec((1, GLA_DV), lambda i: (0, 0)),
                  pl.BlockSpec((gw + ATT_HEADS * HEAD_DIM, D_MODEL), lambda i: (0, 0)),
                  pl.BlockSpec((1, 6, D_MODEL), lambda i: (row(i), 0, 0)),
                  pl.BlockSpec((1, D_MODEL), lambda i: (0, 0))],
        out_specs=[pl.BlockSpec((tm, D_MODEL), lambda i: (i, 0)),
                   pl.BlockSpec((tm, D_MODEL), lambda i: (i, 0))],
        out_shape=[jax.ShapeDtypeStruct((n, D_MODEL), F32),
                   jax.ShapeDtypeStruct((n, D_MODEL), F32)],
        compiler_params=_params("parallel"),
        name="out_proj",
    )(x, odir, proj, oa, gla_norm, w_out, mod, norm_w)


NEG_INF = float("-inf")
N_CAND = PEER_TOPK * PEER_TOPK


_CAND_GROUPS = ((0, 0), (0, 8), (1, 0), (2, 0), (3, 0), (4, 0), (5, 0), (6, 0), (7, 0))
N_CAND_ROWS = 8 * (len(_CAND_GROUPS) + 1)


def _cand_consts():
    pos = np.zeros((N_CAND_ROWS, 1), np.float32)
    off = np.zeros((N_CAND_ROWS, 1), np.float32)
    for g, (a, b0) in enumerate(_CAND_GROUPS):
        for j in range(8):
            pos[8 * g + j] = a * PEER_TOPK + b0 + j
            off[8 * g + j] = 0.0 if (a + 1) * (b0 + j + 1) <= PEER_TOPK else NEG_INF
    for j in range(8):
        pos[N_CAND_ROWS - 8 + j] = (8 + j) * PEER_TOPK
    return (jnp.asarray(np.broadcast_to(pos, (N_CAND_ROWS, LANES))),
            jnp.asarray(np.broadcast_to(off, (N_CAND_ROWS, LANES))))


def _pair_rows(first, second):
    rows = [first[a:a + 1, :] + second[b0:b0 + 8, :] for a, b0 in _CAND_GROUPS]
    rows.append(first[8:16, :] + second[0:1, :])
    return jnp.concatenate(rows, axis=0)


def _top16_rows(s, row_id):
    rank = lax.broadcasted_iota(jnp.int32, (PEER_TOPK, s.shape[1]), 0)
    vals = jnp.zeros((PEER_TOPK, s.shape[1]), F32)
    ids = jnp.zeros((PEER_TOPK, s.shape[1]), F32)
    for r in range(PEER_TOPK):
        m = jnp.max(s, axis=0, keepdims=True)
        i = jnp.min(jnp.where(s == m, row_id, float(N_CAND)), axis=0, keepdims=True)
        s = jnp.where(row_id == i, NEG_INF, s)
        vals = jnp.where(rank == r, m, vals)
        ids = jnp.where(rank == r, i, ids)
    return vals, ids


def _peer_topk_kernel(h_ref, wq_ref, keys_ref, pos_ref, off_ref, idx_ref, gate_ref, q_scr):
    tm = h_ref.shape[0]
    q = _dot(h_ref[...].astype(BF16), wq_ref[...]).astype(BF16)
    for a in range(2 * PEER_HEADS):
        q_scr[a] = q[:, a * LANES:(a + 1) * LANES]
    key_id = lax.broadcasted_iota(jnp.int32, (N_KEYS, LANES), 0).astype(F32)
    rank = lax.broadcasted_iota(jnp.int32, (PEER_TOPK, LANES), 0)
    pos = pos_ref[...]
    off = off_ref[...]

    def head(h, carry):
        out_rows = pl.ds(pl.multiple_of(h * PEER_TOPK, PEER_TOPK), PEER_TOPK)
        for c in range(tm // LANES):
            cols = slice(c * LANES, (c + 1) * LANES)
            s1 = lax.dot_general(keys_ref[h, 0], q_scr[2 * h, cols, :], NT_DIMS, preferred_element_type=F32)
            s2 = lax.dot_general(keys_ref[h, 1], q_scr[2 * h + 1, cols, :], NT_DIMS, preferred_element_type=F32)
            v1, i1 = _top16_rows(s1, key_id)
            v2, i2 = _top16_rows(s2, key_id)
            cand = _pair_rows(v1, v2) + off
            cidx = _pair_rows(i1 * float(N_KEYS), i2)
            best = jnp.zeros((PEER_TOPK, LANES), F32)
            eidx = jnp.zeros((PEER_TOPK, LANES), F32)
            for r in range(PEER_TOPK):
                m = jnp.max(cand, axis=0, keepdims=True)
                p = jnp.min(jnp.where(cand == m, pos, float(N_CAND)), axis=0, keepdims=True)
                sel = pos == p
                e = jnp.max(jnp.where(sel, cidx, -1.0), axis=0, keepdims=True)
                cand = jnp.where(sel, NEG_INF, cand)
                best = jnp.where(rank == r, m, best)
                eidx = jnp.where(rank == r, e, eidx)
            ex = jnp.exp(best - best[0:1, :])
            gate_ref[out_rows, cols] = ex / jnp.sum(ex, axis=0, keepdims=True)
            idx_ref[out_rows, cols] = eidx.astype(jnp.int32)
        return carry

    lax.fori_loop(0, PEER_HEADS, head, 0)


def _peer_topk(h2, wq, keys):
    n = h2.shape[0]
    tm = 256
    qw = PEER_HEADS * PEER_QDIM
    full = lambda shape: pl.BlockSpec(shape, lambda i: (0,) * len(shape))
    return pl.pallas_call(
        _peer_topk_kernel,
        grid=(n // tm,),
        in_specs=[pl.BlockSpec((tm, D_MODEL), lambda i: (i, 0)),
                  full((D_MODEL, qw)),
                  full((PEER_HEADS, 2, N_KEYS, PEER_QDIM // 2)),
                  full((N_CAND_ROWS, LANES)), full((N_CAND_ROWS, LANES))],
        out_specs=[pl.BlockSpec((PEER_SEL, tm), lambda i: (0, i)),
                   pl.BlockSpec((PEER_SEL, tm), lambda i: (0, i))],
        out_shape=[jax.ShapeDtypeStruct((PEER_SEL, n), jnp.int32),
                   jax.ShapeDtypeStruct((PEER_SEL, n), F32)],
        scratch_shapes=[pltpu.VMEM((2 * PEER_HEADS, tm, LANES), BF16)],
        compiler_params=_params("parallel"),
        name="peer_topk",
    )(h2, wq, keys, *_cand_consts())


U32 = jnp.uint32


def _pack_uv_kernel(u_ref, v_ref, o_ref):
    ub = lax.bitcast_convert_type(u_ref[...].astype(BF16).astype(F32), U32)
    vb = lax.bitcast_convert_type(v_ref[...].astype(BF16).astype(F32), U32)
    o_ref[...] = ub | (vb >> 16)


def _pack_uv(u, v):
    n, tr = u.shape[0], 512
    spec = pl.BlockSpec((tr, D_MODEL), lambda i: (i, 0))
    packed = pl.pallas_call(
        _pack_uv_kernel, grid=(n // tr,), in_specs=[spec, spec], out_specs=spec,
        out_shape=jax.ShapeDtypeStruct((n, D_MODEL), U32),
        compiler_params=_params("parallel"), name="pack_uv",
    )(u, v)
    return packed.reshape(n, 1, D_MODEL)


PEER_TB = 128
PEER_NBUF = 8


def _peer_mix_kernel(idx_ref, gate_ref, h_ref, x_ref, mod_ref, uv_ref, o_ref, *scratch):
    bufs, sem = scratch[:PEER_NBUF], scratch[PEER_NBUF]
    tb = h_ref.shape[0]
    n_groups = tb // PEER_NBUF

    def issue(t, slot):
        for k in range(PEER_SEL):
            pltpu.make_async_copy(uv_ref.at[idx_ref[k, t]], bufs[slot].at[pl.ds(k, 1), :], sem.at[slot]).start()

    def wait(slot):
        pltpu.make_async_copy(bufs[slot], bufs[slot], sem.at[slot]).wait()

    gate_t = gate_ref[...]
    tok = lax.broadcasted_iota(jnp.int32, (1, tb), 1)
    g2 = mod_ref[0, 5:6, :]

    def token(t, slot, prefetch):
        wait(slot)
        if prefetch:
            issue(t + PEER_NBUF - 1, (slot + PEER_NBUF - 1) % PEER_NBUF)
        x = h_ref[pl.ds(t, 1), :]
        w = bufs[slot][...]
        u = lax.bitcast_convert_type(w & jnp.uint32(0xFFFF0000), F32)
        s = jnp.sum(u * x, axis=-1, keepdims=True)
        g = jnp.sum(jnp.where(tok == t, gate_t, 0.0), axis=-1, keepdims=True)
        act = jax.nn.gelu(s) * g
        v = lax.bitcast_convert_type(w << 16, F32)
        o = jnp.sum(act * v, axis=0, keepdims=True)
        o_ref[pl.ds(t, 1), :] = x_ref[pl.ds(t, 1), :] + g2 * o

    for t0 in range(PEER_NBUF - 1):
        issue(t0, t0)

    def group(g, carry):
        for j in range(PEER_NBUF):
            token(g * PEER_NBUF + j, j, True)
        return carry

    lax.fori_loop(0, n_groups - 1, group, 0)
    for j in range(PEER_NBUF):
        token((n_groups - 1) * PEER_NBUF + j, j, j == 0)


def _peer_mix(idx_t, gates_t, h2, x1, mod, uv, row):
    n = h2.shape[0]
    tb = PEER_TB
    return pl.pallas_call(
        _peer_mix_kernel,
        grid=(n // tb,),
        in_specs=[pl.BlockSpec((PEER_SEL, tb), lambda i: (0, i), memory_space=pltpu.SMEM),
                  pl.BlockSpec((PEER_SEL, tb), lambda i: (0, i)),
                  pl.BlockSpec((tb, D_MODEL), lambda i: (i, 0)),
                  pl.BlockSpec((tb, D_MODEL), lambda i: (i, 0)),
                  pl.BlockSpec((1, 6, D_MODEL), lambda i: (row(i), 0, 0)),
                  pl.BlockSpec(memory_space=pl.ANY)],
        out_specs=pl.BlockSpec((tb, D_MODEL), lambda i: (i, 0)),
        out_shape=jax.ShapeDtypeStruct((n, D_MODEL), F32),
        scratch_shapes=[pltpu.VMEM((PEER_SEL, D_MODEL), U32)] * PEER_NBUF
                       + [pltpu.SemaphoreType.DMA((PEER_NBUF,))],
        compiler_params=_params("arbitrary"),
        name="peer_mix",
    )(idx_t, gates_t, h2, x1, mod, uv)


def _layer_weights(w_in, w_gate, b_gate, w_out, peer_wq, peer_keys, peer_u, peer_v):
    o = np.cumsum((0, GLA_HEADS * GLA_DK, GLA_HEADS * GLA_DK, GLA_HEADS * GLA_DV, 2 * GLA_RANK,
                   GLA_HEADS * GLA_DV, ATT_HEADS * HEAD_DIM, ATT_KV_HEADS * HEAD_DIM, ATT_KV_HEADS * HEAD_DIM))
    wq_g = w_in[:, o[0]:o[1]].reshape(D_MODEL, GLA_HEADS, GLA_DK) * (GLA_DK ** -0.5)
    wk_g = w_in[:, o[1]:o[2]].reshape(D_MODEL, GLA_HEADS, GLA_DK)
    w_qk = jnp.concatenate([wq_g, wk_g], axis=-1).reshape(D_MODEL, GLA_HEADS * LANES)
    w_main = jnp.concatenate([w_qk, w_in[:, o[2]:o[3]], w_in[:, o[4]:o[5]], w_in[:, o[5]:o[6]],
                              w_in[:, o[6]:o[7]], w_in[:, o[7]:o[8]]], axis=-1).astype(BF16)
    w_glr = jnp.pad(w_in[:, o[3]:o[4]], ((0, 0), (0, LANES - 2 * GLA_RANK))).astype(BF16)
    wg = w_gate.reshape(2, GLA_RANK, GLA_HEADS, GLA_DK)
    wg = jnp.concatenate([wg, wg], axis=-1).reshape(2, GLA_RANK, GLA_HEADS * LANES)
    wg = jnp.stack([jnp.pad(wg[0], ((0, LANES - GLA_RANK), (0, 0))),
                    jnp.pad(wg[1], ((GLA_RANK, LANES - 2 * GLA_RANK), (0, 0)))]).astype(BF16)
    bg = b_gate.reshape(2, GLA_HEADS, GLA_DK)
    bg = jnp.concatenate([bg, bg], axis=-1).reshape(2, 1, GLA_HEADS * LANES)
    uv = _pack_uv(peer_u, peer_v)
    return dict(w_main=w_main, w_glr=w_glr, wg=wg, bg=bg, w_out=w_out.astype(BF16),
                wq=peer_wq.astype(BF16), keys=peer_keys.astype(BF16), uv=uv)


def _state_to_kernel(s):
    st = jnp.swapaxes(s, -1, -2)
    return jnp.pad(st, [(0, 0)] * (st.ndim - 1) + [(0, LANES - GLA_DK)])


def _state_from_kernel(st):
    return jnp.swapaxes(st[..., :GLA_DK], -1, -2)


def kernel(x_prompt, x_sample, cache_k, cache_v, state_gla, c, c_ctx, w_mod, b_mod, norm_mix, norm_ffn,
           w_in, w_gate, b_gate, gla_norm, q_norm, k_norm, w_out, peer_wq, peer_keys, peer_u, peer_v):
    bc, tc, _ = x_prompt.shape
    bl, tl, _ = x_sample.shape
    n_ctx, n_lat = bc * tc, bl * tl
    assert bl + 1 <= MOD_ROWS

    cvec = jnp.concatenate([c_ctx[None, :], c, jnp.zeros((MOD_ROWS - 1 - bl, D_MODEL), F32)], axis=0)
    mod_all = _modulation(cvec, w_mod, b_mod)
    rope_tabs = _rope_tables(tl)
    x = jnp.concatenate([x_prompt.reshape(n_ctx, D_MODEL), x_sample.reshape(n_lat, D_MODEL)], axis=0)
    zero_state = jnp.zeros((bc, 2, GLA_HEADS, LANES, LANES), F32)

    ks, vs, ss = [], [], []
    for l in range(DEPTH):
        w = _layer_weights(w_in[l], w_gate[l], b_gate[l], w_out[l], peer_wq[l], peer_keys[l],
                           peer_u[l], peer_v[l])
        mod = mod_all[l]
        proj, glr = _proj_in(x, mod, norm_mix[l][None, :], w["w_main"], w["w_glr"], _mod_row(512, n_ctx, tl))

        od_c, st_c = _gla(proj, glr, w["wg"], w["bg"], zero_state, row0=0, batch=bc, seq=tc, tile=tc)
        od_l, _ = _gla(proj, glr, w["wg"], w["bg"], _state_to_kernel(state_gla[:, l]),
                       row0=n_ctx, batch=bl, seq=tl, tile=512)
        odir = jnp.concatenate([od_c, od_l], axis=1)
        ss.append(_state_from_kernel(st_c))

        qw, kw = q_norm[l][None, :], k_norm[l][None, :]
        qn_c, kn_c, vb_c, ck, cv = _prep(proj, qw, kw, None, row0=0, batch=bc, seq=tc, cache=True)
        qn_l, kn_l, vb_l = _prep(proj, qw, kw, rope_tabs, row0=n_ctx, batch=bl, seq=tl, cache=False)
        ks.append(ck)
        vs.append(cv)
        oa_c = _attention(qn_c, kn_c, vb_c, None, batch=bc, seq=tc, tq=tc)
        oa_l = _attention(qn_l, kn_l, vb_l, (cache_k, cache_v, l), batch=bl, seq=tl, tq=128)
        oa = jnp.concatenate([oa_c, oa_l], axis=0)

        x1, h2 = _out_proj(x, odir, proj, oa, gla_norm[l][None, :], w["w_out"], mod, norm_ffn[l][None, :],
                           _mod_row(256, n_ctx, tl))
        idx, gates = _peer_topk(h2, w["wq"], w["keys"])
        x = _peer_mix(idx, gates, h2, x1, mod, w["uv"], _mod_row(PEER_TB, n_ctx, tl))

    y_prompt = x[:n_ctx].reshape(bc, tc, D_MODEL)
    y_sample = x[n_ctx:].reshape(bl, tl, D_MODEL)
    return (y_prompt, y_sample, jnp.stack(ks, axis=1), jnp.stack(vs, axis=1), jnp.stack(ss, axis=1))
```

```python
import functools

import jax
import jax.numpy as jnp
import numpy as np
from jax import lax
from jax.experimental import pallas as pl
from jax.experimental.pallas import tpu as pltpu

F32 = jnp.float32
BF16 = jnp.bfloat16

D_MODEL = 2048
DEPTH = 2
GRID_W = 64
GLA_HEADS = 8
GLA_DK = 64
GLA_DV = 128
GLA_RANK = 16
GLA_TAU = 16.0
GLA_CHUNK = 64
ATT_HEADS = 8
ATT_KV_HEADS = 2
ATT_GROUP = ATT_HEADS // ATT_KV_HEADS
HEAD_DIM = 128
ROPE_THETA = 10000.0
PEER_HEADS = 8
PEER_QDIM = 256
N_KEYS = 128
PEER_TOPK = 16
PEER_SEL = PEER_HEADS * PEER_TOPK
EPS = 1e-6

LANES = 128
MOD_ROWS = 8
VMEM_LIMIT = 56 * 1024 * 1024

COL_QK = 0
COL_V = COL_QK + GLA_HEADS * LANES
COL_OG = COL_V + GLA_HEADS * GLA_DV
COL_QA = COL_OG + GLA_HEADS * GLA_DV
COL_KV = COL_QA + ATT_HEADS * HEAD_DIM
PROJ_COLS = COL_KV + 2 * ATT_KV_HEADS * HEAD_DIM

NT_DIMS = (((1,), (1,)), ((), ()))
TN_DIMS = (((0,), (0,)), ((), ()))


def _params(*sem):
    return pltpu.CompilerParams(dimension_semantics=sem, vmem_limit_bytes=VMEM_LIMIT)


def _dot(a, b):
    return jnp.dot(a, b, preferred_element_type=F32)


def _dot_exact01(a01, x):
    hi = x.astype(BF16)
    r1 = x - hi.astype(F32)
    mid = r1.astype(BF16)
    lo = (r1 - mid.astype(F32)).astype(BF16)
    return _dot(a01, hi) + _dot(a01, mid) + _dot(a01, lo)


def _dot_exact01_rhs(x, b01):
    hi = x.astype(BF16)
    r1 = x - hi.astype(F32)
    mid = r1.astype(BF16)
    lo = (r1 - mid.astype(F32)).astype(BF16)
    return _dot(hi, b01) + _dot(mid, b01) + _dot(lo, b01)


def _rms(x):
    return x * lax.rsqrt(jnp.mean(x * x, axis=-1, keepdims=True) + EPS)


def _mod_kernel(cv_ref, w_ref, b_ref, o_ref):
    a = jax.nn.silu(cv_ref[...]).astype(BF16)
    o_ref[0] = _dot(a, w_ref[0].astype(BF16)) + b_ref[0]


def _modulation(cvec, w_mod, b_mod):
    tn = 1536
    n_out = 6 * D_MODEL
    out = pl.pallas_call(
        _mod_kernel,
        grid=(DEPTH, n_out // tn),
        in_specs=[pl.BlockSpec((MOD_ROWS, D_MODEL), lambda l, j: (0, 0)),
                  pl.BlockSpec((1, D_MODEL, tn), lambda l, j: (l, 0, j)),
                  pl.BlockSpec((1, 1, tn), lambda l, j: (l, 0, j))],
        out_specs=pl.BlockSpec((1, MOD_ROWS, tn), lambda l, j: (l, 0, j)),
        out_shape=jax.ShapeDtypeStruct((DEPTH, MOD_ROWS, n_out), F32),
        compiler_params=_params("parallel", "parallel"),
        name="modulation",
    )(cvec, w_mod, b_mod.reshape(DEPTH, 1, n_out))
    return out.reshape(DEPTH, MOD_ROWS, 6, D_MODEL)


def _mod_row(tile_rows, n_ctx, t_lat):
    def row(i):
        start = i * tile_rows
        return jnp.where(start < n_ctx, 0, 1 + (start - n_ctx) // t_lat)
    return row


def _proj_in_kernel(x_ref, mod_ref, nw_ref, w_ref, wg_ref, o_ref, glr_ref, h_scr):
    @pl.when(pl.program_id(1) == 0)
    def _():
        y = _rms(x_ref[...]) * nw_ref[...]
        h = (y * (1.0 + mod_ref[0, 1:2, :]) + mod_ref[0, 0:1, :]).astype(BF16)
        h_scr[...] = h
        glr_ref[...] = _dot(h, wg_ref[...])

    o_ref[...] = _dot(h_scr[...], w_ref[...])


def _proj_in(x, mod, norm_w, w_main, w_glr, row):
    n = x.shape[0]
    tm, tn = 512, 1536
    return pl.pallas_call(
        _proj_in_kernel,
        grid=(n // tm, PROJ_COLS // tn),
        in_specs=[pl.BlockSpec((tm, D_MODEL), lambda i, j: (i, 0)),
                  pl.BlockSpec((1, 6, D_MODEL), lambda i, j: (row(i), 0, 0)),
                  pl.BlockSpec((1, D_MODEL), lambda i, j: (0, 0)),
                  pl.BlockSpec((D_MODEL, tn), lambda i, j: (0, j)),
                  pl.BlockSpec((D_MODEL, LANES), lambda i, j: (0, 0))],
        out_specs=[pl.BlockSpec((tm, tn), lambda i, j: (i, j)),
                   pl.BlockSpec((tm, LANES), lambda i, j: (i, 0))],
        out_shape=[jax.ShapeDtypeStruct((n, PROJ_COLS), F32),
                   jax.ShapeDtypeStruct((n, LANES), F32)],
        scratch_shapes=[pltpu.VMEM((tm, D_MODEL), BF16)],
        compiler_params=_params("parallel", "arbitrary"),
        name="proj_in",
    )(x, mod, norm_w, w_main, w_glr)


GLA_HB = 4


def _gla_kernel(qk_ref, v_ref, glr_ref, wg_ref, bg_ref, swap_ref, mask_ref, init_ref, o_ref, st_ref,
                st_scr, x1_scr, k1_scr, x2_scr, k2_scr, vb_scr, dec_scr, *, n_chunks):
    d = pl.program_id(2)
    t = pl.program_id(3)
    fwd = d == 0

    @pl.when(t == 0)
    def _():
        st_scr[...] = init_ref[0, 0]

    c = GLA_CHUNK
    hw = GLA_HB * LANES
    mask = mask_ref[0]
    qmask = lax.broadcasted_iota(jnp.int32, (1, hw), 1) % LANES < GLA_DK

    z = _dot(glr_ref[...].astype(BF16), wg_ref[0]) + bg_ref[0]
    la = jax.nn.log_sigmoid(z) / GLA_TAU
    b_all = _dot_exact01(mask.astype(BF16), la)
    vb_scr[...] = v_ref[...].astype(BF16)
    for k in range(n_chunks):
        rows = slice(k * c, (k + 1) * c)
        b = b_all[rows, :]
        b_mid = jnp.where(fwd, b[c // 2 - 1:c // 2, :], b[c // 2:c // 2 + 1, :])
        b_end = jnp.where(fwd, b[c - 1:c, :], b[0:1, :])
        qk = qk_ref[rows, :]
        x1_scr[rows, :] = (qk * jnp.exp(jnp.where(qmask, b - b_mid, b_mid - b))).astype(BF16)
        x2_scr[rows, :] = (qk * jnp.exp(jnp.where(qmask, b, b_end - b))).astype(BF16)
        dec_scr[k] = jnp.exp(b_end)
    k1_scr[...] = _dot(x1_scr[...], swap_ref[...]).astype(BF16)
    k2_scr[...] = _dot(x2_scr[...], swap_ref[...]).astype(BF16)

    for h in range(GLA_HB):
        hl = slice(h * LANES, (h + 1) * LANES)
        s = lax.dot_general(x1_scr[:, hl], k1_scr[:, hl], NT_DIMS, preferred_element_type=F32)
        s = jnp.where(mask > 0.0, s, 0.0).astype(BF16)
        o_ref[0, :, hl] = _dot(s, vb_scr[:, hl])

    def chunk(step, carry):
        k = jnp.where(fwd, step, n_chunks - 1 - step)
        rows = pl.ds(pl.multiple_of(k * c, c), c)
        dec = dec_scr[k]
        for h in range(GLA_HB):
            hl = slice(h * LANES, (h + 1) * LANES)
            st = st_scr[h]
            o_ref[0, rows, hl] += lax.dot_general(x2_scr[rows, hl], st.astype(BF16), NT_DIMS,
                                                  preferred_element_type=F32)
            upd = lax.dot_general(vb_scr[rows, hl], k2_scr[rows, hl], TN_DIMS, preferred_element_type=F32)
            st_scr[h] = st * dec[:, hl] + upd
        return carry

    lax.fori_loop(0, n_chunks, chunk, 0)

    @pl.when(t == pl.num_programs(3) - 1)
    def _():
        st_ref[0, 0] = st_scr[...]


def _gla(proj, glr, wg, bg, init_t, *, row0, batch, seq, tile):
    nt = seq // tile
    rb0 = row0 // tile
    hw = GLA_HB * LANES
    n_chunks = tile // GLA_CHUNK
    l = np.arange(hw)
    swap = ((l[:, None] == l[None, :] + GLA_DK) & (l[None, :] % LANES < GLA_DK)).astype(np.float32)
    r = np.arange(tile)
    same = r[:, None] // GLA_CHUNK == r[None, :] // GLA_CHUNK
    mask = np.stack([same & (r[:, None] >= r[None, :]), same & (r[:, None] <= r[None, :])]).astype(np.float32)

    def rblk(b, t, d):
        return rb0 + b * nt + jnp.where(d == 0, t, nt - 1 - t)

    return pl.pallas_call(
        functools.partial(_gla_kernel, n_chunks=n_chunks),
        grid=(batch, GLA_HEADS // GLA_HB, 2, nt),
        in_specs=[pl.BlockSpec((tile, hw), lambda b, g, d, t: (rblk(b, t, d), COL_QK // hw + g)),
                  pl.BlockSpec((tile, hw), lambda b, g, d, t: (rblk(b, t, d), COL_V // hw + g)),
                  pl.BlockSpec((tile, LANES), lambda b, g, d, t: (rblk(b, t, d), 0)),
                  pl.BlockSpec((1, LANES, hw), lambda b, g, d, t: (d, 0, g)),
                  pl.BlockSpec((1, 1, hw), lambda b, g, d, t: (d, 0, g)),
                  pl.BlockSpec((hw, hw), lambda b, g, d, t: (0, 0)),
                  pl.BlockSpec((1, tile, tile), lambda b, g, d, t: (d, 0, 0)),
                  pl.BlockSpec((1, 1, GLA_HB, LANES, LANES), lambda b, g, d, t: (b, d, g, 0, 0))],
        out_specs=[pl.BlockSpec((1, tile, hw), lambda b, g, d, t: (d, rblk(b, t, d) - rb0, g)),
                   pl.BlockSpec((1, 1, GLA_HB, LANES, LANES), lambda b, g, d, t: (b, d, g, 0, 0))],
        out_shape=[jax.ShapeDtypeStruct((2, batch * seq, GLA_HEADS * GLA_DV), F32),
                   jax.ShapeDtypeStruct((batch, 2, GLA_HEADS, LANES, LANES), F32)],
        scratch_shapes=[pltpu.VMEM((GLA_HB, LANES, LANES), F32)]
                       + [pltpu.VMEM((tile, hw), BF16)] * 5
                       + [pltpu.VMEM((n_chunks, 1, hw), F32)],
        compiler_params=_params("parallel", "parallel", "arbitrary", "arbitrary"),
        name="gla",
    )(proj, proj, glr, wg, bg, jnp.asarray(swap, BF16), jnp.asarray(mask), init_t)


def _prep_kernel(*refs, rope, cache):
    qa_ref, kv_ref, qw_ref, kw_ref = refs[:4]
    refs = refs[4:]
    if rope:
        cos_ref, sa_ref, sb_ref = refs[:3]
        refs = refs[3:]
    qn_ref, kn_ref, vb_ref = refs[:3]
    if cache:
        ck_ref, cv_ref = refs[3:5]

    def rot(y):
        if not rope:
            return y
        return (y * cos_ref[...] + pltpu.roll(y, LANES - 32, 1) * sa_ref[...]
                + pltpu.roll(y, 32, 1) * sb_ref[...])

    for h in range(ATT_HEADS):
        hl = slice(h * HEAD_DIM, (h + 1) * HEAD_DIM)
        qn_ref[:, hl] = rot(_rms(qa_ref[:, hl]) * qw_ref[...]).astype(BF16)
    for h in range(ATT_KV_HEADS):
        hl = slice(h * HEAD_DIM, (h + 1) * HEAD_DIM)
        vl = slice((ATT_KV_HEADS + h) * HEAD_DIM, (ATT_KV_HEADS + h + 1) * HEAD_DIM)
        kn = _rms(kv_ref[:, hl]) * kw_ref[...]
        v = kv_ref[:, vl]
        kn_ref[:, hl] = rot(kn).astype(BF16)
        vb_ref[:, hl] = v.astype(BF16)
        if cache:
            ck_ref[0, h] = kn
            cv_ref[0, h] = v


def _prep(proj, qw, kw, rope_tabs, *, row0, batch, seq, cache):
    tm = 256
    n = batch * seq
    rb0 = row0 // tm
    per_req = seq // tm
    kvw = ATT_KV_HEADS * HEAD_DIM
    rope = rope_tabs is not None
    in_specs = [pl.BlockSpec((tm, ATT_HEADS * HEAD_DIM), lambda i: (rb0 + i, COL_QA // (ATT_HEADS * HEAD_DIM))),
                pl.BlockSpec((tm, 2 * kvw), lambda i: (rb0 + i, COL_KV // (2 * kvw))),
                pl.BlockSpec((1, HEAD_DIM), lambda i: (0, 0)),
                pl.BlockSpec((1, HEAD_DIM), lambda i: (0, 0))]
    args = [proj, proj, qw, kw]
    if rope:
        in_specs += [pl.BlockSpec((tm, HEAD_DIM), lambda i: (i % per_req, 0))] * 3
        args += list(rope_tabs)
    out_specs = [pl.BlockSpec((tm, ATT_HEADS * HEAD_DIM), lambda i: (i, 0)),
                 pl.BlockSpec((tm, kvw), lambda i: (i, 0)),
                 pl.BlockSpec((tm, kvw), lambda i: (i, 0))]
    out_shape = [jax.ShapeDtypeStruct((n, ATT_HEADS * HEAD_DIM), BF16),
                 jax.ShapeDtypeStruct((n, kvw), BF16),
                 jax.ShapeDtypeStruct((n, kvw), BF16)]
    if cache:
        assert seq == tm
        out_specs += [pl.BlockSpec((1, ATT_KV_HEADS, seq, HEAD_DIM), lambda i: (i, 0, 0, 0))] * 2
        out_shape += [jax.ShapeDtypeStruct((batch, ATT_KV_HEADS, seq, HEAD_DIM), F32)] * 2
    return pl.pallas_call(
        functools.partial(_prep_kernel, rope=rope, cache=cache),
        grid=(n // tm,),
        in_specs=in_specs, out_specs=out_specs, out_shape=out_shape,
        compiler_params=_params("parallel"),
        name="attn_prep",
    )(*args)


def _rope_tables(seq):
    half = HEAD_DIM // 2
    tok = jnp.arange(seq)
    row = (tok // GRID_W).astype(F32)
    col = (tok % GRID_W).astype(F32)
    inv_freq = ROPE_THETA ** (-jnp.arange(0, half, 2, dtype=F32) / half)
    ang_r = row[:, None] * inv_freq[None, :]
    ang_c = col[:, None] * inv_freq[None, :]
    ang = jnp.concatenate([ang_r, ang_r, ang_c, ang_c], axis=-1)
    first = (jnp.arange(HEAD_DIM) % half) < half // 2
    sin = jnp.sin(ang)
    return jnp.cos(ang), jnp.where(first, -sin, 0.0), jnp.where(first, 0.0, sin)


def _attn_kernel(*refs, ctx):
    if ctx:
        q_ref, k_ref, v_ref, ck_ref, cv_ref, o_ref = refs
        ck = ck_ref[0, 0, 0].astype(BF16)
        cv = cv_ref[0, 0, 0].astype(BF16)
    else:
        q_ref, k_ref, v_ref, o_ref = refs
    k = k_ref[...]
    v = v_ref[...]
    scale = HEAD_DIM ** -0.5
    for g in range(ATT_GROUP):
        gl = slice(g * HEAD_DIM, (g + 1) * HEAD_DIM)
        q = q_ref[:, gl]
        s = lax.dot_general(q, k, NT_DIMS, preferred_element_type=F32)
        m = jnp.max(s, axis=-1, keepdims=True)
        if ctx:
            s2 = lax.dot_general(q, ck, NT_DIMS, preferred_element_type=F32)
            m = jnp.maximum(m, jnp.max(s2, axis=-1, keepdims=True))
        p = jnp.exp((s - m) * scale)
        l = jnp.sum(p, axis=-1, keepdims=True)
        acc = _dot(p.astype(BF16), v)
        if ctx:
            p2 = jnp.exp((s2 - m) * scale)
            l = l + jnp.sum(p2, axis=-1, keepdims=True)
            acc = acc + _dot(p2.astype(BF16), cv)
        o_ref[:, gl] = (acc / l).astype(BF16)


def _attention(qn, kn, vb, ctx_kv, *, batch, seq, tq):
    n = batch * seq
    nq = seq // tq
    gw = ATT_GROUP * HEAD_DIM
    in_specs = [pl.BlockSpec((tq, gw), lambda b, h, t: (b * nq + t, h)),
                pl.BlockSpec((seq, HEAD_DIM), lambda b, h, t: (b, h)),
                pl.BlockSpec((seq, HEAD_DIM), lambda b, h, t: (b, h))]
    args = [qn, kn, vb]
    if ctx_kv is not None:
        ck, cv, layer = ctx_kv
        past = ck.shape[3]
        spec = pl.BlockSpec((1, 1, 1, past, HEAD_DIM), lambda b, h, t: (b, layer, h, 0, 0))
        in_specs += [spec, spec]
        args += [ck, cv]
    return pl.pallas_call(
        functools.partial(_attn_kernel, ctx=ctx_kv is not None),
        grid=(batch, ATT_KV_HEADS, nq),
        in_specs=in_specs,
        out_specs=pl.BlockSpec((tq, gw), lambda b, h, t: (b * nq + t, h)),
        out_shape=jax.ShapeDtypeStruct((n, ATT_HEADS * HEAD_DIM), BF16),
        compiler_params=_params("parallel", "parallel", "arbitrary"),
        name="attention",
    )(*args)


def _out_proj_kernel(x_ref, od_ref, og_ref, oa_ref, gn_ref, w_ref, mod_ref, nw_ref, x1_ref, h2_ref):
    o = od_ref[0] + od_ref[1]
    parts = []
    for h in range(GLA_HEADS):
        hl = slice(h * GLA_DV, (h + 1) * GLA_DV)
        parts.append((_rms(o[:, hl]) * gn_ref[...] * jax.nn.silu(og_ref[:, hl])).astype(BF16))
    og = jnp.concatenate(parts, axis=-1)
    gw = GLA_HEADS * GLA_DV
    mix = _dot(og, w_ref[0:gw, :]) + _dot(oa_ref[...], w_ref[gw:, :])
    x1 = x_ref[...] + mod_ref[0, 2:3, :] * mix
    x1_ref[...] = x1
    h2_ref[...] = _rms(x1) * nw_ref[...] * (1.0 + mod_ref[0, 4:5, :]) + mod_ref[0, 3:4, :]


def _out_proj(x, odir, proj, oa, gla_norm, w_out, mod, norm_w, row):
    n = x.shape[0]
    tm = 256
    gw = GLA_HEADS * GLA_DV
    return pl.pallas_call(
        _out_proj_kernel,
        grid=(n // tm,),
        in_specs=[pl.BlockSpec((tm, D_MODEL), lambda i: (i, 0)),
                  pl.BlockSpec((2, tm, gw), lambda i: (0, i, 0)),
                  pl.BlockSpec((tm, gw), lambda i: (i, COL_OG // gw)),
                  pl.BlockSpec((tm, ATT_HEADS * HEAD_DIM), lambda i: (i, 0)),
                  pl.BlockSpec((1, GLA_DV), lambda i: (0, 0)),
                  pl.BlockSpec((gw + ATT_HEADS * HEAD_DIM, D_MODEL), lambda i: (0, 0)),
                  pl.BlockSpec((1, 6, D_MODEL), lambda i: (row(i), 0, 0)),
                  pl.BlockSpec((1, D_MODEL), lambda i: (0, 0))],
        out_specs=[pl.BlockSpec((tm, D_MODEL), lambda i: (i, 0)),
                   pl.BlockSpec((tm, D_MODEL), lambda i: (i, 0))],
        out_shape=[jax.ShapeDtypeStruct((n, D_MODEL), F32),
                   jax.ShapeDtypeStruct((n, D_MODEL), F32)],
        compiler_params=_params("parallel"),
        name="out_proj",
    )(x, odir, proj, oa, gla_norm, w_out, mod, norm_w)


NEG_INF = float("-inf")
N_CAND = PEER_TOPK * PEER_TOPK


_CAND_GROUPS = ((0, 0), (0, 8), (1, 0), (2, 0), (3, 0), (4, 0), (5, 0), (6, 0), (7, 0))
N_CAND_ROWS = 8 * (len(_CAND_GROUPS) + 1)


def _cand_consts():
    pos = np.zeros((N_CAND_ROWS, 1), np.float32)
    off = np.zeros((N_CAND_ROWS, 1), np.float32)
    for g, (a, b0) in enumerate(_CAND_GROUPS):
        for j in range(8):
            pos[8 * g + j] = a * PEER_TOPK + b0 + j
            off[8 * g + j] = 0.0 if (a + 1) * (b0 + j + 1) <= PEER_TOPK else NEG_INF
    for j in range(8):
        pos[N_CAND_ROWS - 8 + j] = (8 + j) * PEER_TOPK
    return (jnp.asarray(np.broadcast_to(pos, (N_CAND_ROWS, LANES))),
            jnp.asarray(np.broadcast_to(off, (N_CAND_ROWS, LANES))))


def _pair_rows(first, second):
    rows = [first[a:a + 1, :] + second[b0:b0 + 8, :] for a, b0 in _CAND_GROUPS]
    rows.append(first[8:16, :] + second[0:1, :])
    return jnp.concatenate(rows, axis=0)


def _top16_rows(s, row_id):
    rank = lax.broadcasted_iota(jnp.int32, (PEER_TOPK, s.shape[1]), 0)
    vals = jnp.zeros((PEER_TOPK, s.shape[1]), F32)
    ids = jnp.zeros((PEER_TOPK, s.shape[1]), F32)
    for r in range(PEER_TOPK):
        m = jnp.max(s, axis=0, keepdims=True)
        i = jnp.min(jnp.where(s == m, row_id, float(N_CAND)), axis=0, keepdims=True)
        s = jnp.where(row_id == i, NEG_INF, s)
        vals = jnp.where(rank == r, m, vals)
        ids = jnp.where(rank == r, i, ids)
    return vals, ids


def _peer_topk_kernel(h_ref, wq_ref, keys_ref, pos_ref, off_ref, idx_ref, gate_ref, q_scr):
    tm = h_ref.shape[0]
    q = _dot(h_ref[...].astype(BF16), wq_ref[...]).astype(BF16)
    for a in range(2 * PEER_HEADS):
        q_scr[a] = q[:, a * LANES:(a + 1) * LANES]
    key_id = lax.broadcasted_iota(jnp.int32, (N_KEYS, LANES), 0).astype(F32)
    rank = lax.broadcasted_iota(jnp.int32, (PEER_TOPK, LANES), 0)
    pos = pos_ref[...]
    off = off_ref[...]

    def head(h, carry):
        out_rows = pl.ds(pl.multiple_of(h * PEER_TOPK, PEER_TOPK), PEER_TOPK)
        for c in range(tm // LANES):
            cols = slice(c * LANES, (c + 1) * LANES)
            s1 = lax.dot_general(keys_ref[h, 0], q_scr[2 * h, cols, :], NT_DIMS, preferred_element_type=F32)
            s2 = lax.dot_general(keys_ref[h, 1], q_scr[2 * h + 1, cols, :], NT_DIMS, preferred_element_type=F32)
            v1, i1 = _top16_rows(s1, key_id)
            v2, i2 = _top16_rows(s2, key_id)
            cand = _pair_rows(v1, v2) + off
            cidx = _pair_rows(i1 * float(N_KEYS), i2)
            best = jnp.zeros((PEER_TOPK, LANES), F32)
            eidx = jnp.zeros((PEER_TOPK, LANES), F32)
            for r in range(PEER_TOPK):
                m = jnp.max(cand, axis=0, keepdims=True)
                p = jnp.min(jnp.where(cand == m, pos, float(N_CAND)), axis=0, keepdims=True)
                sel = pos == p
                e = jnp.max(jnp.where(sel, cidx, -1.0), axis=0, keepdims=True)
                cand = jnp.where(sel, NEG_INF, cand)
                best = jnp.where(rank == r, m, best)
                eidx = jnp.where(rank == r, e, eidx)
            ex = jnp.exp(best - best[0:1, :])
            gate_ref[out_rows, cols] = ex / jnp.sum(ex, axis=0, keepdims=True)
            idx_ref[out_rows, cols] = eidx.astype(jnp.int32)
        return carry

    lax.fori_loop(0, PEER_HEADS, head, 0)


def _peer_topk(h2, wq, keys):
    n = h2.shape[0]
    tm = 256
    qw = PEER_HEADS * PEER_QDIM
    full = lambda shape: pl.BlockSpec(shape, lambda i: (0,) * len(shape))
    return pl.pallas_call(
        _peer_topk_kernel,
        grid=(n // tm,),
        in_specs=[pl.BlockSpec((tm, D_MODEL), lambda i: (i, 0)),
                  full((D_MODEL, qw)),
                  full((PEER_HEADS, 2, N_KEYS, PEER_QDIM // 2)),
                  full((N_CAND_ROWS, LANES)), full((N_CAND_ROWS, LANES))],
        out_specs=[pl.BlockSpec((PEER_SEL, tm), lambda i: (0, i)),
                   pl.BlockSpec((PEER_SEL, tm), lambda i: (0, i))],
        out_shape=[jax.ShapeDtypeStruct((PEER_SEL, n), jnp.int32),
                   jax.ShapeDtypeStruct((PEER_SEL, n), F32)],
        scratch_shapes=[pltpu.VMEM((2 * PEER_HEADS, tm, LANES), BF16)],
        compiler_params=_params("parallel"),
        name="peer_topk",
    )(h2, wq, keys, *_cand_consts())


U32 = jnp.uint32


def _pack_uv_kernel(u_ref, v_ref, o_ref):
    ub = lax.bitcast_convert_type(u_ref[...].astype(BF16).astype(F32), U32)
    vb = lax.bitcast_convert_type(v_ref[...].astype(BF16).astype(F32), U32)
    o_ref[...] = ub | (vb >> 16)


def _pack_uv(u, v):
    n, tr = u.shape[0], 512
    spec = pl.BlockSpec((tr, D_MODEL), lambda i: (i, 0))
    packed = pl.pallas_call(
        _pack_uv_kernel, grid=(n // tr,), in_specs=[spec, spec], out_specs=spec,
        out_shape=jax.ShapeDtypeStruct((n, D_MODEL), U32),
        compiler_params=_params("parallel"), name="pack_uv",
    )(u, v)
    return packed.reshape(n, 1, D_MODEL)


PEER_TB = 128
PEER_NBUF = 8


def _peer_mix_kernel(idx_ref, gate_ref, h_ref, x_ref, mod_ref, uv_ref, o_ref, *scratch):
    bufs, sem = scratch[:PEER_NBUF], scratch[PEER_NBUF]
    tb = h_ref.shape[0]
    n_groups = tb // PEER_NBUF

    def issue(t, slot):
        for k in range(PEER_SEL):
            pltpu.make_async_copy(uv_ref.at[idx_ref[k, t]], bufs[slot].at[pl.ds(k, 1), :],
                                  sem.at[slot]).start(priority=k % 2)

    def wait(slot):
        pltpu.make_async_copy(bufs[slot], bufs[slot], sem.at[slot]).wait()

    gate_t = gate_ref[...]
    tok = lax.broadcasted_iota(jnp.int32, (1, tb), 1)
    g2 = mod_ref[0, 5:6, :]

    def token(t, slot, prefetch):
        wait(slot)
        if prefetch:
            issue(t + PEER_NBUF - 1, (slot + PEER_NBUF - 1) % PEER_NBUF)
        x = h_ref[pl.ds(t, 1), :]
        w = bufs[slot][...]
        u = lax.bitcast_convert_type(w & jnp.uint32(0xFFFF0000), F32)
        s = jnp.sum(u * x, axis=-1, keepdims=True)
        g = jnp.sum(jnp.where(tok == t, gate_t, 0.0), axis=-1, keepdims=True)
        act = jax.nn.gelu(s) * g
        v = lax.bitcast_convert_type(w << 16, F32)
        o = jnp.sum(act * v, axis=0, keepdims=True)
        o_ref[pl.ds(t, 1), :] = x_ref[pl.ds(t, 1), :] + g2 * o

    for t0 in range(PEER_NBUF - 1):
        issue(t0, t0)

    def group(g, carry):
        for j in range(PEER_NBUF):
            token(g * PEER_NBUF + j, j, True)
        return carry

    lax.fori_loop(0, n_groups - 1, group, 0)
    for j in range(PEER_NBUF):
        token((n_groups - 1) * PEER_NBUF + j, j, j == 0)


def _peer_mix(idx_t, gates_t, h2, x1, mod, uv, row):
    n = h2.shape[0]
    tb = PEER_TB
    return pl.pallas_call(
        _peer_mix_kernel,
        grid=(n // tb,),
        in_specs=[pl.BlockSpec((PEER_SEL, tb), lambda i: (0, i), memory_space=pltpu.SMEM),
                  pl.BlockSpec((PEER_SEL, tb), lambda i: (0, i)),
                  pl.BlockSpec((tb, D_MODEL), lambda i: (i, 0)),
                  pl.BlockSpec((tb, D_MODEL), lambda i: (i, 0)),
                  pl.BlockSpec((1, 6, D_MODEL), lambda i: (row(i), 0, 0)),
                  pl.BlockSpec(memory_space=pl.ANY)],
        out_specs=pl.BlockSpec((tb, D_MODEL), lambda i: (i, 0)),
        out_shape=jax.ShapeDtypeStruct((n, D_MODEL), F32),
        scratch_shapes=[pltpu.VMEM((PEER_SEL, D_MODEL), U32)] * PEER_NBUF
                       + [pltpu.SemaphoreType.DMA((PEER_NBUF,))],
        compiler_params=_params("arbitrary"),
        name="peer_mix",
    )(idx_t, gates_t, h2, x1, mod, uv)


def _layer_weights(w_in, w_gate, b_gate, w_out, peer_wq, peer_keys, peer_u, peer_v):
    o = np.cumsum((0, GLA_HEADS * GLA_DK, GLA_HEADS * GLA_DK, GLA_HEADS * GLA_DV, 2 * GLA_RANK,
                   GLA_HEADS * GLA_DV, ATT_HEADS * HEAD_DIM, ATT_KV_HEADS * HEAD_DIM, ATT_KV_HEADS * HEAD_DIM))
    wq_g = w_in[:, o[0]:o[1]].reshape(D_MODEL, GLA_HEADS, GLA_DK) * (GLA_DK ** -0.5)
    wk_g = w_in[:, o[1]:o[2]].reshape(D_MODEL, GLA_HEADS, GLA_DK)
    w_qk = jnp.concatenate([wq_g, wk_g], axis=-1).reshape(D_MODEL, GLA_HEADS * LANES)
    w_main = jnp.concatenate([w_qk, w_in[:, o[2]:o[3]], w_in[:, o[4]:o[5]], w_in[:, o[5]:o[6]],
                              w_in[:, o[6]:o[7]], w_in[:, o[7]:o[8]]], axis=-1).astype(BF16)
    w_glr = jnp.pad(w_in[:, o[3]:o[4]], ((0, 0), (0, LANES - 2 * GLA_RANK))).astype(BF16)
    wg = w_gate.reshape(2, GLA_RANK, GLA_HEADS, GLA_DK)
    wg = jnp.concatenate([wg, wg], axis=-1).reshape(2, GLA_RANK, GLA_HEADS * LANES)
    wg = jnp.stack([jnp.pad(wg[0], ((0, LANES - GLA_RANK), (0, 0))),
                    jnp.pad(wg[1], ((GLA_RANK, LANES - 2 * GLA_RANK), (0, 0)))]).astype(BF16)
    bg = b_gate.reshape(2, GLA_HEADS, GLA_DK)
    bg = jnp.concatenate([bg, bg], axis=-1).reshape(2, 1, GLA_HEADS * LANES)
    uv = _pack_uv(peer_u, peer_v)
    return dict(w_main=w_main, w_glr=w_glr, wg=wg, bg=bg, w_out=w_out.astype(BF16),
                wq=peer_wq.astype(BF16), keys=peer_keys.astype(BF16), uv=uv)


def _state_to_kernel(s):
    st = jnp.swapaxes(s, -1, -2)
    return jnp.pad(st, [(0, 0)] * (st.ndim - 1) + [(0, LANES - GLA_DK)])


def _state_from_kernel(st):
    return jnp.swapaxes(st[..., :GLA_DK], -1, -2)


def kernel(x_prompt, x_sample, cache_k, cache_v, state_gla, c, c_ctx, w_mod, b_mod, norm_mix, norm_ffn,
           w_in, w_gate, b_gate, gla_norm, q_norm, k_norm, w_out, peer_wq, peer_keys, peer_u, peer_v):
    bc, tc, _ = x_prompt.shape
    bl, tl, _ = x_sample.shape
    n_ctx, n_lat = bc * tc, bl * tl
    assert bl + 1 <= MOD_ROWS

    cvec = jnp.concatenate([c_ctx[None, :], c, jnp.zeros((MOD_ROWS - 1 - bl, D_MODEL), F32)], axis=0)
    mod_all = _modulation(cvec, w_mod, b_mod)
    rope_tabs = _rope_tables(tl)
    x = jnp.concatenate([x_prompt.reshape(n_ctx, D_MODEL), x_sample.reshape(n_lat, D_MODEL)], axis=0)
    zero_state = jnp.zeros((bc, 2, GLA_HEADS, LANES, LANES), F32)

    ks, vs, ss = [], [], []
    for l in range(DEPTH):
        w = _layer_weights(w_in[l], w_gate[l], b_gate[l], w_out[l], peer_wq[l], peer_keys[l],
                           peer_u[l], peer_v[l])
        mod = mod_all[l]
        proj, glr = _proj_in(x, mod, norm_mix[l][None, :], w["w_main"], w["w_glr"], _mod_row(512, n_ctx, tl))

        od_c, st_c = _gla(proj, glr, w["wg"], w["bg"], zero_state, row0=0, batch=bc, seq=tc, tile=tc)
        od_l, _ = _gla(proj, glr, w["wg"], w["bg"], _state_to_kernel(state_gla[:, l]),
                       row0=n_ctx, batch=bl, seq=tl, tile=512)
        odir = jnp.concatenate([od_c, od_l], axis=1)
        ss.append(_state_from_kernel(st_c))

        qw, kw = q_norm[l][None, :], k_norm[l][None, :]
        qn_c, kn_c, vb_c, ck, cv = _prep(proj, qw, kw, None, row0=0, batch=bc, seq=tc, cache=True)
        qn_l, kn_l, vb_l = _prep(proj, qw, kw, rope_tabs, row0=n_ctx, batch=bl, seq=tl, cache=False)
        ks.append(ck)
        vs.append(cv)
        oa_c = _attention(qn_c, kn_c, vb_c, None, batch=bc, seq=tc, tq=tc)
        oa_l = _attention(qn_l, kn_l, vb_l, (cache_k, cache_v, l), batch=bl, seq=tl, tq=128)
        oa = jnp.concatenate([oa_c, oa_l], axis=0)

        x1, h2 = _out_proj(x, odir, proj, oa, gla_norm[l][None, :], w["w_out"], mod, norm_ffn[l][None, :],
                           _mod_row(256, n_ctx, tl))
        idx, gates = _peer_topk(h2, w["wq"], w["keys"])
        x = _peer_mix(idx, gates, h2, x1, mod, w["uv"], _mod_row(PEER_TB, n_ctx, tl))

    y_prompt = x[:n_ctx].reshape(bc, tc, D_MODEL)
    y_sample = x[n_ctx:].reshape(bl, tl, D_MODEL)
    return (y_prompt, y_sample, jnp.stack(ks, axis=1), jnp.stack(vs, axis=1), jnp.stack(ss, axis=1))
```

```python
import functools

import jax
import jax.numpy as jnp
import numpy as np
from jax import lax
from jax.experimental import pallas as pl
from jax.experimental.pallas import tpu as pltpu

F32 = jnp.float32
BF16 = jnp.bfloat16

D_MODEL = 2048
DEPTH = 2
GRID_W = 64
GLA_HEADS = 8
GLA_DK = 64
GLA_DV = 128
GLA_RANK = 16
GLA_TAU = 16.0
GLA_CHUNK = 64
ATT_HEADS = 8
ATT_KV_HEADS = 2
ATT_GROUP = ATT_HEADS // ATT_KV_HEADS
HEAD_DIM = 128
ROPE_THETA = 10000.0
PEER_HEADS = 8
PEER_QDIM = 256
N_KEYS = 128
PEER_TOPK = 16
PEER_SEL = PEER_HEADS * PEER_TOPK
EPS = 1e-6

LANES = 128
MOD_ROWS = 8
VMEM_LIMIT = 56 * 1024 * 1024

COL_QK = 0
COL_V = COL_QK + GLA_HEADS * LANES
COL_OG = COL_V + GLA_HEADS * GLA_DV
COL_QA = COL_OG + GLA_HEADS * GLA_DV
COL_KV = COL_QA + ATT_HEADS * HEAD_DIM
PROJ_COLS = COL_KV + 2 * ATT_KV_HEADS * HEAD_DIM

NT_DIMS = (((1,), (1,)), ((), ()))
TN_DIMS = (((0,), (0,)), ((), ()))


def _params(*sem):
    return pltpu.CompilerParams(dimension_semantics=sem, vmem_limit_bytes=VMEM_LIMIT)


def _dot(a, b):
    return jnp.dot(a, b, preferred_element_type=F32)


def _dot_exact01(a01, x):
    hi = x.astype(BF16)
    r1 = x - hi.astype(F32)
    mid = r1.astype(BF16)
    lo = (r1 - mid.astype(F32)).astype(BF16)
    return _dot(a01, hi) + _dot(a01, mid) + _dot(a01, lo)


def _dot_exact01_rhs(x, b01):
    hi = x.astype(BF16)
    r1 = x - hi.astype(F32)
    mid = r1.astype(BF16)
    lo = (r1 - mid.astype(F32)).astype(BF16)
    return _dot(hi, b01) + _dot(mid, b01) + _dot(lo, b01)


def _rms(x):
    return x * lax.rsqrt(jnp.mean(x * x, axis=-1, keepdims=True) + EPS)


def _mod_kernel(cv_ref, w_ref, b_ref, o_ref):
    a = jax.nn.silu(cv_ref[...]).astype(BF16)
    o_ref[0] = _dot(a, w_ref[0].astype(BF16)) + b_ref[0]


def _modulation(cvec, w_mod, b_mod):
    tn = 1536
    n_out = 6 * D_MODEL
    out = pl.pallas_call(
        _mod_kernel,
        grid=(DEPTH, n_out // tn),
        in_specs=[pl.BlockSpec((MOD_ROWS, D_MODEL), lambda l, j: (0, 0)),
                  pl.BlockSpec((1, D_MODEL, tn), lambda l, j: (l, 0, j)),
                  pl.BlockSpec((1, 1, tn), lambda l, j: (l, 0, j))],
        out_specs=pl.BlockSpec((1, MOD_ROWS, tn), lambda l, j: (l, 0, j)),
        out_shape=jax.ShapeDtypeStruct((DEPTH, MOD_ROWS, n_out), F32),
        compiler_params=_params("parallel", "parallel"),
        name="modulation",
    )(cvec, w_mod, b_mod.reshape(DEPTH, 1, n_out))
    return out.reshape(DEPTH, MOD_ROWS, 6, D_MODEL)


def _mod_row(tile_rows, n_ctx, t_lat):
    def row(i):
        start = i * tile_rows
        return jnp.where(start < n_ctx, 0, 1 + (start - n_ctx) // t_lat)
    return row


def _proj_in_kernel(x_ref, mod_ref, nw_ref, w_ref, wg_ref, o_ref, glr_ref, h_scr):
    @pl.when(pl.program_id(1) == 0)
    def _():
        y = _rms(x_ref[...]) * nw_ref[...]
        h = (y * (1.0 + mod_ref[0, 1:2, :]) + mod_ref[0, 0:1, :]).astype(BF16)
        h_scr[...] = h
        glr_ref[...] = _dot(h, wg_ref[...])

    o_ref[...] = _dot(h_scr[...], w_ref[...])


def _proj_in(x, mod, norm_w, w_main, w_glr, row):
    n = x.shape[0]
    tm, tn = 512, 1536
    return pl.pallas_call(
        _proj_in_kernel,
        grid=(n // tm, PROJ_COLS // tn),
        in_specs=[pl.BlockSpec((tm, D_MODEL), lambda i, j: (i, 0)),
                  pl.BlockSpec((1, 6, D_MODEL), lambda i, j: (row(i), 0, 0)),
                  pl.BlockSpec((1, D_MODEL), lambda i, j: (0, 0)),
                  pl.BlockSpec((D_MODEL, tn), lambda i, j: (0, j)),
                  pl.BlockSpec((D_MODEL, LANES), lambda i, j: (0, 0))],
        out_specs=[pl.BlockSpec((tm, tn), lambda i, j: (i, j)),
                   pl.BlockSpec((tm, LANES), lambda i, j: (i, 0))],
        out_shape=[jax.ShapeDtypeStruct((n, PROJ_COLS), F32),
                   jax.ShapeDtypeStruct((n, LANES), F32)],
        scratch_shapes=[pltpu.VMEM((tm, D_MODEL), BF16)],
        compiler_params=_params("parallel", "arbitrary"),
        name="proj_in",
    )(x, mod, norm_w, w_main, w_glr)


GLA_HB = 4


def _gla_kernel(qk_ref, v_ref, glr_ref, wg_ref, bg_ref, swap_ref, mask_ref, init_ref, *rest, n_chunks, in_place):
    o_ref, st_ref, st_scr, x1_scr, k1_scr, x2_scr, k2_scr, vb_scr, dec_scr = rest[1:] if in_place else rest
    d = pl.program_id(2)
    t = pl.program_id(3)
    fwd = d == 0

    @pl.when(t == 0)
    def _():
        st_scr[...] = init_ref[0, 0]

    c = GLA_CHUNK
    hw = GLA_HB * LANES
    mask = mask_ref[0]
    qmask = lax.broadcasted_iota(jnp.int32, (1, hw), 1) % LANES < GLA_DK

    z = _dot(glr_ref[...].astype(BF16), wg_ref[0]) + bg_ref[0]
    la = jax.nn.log_sigmoid(z) / GLA_TAU
    b_all = _dot_exact01(mask.astype(BF16), la)
    vb_scr[...] = v_ref[...].astype(BF16)
    for k in range(n_chunks):
        rows = slice(k * c, (k + 1) * c)
        b = b_all[rows, :]
        b_mid = jnp.where(fwd, b[c // 2 - 1:c // 2, :], b[c // 2:c // 2 + 1, :])
        b_end = jnp.where(fwd, b[c - 1:c, :], b[0:1, :])
        qk = qk_ref[rows, :]
        x1_scr[rows, :] = (qk * jnp.exp(jnp.where(qmask, b - b_mid, b_mid - b))).astype(BF16)
        x2_scr[rows, :] = (qk * jnp.exp(jnp.where(qmask, b, b_end - b))).astype(BF16)
        dec_scr[k] = jnp.exp(b_end)
    k1_scr[...] = _dot(x1_scr[...], swap_ref[...]).astype(BF16)
    k2_scr[...] = _dot(x2_scr[...], swap_ref[...]).astype(BF16)

    for h in range(GLA_HB):
        hl = slice(h * LANES, (h + 1) * LANES)
        s = lax.dot_general(x1_scr[:, hl], k1_scr[:, hl], NT_DIMS, preferred_element_type=F32)
        s = jnp.where(mask > 0.0, s, 0.0).astype(BF16)
        o_ref[0, :, hl] = _dot(s, vb_scr[:, hl])

    def chunk(step, carry):
        k = jnp.where(fwd, step, n_chunks - 1 - step)
        rows = pl.ds(pl.multiple_of(k * c, c), c)
        dec = dec_scr[k]
        for h in range(GLA_HB):
            hl = slice(h * LANES, (h + 1) * LANES)
            st = st_scr[h]
            o_ref[0, rows, hl] += lax.dot_general(x2_scr[rows, hl], st.astype(BF16), NT_DIMS,
                                                  preferred_element_type=F32)
            upd = lax.dot_general(vb_scr[rows, hl], k2_scr[rows, hl], TN_DIMS, preferred_element_type=F32)
            st_scr[h] = st * dec[:, hl] + upd
        return carry

    lax.fori_loop(0, n_chunks, chunk, 0)

    @pl.when(t == pl.num_programs(3) - 1)
    def _():
        st_ref[0, 0] = st_scr[...]


def _gla(proj, glr, wg, bg, init_t, out_prev, *, row0, batch, seq, tile):
    n_all = proj.shape[0]
    nt = seq // tile
    rb0 = row0 // tile
    hw = GLA_HB * LANES
    n_chunks = tile // GLA_CHUNK
    l = np.arange(hw)
    swap = ((l[:, None] == l[None, :] + GLA_DK) & (l[None, :] % LANES < GLA_DK)).astype(np.float32)
    r = np.arange(tile)
    same = r[:, None] // GLA_CHUNK == r[None, :] // GLA_CHUNK
    mask = np.stack([same & (r[:, None] >= r[None, :]), same & (r[:, None] <= r[None, :])]).astype(np.float32)

    def rblk(b, t, d):
        return rb0 + b * nt + jnp.where(d == 0, t, nt - 1 - t)

    in_specs = [pl.BlockSpec((tile, hw), lambda b, g, d, t: (rblk(b, t, d), COL_QK // hw + g)),
                pl.BlockSpec((tile, hw), lambda b, g, d, t: (rblk(b, t, d), COL_V // hw + g)),
                pl.BlockSpec((tile, LANES), lambda b, g, d, t: (rblk(b, t, d), 0)),
                pl.BlockSpec((1, LANES, hw), lambda b, g, d, t: (d, 0, g)),
                pl.BlockSpec((1, 1, hw), lambda b, g, d, t: (d, 0, g)),
                pl.BlockSpec((hw, hw), lambda b, g, d, t: (0, 0)),
                pl.BlockSpec((1, tile, tile), lambda b, g, d, t: (d, 0, 0)),
                pl.BlockSpec((1, 1, GLA_HB, LANES, LANES), lambda b, g, d, t: (b, d, g, 0, 0))]
    args = [proj, proj, glr, wg, bg, jnp.asarray(swap, BF16), jnp.asarray(mask), init_t]
    aliases = {}
    if out_prev is not None:
        aliases = {len(args): 0}
        in_specs.append(pl.BlockSpec(memory_space=pl.ANY))
        args.append(out_prev)
    return pl.pallas_call(
        functools.partial(_gla_kernel, n_chunks=n_chunks, in_place=out_prev is not None),
        grid=(batch, GLA_HEADS // GLA_HB, 2, nt),
        in_specs=in_specs,
        out_specs=[pl.BlockSpec((1, tile, hw), lambda b, g, d, t: (d, rblk(b, t, d), g)),
                   pl.BlockSpec((1, 1, GLA_HB, LANES, LANES), lambda b, g, d, t: (b, d, g, 0, 0))],
        out_shape=[jax.ShapeDtypeStruct((2, n_all, GLA_HEADS * GLA_DV), F32),
                   jax.ShapeDtypeStruct((batch, 2, GLA_HEADS, LANES, LANES), F32)],
        scratch_shapes=[pltpu.VMEM((GLA_HB, LANES, LANES), F32)]
                       + [pltpu.VMEM((tile, hw), BF16)] * 5
                       + [pltpu.VMEM((n_chunks, 1, hw), F32)],
        input_output_aliases=aliases,
        compiler_params=_params("parallel", "parallel", "arbitrary", "arbitrary"),
        name="gla",
    )(*args)


def _prep_kernel(*refs, rope, cache):
    qa_ref, kv_ref, qw_ref, kw_ref = refs[:4]
    refs = refs[4:]
    if rope:
        cos_ref, sa_ref, sb_ref = refs[:3]
        refs = refs[3:]
    qn_ref, kn_ref, vb_ref = refs[:3]
    if cache:
        ck_ref, cv_ref = refs[3:5]

    def rot(y):
        if not rope:
            return y
        return (y * cos_ref[...] + pltpu.roll(y, LANES - 32, 1) * sa_ref[...]
                + pltpu.roll(y, 32, 1) * sb_ref[...])

    for h in range(ATT_HEADS):
        hl = slice(h * HEAD_DIM, (h + 1) * HEAD_DIM)
        qn_ref[:, hl] = rot(_rms(qa_ref[:, hl]) * qw_ref[...]).astype(BF16)
    for h in range(ATT_KV_HEADS):
        hl = slice(h * HEAD_DIM, (h + 1) * HEAD_DIM)
        vl = slice((ATT_KV_HEADS + h) * HEAD_DIM, (ATT_KV_HEADS + h + 1) * HEAD_DIM)
        kn = _rms(kv_ref[:, hl]) * kw_ref[...]
        v = kv_ref[:, vl]
        kn_ref[:, hl] = rot(kn).astype(BF16)
        vb_ref[:, hl] = v.astype(BF16)
        if cache:
            ck_ref[0, h] = kn
            cv_ref[0, h] = v


def _prep(proj, qw, kw, rope_tabs, *, row0, batch, seq, cache):
    tm = 256
    n = batch * seq
    rb0 = row0 // tm
    per_req = seq // tm
    kvw = ATT_KV_HEADS * HEAD_DIM
    rope = rope_tabs is not None
    in_specs = [pl.BlockSpec((tm, ATT_HEADS * HEAD_DIM), lambda i: (rb0 + i, COL_QA // (ATT_HEADS * HEAD_DIM))),
                pl.BlockSpec((tm, 2 * kvw), lambda i: (rb0 + i, COL_KV // (2 * kvw))),
                pl.BlockSpec((1, HEAD_DIM), lambda i: (0, 0)),
                pl.BlockSpec((1, HEAD_DIM), lambda i: (0, 0))]
    args = [proj, proj, qw, kw]
    if rope:
        in_specs += [pl.BlockSpec((tm, HEAD_DIM), lambda i: (i % per_req, 0))] * 3
        args += list(rope_tabs)
    out_specs = [pl.BlockSpec((tm, ATT_HEADS * HEAD_DIM), lambda i: (i, 0)),
                 pl.BlockSpec((tm, kvw), lambda i: (i, 0)),
                 pl.BlockSpec((tm, kvw), lambda i: (i, 0))]
    out_shape = [jax.ShapeDtypeStruct((n, ATT_HEADS * HEAD_DIM), BF16),
                 jax.ShapeDtypeStruct((n, kvw), BF16),
                 jax.ShapeDtypeStruct((n, kvw), BF16)]
    if cache:
        assert seq == tm
        out_specs += [pl.BlockSpec((1, ATT_KV_HEADS, seq, HEAD_DIM), lambda i: (i, 0, 0, 0))] * 2
        out_shape += [jax.ShapeDtypeStruct((batch, ATT_KV_HEADS, seq, HEAD_DIM), F32)] * 2
    return pl.pallas_call(
        functools.partial(_prep_kernel, rope=rope, cache=cache),
        grid=(n // tm,),
        in_specs=in_specs, out_specs=out_specs, out_shape=out_shape,
        compiler_params=_params("parallel"),
        name="attn_prep",
    )(*args)


def _rope_tables(seq):
    half = HEAD_DIM // 2
    tok = jnp.arange(seq)
    row = (tok // GRID_W).astype(F32)
    col = (tok % GRID_W).astype(F32)
    inv_freq = ROPE_THETA ** (-jnp.arange(0, half, 2, dtype=F32) / half)
    ang_r = row[:, None] * inv_freq[None, :]
    ang_c = col[:, None] * inv_freq[None, :]
    ang = jnp.concatenate([ang_r, ang_r, ang_c, ang_c], axis=-1)
    first = (jnp.arange(HEAD_DIM) % half) < half // 2
    sin = jnp.sin(ang)
    return jnp.cos(ang), jnp.where(first, -sin, 0.0), jnp.where(first, 0.0, sin)


def _attn_kernel(*refs, ctx, in_place):
    o_ref = refs[-1]
    refs = refs[:-2] if in_place else refs[:-1]
    if ctx:
        q_ref, k_ref, v_ref, ck_ref, cv_ref = refs
        ck = ck_ref[0, 0, 0].astype(BF16)
        cv = cv_ref[0, 0, 0].astype(BF16)
    else:
        q_ref, k_ref, v_ref = refs
    k = k_ref[...]
    v = v_ref[...]
    scale = HEAD_DIM ** -0.5
    for g in range(ATT_GROUP):
        gl = slice(g * HEAD_DIM, (g + 1) * HEAD_DIM)
        q = q_ref[:, gl]
        s = lax.dot_general(q, k, NT_DIMS, preferred_element_type=F32)
        m = jnp.max(s, axis=-1, keepdims=True)
        if ctx:
            s2 = lax.dot_general(q, ck, NT_DIMS, preferred_element_type=F32)
            m = jnp.maximum(m, jnp.max(s2, axis=-1, keepdims=True))
        p = jnp.exp((s - m) * scale)
        l = jnp.sum(p, axis=-1, keepdims=True)
        acc = _dot(p.astype(BF16), v)
        if ctx:
            p2 = jnp.exp((s2 - m) * scale)
            l = l + jnp.sum(p2, axis=-1, keepdims=True)
            acc = acc + _dot(p2.astype(BF16), cv)
        o_ref[:, gl] = (acc / l).astype(BF16)


def _attention(qn, kn, vb, ctx_kv, out_prev, *, row0, n_all, batch, seq, tq):
    nq = seq // tq
    ob0 = row0 // tq
    gw = ATT_GROUP * HEAD_DIM
    in_specs = [pl.BlockSpec((tq, gw), lambda b, h, t: (b * nq + t, h)),
                pl.BlockSpec((seq, HEAD_DIM), lambda b, h, t: (b, h)),
                pl.BlockSpec((seq, HEAD_DIM), lambda b, h, t: (b, h))]
    args = [qn, kn, vb]
    if ctx_kv is not None:
        ck, cv, layer = ctx_kv
        past = ck.shape[3]
        spec = pl.BlockSpec((1, 1, 1, past, HEAD_DIM), lambda b, h, t: (b, layer, h, 0, 0))
        in_specs += [spec, spec]
        args += [ck, cv]
    aliases = {}
    if out_prev is not None:
        aliases = {len(args): 0}
        in_specs.append(pl.BlockSpec(memory_space=pl.ANY))
        args.append(out_prev)
    return pl.pallas_call(
        functools.partial(_attn_kernel, ctx=ctx_kv is not None, in_place=out_prev is not None),
        grid=(batch, ATT_KV_HEADS, nq),
        in_specs=in_specs,
        out_specs=pl.BlockSpec((tq, gw), lambda b, h, t: (ob0 + b * nq + t, h)),
        out_shape=jax.ShapeDtypeStruct((n_all, ATT_HEADS * HEAD_DIM), BF16),
        input_output_aliases=aliases,
        compiler_params=_params("parallel", "parallel", "arbitrary"),
        name="attention",
    )(*args)


def _out_proj_kernel(x_ref, od_ref, og_ref, oa_ref, gn_ref, w_ref, mod_ref, nw_ref, x1_ref, h2_ref):
    o = od_ref[0] + od_ref[1]
    parts = []
    for h in range(GLA_HEADS):
        hl = slice(h * GLA_DV, (h + 1) * GLA_DV)
        parts.append((_rms(o[:, hl]) * gn_ref[...] * jax.nn.silu(og_ref[:, hl])).astype(BF16))
    og = jnp.concatenate(parts, axis=-1)
    gw = GLA_HEADS * GLA_DV
    mix = _dot(og, w_ref[0:gw, :]) + _dot(oa_ref[...], w_ref[gw:, :])
    x1 = x_ref[...] + mod_ref[0, 2:3, :] * mix
    x1_ref[...] = x1
    h2_ref[...] = _rms(x1) * nw_ref[...] * (1.0 + mod_ref[0, 4:5, :]) + mod_ref[0, 3:4, :]


def _out_proj(x, odir, proj, oa, gla_norm, w_out, mod, norm_w, row):
    n = x.shape[0]
    tm = 256
    gw = GLA_HEADS * GLA_DV
    return pl.pallas_call(
        _out_proj_kernel,
        grid=(n // tm,),
        in_specs=[pl.BlockSpec((tm, D_MODEL), lambda i: (i, 0)),
                  pl.BlockSpec((2, tm, gw), lambda i: (0, i, 0)),
                  pl.BlockSpec((tm, gw), lambda i: (i, COL_OG // gw)),
                  pl.BlockSpec((tm, ATT_HEADS * HEAD_DIM), lambda i: (i, 0)),
                  pl.BlockSpec((1, GLA_DV), lambda i: (0, 0)),
                  pl.BlockSpec((gw + ATT_HEADS * HEAD_DIM, D_MODEL), lambda i: (0, 0)),
                  pl.BlockSpec((1, 6, D_MODEL), lambda i: (row(i), 0, 0)),
                  pl.BlockSpec((1, D_MODEL), lambda i: (0, 0))],
        out_specs=[pl.BlockSpec((tm, D_MODEL), lambda i: (i, 0)),
                   pl.BlockSpec((tm, D_MODEL), lambda i: (i, 0))],
        out_shape=[jax.ShapeDtypeStruct((n, D_MODEL), F32),
                   jax.ShapeDtypeStruct((n, D_MODEL), F32)],
        compiler_params=_params("parallel"),
        name="out_proj",
    )(x, odir, proj, oa, gla_norm, w_out, mod, norm_w)


NEG_INF = float("-inf")
N_CAND = PEER_TOPK * PEER_TOPK


_CAND_GROUPS = ((0, 0), (0, 8), (1, 0), (2, 0), (3, 0), (4, 0), (5, 0), (6, 0), (7, 0))
N_CAND_ROWS = 8 * (len(_CAND_GROUPS) + 1)


def _cand_consts():
    pos = np.zeros((N_CAND_ROWS, 1), np.float32)
    off = np.zeros((N_CAND_ROWS, 1), np.float32)
    for g, (a, b0) in enumerate(_CAND_GROUPS):
        for j in range(8):
            pos[8 * g + j] = a * PEER_TOPK + b0 + j
            off[8 * g + j] = 0.0 if (a + 1) * (b0 + j + 1) <= PEER_TOPK else NEG_INF
    for j in range(8):
        pos[N_CAND_ROWS - 8 + j] = (8 + j) * PEER_TOPK
    return (jnp.asarray(np.broadcast_to(pos, (N_CAND_ROWS, LANES))),
            jnp.asarray(np.broadcast_to(off, (N_CAND_ROWS, LANES))))


def _pair_rows(first, second):
    rows = [first[a:a + 1, :] + second[b0:b0 + 8, :] for a, b0 in _CAND_GROUPS]
    rows.append(first[8:16, :] + second[0:1, :])
    return jnp.concatenate(rows, axis=0)


def _top16_rows(s, row_id):
    rank = lax.broadcasted_iota(jnp.int32, (PEER_TOPK, s.shape[1]), 0)
    vals = jnp.zeros((PEER_TOPK, s.shape[1]), F32)
    ids = jnp.zeros((PEER_TOPK, s.shape[1]), F32)
    for r in range(PEER_TOPK):
        m = jnp.max(s, axis=0, keepdims=True)
        i = jnp.min(jnp.where(s == m, row_id, float(N_CAND)), axis=0, keepdims=True)
        s = jnp.where(row_id == i, NEG_INF, s)
        vals = jnp.where(rank == r, m, vals)
        ids = jnp.where(rank == r, i, ids)
    return vals, ids


def _peer_topk_kernel(h_ref, wq_ref, keys_ref, pos_ref, off_ref, idx_ref, gate_ref, q_scr):
    tm = h_ref.shape[0]
    q = _dot(h_ref[...].astype(BF16), wq_ref[...]).astype(BF16)
    for a in range(2 * PEER_HEADS):
        q_scr[a] = q[:, a * LANES:(a + 1) * LANES]
    key_id = lax.broadcasted_iota(jnp.int32, (N_KEYS, LANES), 0).astype(F32)
    rank = lax.broadcasted_iota(jnp.int32, (PEER_TOPK, LANES), 0)
    pos = pos_ref[...]
    off = off_ref[...]

    def head(h, carry):
        out_rows = pl.ds(pl.multiple_of(h * PEER_TOPK, PEER_TOPK), PEER_TOPK)
        for c in range(tm // LANES):
            cols = slice(c * LANES, (c + 1) * LANES)
            s1 = lax.dot_general(keys_ref[h, 0], q_scr[2 * h, cols, :], NT_DIMS, preferred_element_type=F32)
            s2 = lax.dot_general(keys_ref[h, 1], q_scr[2 * h + 1, cols, :], NT_DIMS, preferred_element_type=F32)
            v1, i1 = _top16_rows(s1, key_id)
            v2, i2 = _top16_rows(s2, key_id)
            cand = _pair_rows(v1, v2) + off
            cidx = _pair_rows(i1 * float(N_KEYS), i2)
            best = jnp.zeros((PEER_TOPK, LANES), F32)
            eidx = jnp.zeros((PEER_TOPK, LANES), F32)
            for r in range(PEER_TOPK):
                m = jnp.max(cand, axis=0, keepdims=True)
                p = jnp.min(jnp.where(cand == m, pos, float(N_CAND)), axis=0, keepdims=True)
                sel = pos == p
                e = jnp.max(jnp.where(sel, cidx, -1.0), axis=0, keepdims=True)
                cand = jnp.where(sel, NEG_INF, cand)
                best = jnp.where(rank == r, m, best)
                eidx = jnp.where(rank == r, e, eidx)
            ex = jnp.exp(best - best[0:1, :])
            gate_ref[out_rows, cols] = ex / jnp.sum(ex, axis=0, keepdims=True)
            idx_ref[out_rows, cols] = eidx.astype(jnp.int32)
        return carry

    lax.fori_loop(0, PEER_HEADS, head, 0)


def _peer_topk(h2, wq, keys):
    n = h2.shape[0]
    tm = 256
    qw = PEER_HEADS * PEER_QDIM
    full = lambda shape: pl.BlockSpec(shape, lambda i: (0,) * len(shape))
    return pl.pallas_call(
        _peer_topk_kernel,
        grid=(n // tm,),
        in_specs=[pl.BlockSpec((tm, D_MODEL), lambda i: (i, 0)),
                  full((D_MODEL, qw)),
                  full((PEER_HEADS, 2, N_KEYS, PEER_QDIM // 2)),
                  full((N_CAND_ROWS, LANES)), full((N_CAND_ROWS, LANES))],
        out_specs=[pl.BlockSpec((PEER_SEL, tm), lambda i: (0, i)),
                   pl.BlockSpec((PEER_SEL, tm), lambda i: (0, i))],
        out_shape=[jax.ShapeDtypeStruct((PEER_SEL, n), jnp.int32),
                   jax.ShapeDtypeStruct((PEER_SEL, n), F32)],
        scratch_shapes=[pltpu.VMEM((2 * PEER_HEADS, tm, LANES), BF16)],
        compiler_params=_params("parallel"),
        name="peer_topk",
    )(h2, wq, keys, *_cand_consts())


U32 = jnp.uint32


def _pack_uv_kernel(u_ref, v_ref, o_ref):
    ub = lax.bitcast_convert_type(u_ref[0].astype(BF16).astype(F32), U32)
    vb = lax.bitcast_convert_type(v_ref[0].astype(BF16).astype(F32), U32)
    o_ref[:, 0, :] = ub | (vb >> 16)


def _pack_uv(u, v, layer):
    n, tr = u.shape[1], 512
    spec = pl.BlockSpec((1, tr, D_MODEL), lambda i: (layer, i, 0))
    return pl.pallas_call(
        _pack_uv_kernel, grid=(n // tr,), in_specs=[spec, spec],
        out_specs=pl.BlockSpec((tr, 1, D_MODEL), lambda i: (i, 0, 0)),
        out_shape=jax.ShapeDtypeStruct((n, 1, D_MODEL), U32),
        compiler_params=_params("parallel"), name="pack_uv",
    )(u, v)


PEER_TB = 128
PEER_NBUF = 8


def _peer_mix_kernel(idx_ref, gate_ref, h_ref, x_ref, mod_ref, uv_ref, o_ref, *scratch):
    bufs, sem = scratch[:PEER_NBUF], scratch[PEER_NBUF]
    tb = h_ref.shape[0]
    n_groups = tb // PEER_NBUF

    def issue(t, slot):
        for k in range(PEER_SEL):
            pltpu.make_async_copy(uv_ref.at[idx_ref[k, t]], bufs[slot].at[pl.ds(k, 1), :],
                                  sem.at[slot]).start(priority=k % 2)

    def wait(slot):
        pltpu.make_async_copy(bufs[slot], bufs[slot], sem.at[slot]).wait()

    gate_t = gate_ref[...]
    tok = lax.broadcasted_iota(jnp.int32, (1, tb), 1)
    g2 = mod_ref[0, 5:6, :]

    def token(t, slot, prefetch):
        wait(slot)
        if prefetch:
            issue(t + PEER_NBUF - 1, (slot + PEER_NBUF - 1) % PEER_NBUF)
        x = h_ref[pl.ds(t, 1), :]
        w = bufs[slot][...]
        u = lax.bitcast_convert_type(w & jnp.uint32(0xFFFF0000), F32)
        s = jnp.sum(u * x, axis=-1, keepdims=True)
        g = jnp.sum(jnp.where(tok == t, gate_t, 0.0), axis=-1, keepdims=True)
        act = jax.nn.gelu(s) * g
        v = lax.bitcast_convert_type(w << 16, F32)
        o = jnp.sum(act * v, axis=0, keepdims=True)
        o_ref[pl.ds(t, 1), :] = x_ref[pl.ds(t, 1), :] + g2 * o

    for t0 in range(PEER_NBUF - 1):
        issue(t0, t0)

    def group(g, carry):
        for j in range(PEER_NBUF):
            token(g * PEER_NBUF + j, j, True)
        return carry

    lax.fori_loop(0, n_groups - 1, group, 0)
    for j in range(PEER_NBUF):
        token((n_groups - 1) * PEER_NBUF + j, j, j == 0)


def _peer_mix(idx_t, gates_t, h2, x1, mod, uv, row):
    n = h2.shape[0]
    tb = PEER_TB
    return pl.pallas_call(
        _peer_mix_kernel,
        grid=(n // tb,),
        in_specs=[pl.BlockSpec((PEER_SEL, tb), lambda i: (0, i), memory_space=pltpu.SMEM),
                  pl.BlockSpec((PEER_SEL, tb), lambda i: (0, i)),
                  pl.BlockSpec((tb, D_MODEL), lambda i: (i, 0)),
                  pl.BlockSpec((tb, D_MODEL), lambda i: (i, 0)),
                  pl.BlockSpec((1, 6, D_MODEL), lambda i: (row(i), 0, 0)),
                  pl.BlockSpec(memory_space=pl.ANY)],
        out_specs=pl.BlockSpec((tb, D_MODEL), lambda i: (i, 0)),
        out_shape=jax.ShapeDtypeStruct((n, D_MODEL), F32),
        scratch_shapes=[pltpu.VMEM((PEER_SEL, D_MODEL), U32)] * PEER_NBUF
                       + [pltpu.SemaphoreType.DMA((PEER_NBUF,))],
        compiler_params=_params("arbitrary"),
        name="peer_mix",
    )(idx_t, gates_t, h2, x1, mod, uv)


def _layer_weights(w_in, w_gate, b_gate, w_out, peer_wq, peer_keys):
    o = np.cumsum((0, GLA_HEADS * GLA_DK, GLA_HEADS * GLA_DK, GLA_HEADS * GLA_DV, 2 * GLA_RANK,
                   GLA_HEADS * GLA_DV, ATT_HEADS * HEAD_DIM, ATT_KV_HEADS * HEAD_DIM, ATT_KV_HEADS * HEAD_DIM))
    wq_g = w_in[:, o[0]:o[1]].reshape(D_MODEL, GLA_HEADS, GLA_DK) * (GLA_DK ** -0.5)
    wk_g = w_in[:, o[1]:o[2]].reshape(D_MODEL, GLA_HEADS, GLA_DK)
    w_qk = jnp.concatenate([wq_g, wk_g], axis=-1).reshape(D_MODEL, GLA_HEADS * LANES)
    w_main = jnp.concatenate([w_qk, w_in[:, o[2]:o[3]], w_in[:, o[4]:o[5]], w_in[:, o[5]:o[6]],
                              w_in[:, o[6]:o[7]], w_in[:, o[7]:o[8]]], axis=-1).astype(BF16)
    w_glr = jnp.pad(w_in[:, o[3]:o[4]], ((0, 0), (0, LANES - 2 * GLA_RANK))).astype(BF16)
    wg = w_gate.reshape(2, GLA_RANK, GLA_HEADS, GLA_DK)
    wg = jnp.concatenate([wg, wg], axis=-1).reshape(2, GLA_RANK, GLA_HEADS * LANES)
    wg = jnp.stack([jnp.pad(wg[0], ((0, LANES - GLA_RANK), (0, 0))),
                    jnp.pad(wg[1], ((GLA_RANK, LANES - 2 * GLA_RANK), (0, 0)))]).astype(BF16)
    bg = b_gate.reshape(2, GLA_HEADS, GLA_DK)
    bg = jnp.concatenate([bg, bg], axis=-1).reshape(2, 1, GLA_HEADS * LANES)
    return dict(w_main=w_main, w_glr=w_glr, wg=wg, bg=bg, w_out=w_out.astype(BF16),
                wq=peer_wq.astype(BF16), keys=peer_keys.astype(BF16))


def _state_to_kernel(s):
    st = jnp.swapaxes(s, -1, -2)
    return jnp.pad(st, [(0, 0)] * (st.ndim - 1) + [(0, LANES - GLA_DK)])


def _state_from_kernel(st):
    return jnp.swapaxes(st[..., :GLA_DK], -1, -2)


def kernel(x_prompt, x_sample, cache_k, cache_v, state_gla, c, c_ctx, w_mod, b_mod, norm_mix, norm_ffn,
           w_in, w_gate, b_gate, gla_norm, q_norm, k_norm, w_out, peer_wq, peer_keys, peer_u, peer_v):
    bc, tc, _ = x_prompt.shape
    bl, tl, _ = x_sample.shape
    n_ctx, n_lat = bc * tc, bl * tl
    assert bl + 1 <= MOD_ROWS

    cvec = jnp.concatenate([c_ctx[None, :], c, jnp.zeros((MOD_ROWS - 1 - bl, D_MODEL), F32)], axis=0)
    mod_all = _modulation(cvec, w_mod, b_mod)
    rope_tabs = _rope_tables(tl)
    x = jnp.concatenate([x_prompt.reshape(n_ctx, D_MODEL), x_sample.reshape(n_lat, D_MODEL)], axis=0)
    zero_state = jnp.zeros((bc, 2, GLA_HEADS, LANES, LANES), F32)

    ks, vs, ss = [], [], []
    for l in range(DEPTH):
        w = _layer_weights(w_in[l], w_gate[l], b_gate[l], w_out[l], peer_wq[l], peer_keys[l])
        uv = _pack_uv(peer_u, peer_v, l)
        mod = mod_all[l]
        proj, glr = _proj_in(x, mod, norm_mix[l][None, :], w["w_main"], w["w_glr"], _mod_row(512, n_ctx, tl))

        odir, st_c = _gla(proj, glr, w["wg"], w["bg"], zero_state, None, row0=0, batch=bc, seq=tc, tile=tc)
        odir, _ = _gla(proj, glr, w["wg"], w["bg"], _state_to_kernel(state_gla[:, l]), odir,
                       row0=n_ctx, batch=bl, seq=tl, tile=512)
        ss.append(_state_from_kernel(st_c))

        qw, kw = q_norm[l][None, :], k_norm[l][None, :]
        qn_c, kn_c, vb_c, ck, cv = _prep(proj, qw, kw, None, row0=0, batch=bc, seq=tc, cache=True)
        qn_l, kn_l, vb_l = _prep(proj, qw, kw, rope_tabs, row0=n_ctx, batch=bl, seq=tl, cache=False)
        ks.append(ck)
        vs.append(cv)
        n_all = n_ctx + n_lat
        oa = _attention(qn_c, kn_c, vb_c, None, None, row0=0, n_all=n_all, batch=bc, seq=tc, tq=tc)
        oa = _attention(qn_l, kn_l, vb_l, (cache_k, cache_v, l), oa, row0=n_ctx, n_all=n_all,
                        batch=bl, seq=tl, tq=128)

        x1, h2 = _out_proj(x, odir, proj, oa, gla_norm[l][None, :], w["w_out"], mod, norm_ffn[l][None, :],
                           _mod_row(256, n_ctx, tl))
        idx, gates = _peer_topk(h2, w["wq"], w["keys"])
        x = _peer_mix(idx, gates, h2, x1, mod, uv, _mod_row(PEER_TB, n_ctx, tl))

    y_prompt = x[:n_ctx].reshape(bc, tc, D_MODEL)
    y_sample = x[n_ctx:].reshape(bl, tl, D_MODEL)
    return (y_prompt, y_sample, jnp.stack(ks, axis=1), jnp.stack(vs, axis=1), jnp.stack(ss, axis=1))
```

```python
import functools

import jax
import jax.numpy as jnp
import numpy as np
from jax import lax
from jax.experimental import pallas as pl
from jax.experimental.pallas import tpu as pltpu

F32 = jnp.float32
BF16 = jnp.bfloat16

D_MODEL = 2048
DEPTH = 2
GRID_W = 64
GLA_HEADS = 8
GLA_DK = 64
GLA_DV = 128
GLA_RANK = 16
GLA_TAU = 16.0
GLA_CHUNK = 64
ATT_HEADS = 8
ATT_KV_HEADS = 2
ATT_GROUP = ATT_HEADS // ATT_KV_HEADS
HEAD_DIM = 128
ROPE_THETA = 10000.0
PEER_HEADS = 8
PEER_QDIM = 256
N_KEYS = 128
PEER_TOPK = 16
PEER_SEL = PEER_HEADS * PEER_TOPK
EPS = 1e-6
LOG2_E = 1.4426950408889634

LANES = 128
MOD_ROWS = 8
VMEM_LIMIT = 56 * 1024 * 1024

COL_QK = 0
COL_V = COL_QK + GLA_HEADS * LANES
COL_OG = COL_V + GLA_HEADS * GLA_DV
COL_QA = COL_OG + GLA_HEADS * GLA_DV
COL_KV = COL_QA + ATT_HEADS * HEAD_DIM
PROJ_COLS = COL_KV + 2 * ATT_KV_HEADS * HEAD_DIM

NT_DIMS = (((1,), (1,)), ((), ()))
TN_DIMS = (((0,), (0,)), ((), ()))


def _params(*sem):
    return pltpu.CompilerParams(dimension_semantics=sem, vmem_limit_bytes=VMEM_LIMIT)


def _dot(a, b):
    return jnp.dot(a, b, preferred_element_type=F32)


def _dot_exact01(a01, x):
    hi = x.astype(BF16)
    r1 = x - hi.astype(F32)
    mid = r1.astype(BF16)
    lo = (r1 - mid.astype(F32)).astype(BF16)
    return _dot(a01, hi) + _dot(a01, mid) + _dot(a01, lo)


def _dot_exact01_rhs(x, b01):
    hi = x.astype(BF16)
    r1 = x - hi.astype(F32)
    mid = r1.astype(BF16)
    lo = (r1 - mid.astype(F32)).astype(BF16)
    return _dot(hi, b01) + _dot(mid, b01) + _dot(lo, b01)


def _rms(x):
    return x * lax.rsqrt(jnp.mean(x * x, axis=-1, keepdims=True) + EPS)


def _mod_kernel(cv_ref, w_ref, b_ref, o_ref):
    a = jax.nn.silu(cv_ref[...]).astype(BF16)
    o_ref[0] = _dot(a, w_ref[0].astype(BF16)) + b_ref[0]


def _modulation(cvec, w_mod, b_mod):
    tn = 1536
    n_out = 6 * D_MODEL
    out = pl.pallas_call(
        _mod_kernel,
        grid=(DEPTH, n_out // tn),
        in_specs=[pl.BlockSpec((MOD_ROWS, D_MODEL), lambda l, j: (0, 0)),
                  pl.BlockSpec((1, D_MODEL, tn), lambda l, j: (l, 0, j)),
                  pl.BlockSpec((1, 1, tn), lambda l, j: (l, 0, j))],
        out_specs=pl.BlockSpec((1, MOD_ROWS, tn), lambda l, j: (l, 0, j)),
        out_shape=jax.ShapeDtypeStruct((DEPTH, MOD_ROWS, n_out), F32),
        compiler_params=_params("parallel", "parallel"),
        name="modulation",
    )(cvec, w_mod, b_mod.reshape(DEPTH, 1, n_out))
    return out.reshape(DEPTH, MOD_ROWS, 6, D_MODEL)


def _mod_row(tile_rows, n_ctx, t_lat):
    def row(i):
        start = i * tile_rows
        return jnp.where(start < n_ctx, 0, 1 + (start - n_ctx) // t_lat)
    return row


def _proj_in_kernel(x_ref, mod_ref, nw_ref, w_ref, wg_ref, o_ref, glr_ref, h_scr):
    @pl.when(pl.program_id(1) == 0)
    def _():
        y = _rms(x_ref[...]) * nw_ref[...]
        h = (y * (1.0 + mod_ref[0, 1:2, :]) + mod_ref[0, 0:1, :]).astype(BF16)
        h_scr[...] = h
        glr_ref[...] = _dot(h, wg_ref[...])

    o_ref[...] = _dot(h_scr[...], w_ref[...])


def _proj_in(x, mod, norm_w, w_main, w_glr, row):
    n = x.shape[0]
    tm, tn = 512, 1536
    return pl.pallas_call(
        _proj_in_kernel,
        grid=(n // tm, PROJ_COLS // tn),
        in_specs=[pl.BlockSpec((tm, D_MODEL), lambda i, j: (i, 0)),
                  pl.BlockSpec((1, 6, D_MODEL), lambda i, j: (row(i), 0, 0)),
                  pl.BlockSpec((1, D_MODEL), lambda i, j: (0, 0)),
                  pl.BlockSpec((D_MODEL, tn), lambda i, j: (0, j)),
                  pl.BlockSpec((D_MODEL, LANES), lambda i, j: (0, 0))],
        out_specs=[pl.BlockSpec((tm, tn), lambda i, j: (i, j)),
                   pl.BlockSpec((tm, LANES), lambda i, j: (i, 0))],
        out_shape=[jax.ShapeDtypeStruct((n, PROJ_COLS), F32),
                   jax.ShapeDtypeStruct((n, LANES), F32)],
        scratch_shapes=[pltpu.VMEM((tm, D_MODEL), BF16)],
        compiler_params=_params("parallel", "arbitrary"),
        name="proj_in",
    )(x, mod, norm_w, w_main, w_glr)


GLA_HB = 4


def _gla_kernel(qk_ref, v_ref, glr_ref, wg_ref, bg_ref, swap_ref, mask_ref, init_ref, *rest, n_chunks, in_place):
    (o_ref, st_ref, st_scr, x1_scr, k1_scr, x2_scr, k2_scr, vb_scr, dec_scr, upd_scr,
     sp_scr) = rest[1:] if in_place else rest
    d = pl.program_id(2)
    t = pl.program_id(3)
    fwd = d == 0

    @pl.when(t == 0)
    def _():
        st_scr[...] = init_ref[0, 0]

    c = GLA_CHUNK
    hw = GLA_HB * LANES
    mask = mask_ref[0]
    qmask = lax.broadcasted_iota(jnp.int32, (1, hw), 1) % LANES < GLA_DK

    z = _dot(glr_ref[...].astype(BF16), wg_ref[0]) + bg_ref[0]
    la = jax.nn.log_sigmoid(z) / GLA_TAU
    blk = 2 * c
    m01 = mask[0:blk, 0:blk].astype(BF16)
    b_all = jnp.concatenate([_dot_exact01(m01, la[r:r + blk, :]) for r in range(0, n_chunks * c, blk)], axis=0)
    vb_scr[...] = v_ref[...].astype(BF16)
    for k in range(n_chunks):
        rows = slice(k * c, (k + 1) * c)
        b = b_all[rows, :]
        b_mid = jnp.where(fwd, b[c // 2 - 1:c // 2, :], b[c // 2:c // 2 + 1, :])
        b_end = jnp.where(fwd, b[c - 1:c, :], b[0:1, :])
        qk = qk_ref[rows, :]
        x1_scr[rows, :] = (qk * jnp.exp(jnp.where(qmask, b - b_mid, b_mid - b))).astype(BF16)
        x2_scr[rows, :] = (qk * jnp.exp(jnp.where(qmask, b, b_end - b))).astype(BF16)
        dec_scr[k] = jnp.exp(b_end)
    swap = swap_ref[...]
    for h in range(GLA_HB):
        hl = slice(h * LANES, (h + 1) * LANES)
        k1_scr[:, hl] = _dot(x1_scr[:, hl], swap).astype(BF16)
        k2_scr[:, hl] = _dot(x2_scr[:, hl], swap).astype(BF16)

    for h in range(GLA_HB):
        hl = slice(h * LANES, (h + 1) * LANES)
        s = lax.dot_general(x1_scr[:, hl], k1_scr[:, hl], NT_DIMS, preferred_element_type=F32)
        s = jnp.where(mask > 0.0, s, 0.0).astype(BF16)
        o_ref[0, :, hl] = _dot(s, vb_scr[:, hl])

    for k in range(n_chunks):
        rows = slice(k * c, (k + 1) * c)
        for h in range(GLA_HB):
            hl = slice(h * LANES, (h + 1) * LANES)
            upd_scr[k, h] = lax.dot_general(vb_scr[rows, hl], k2_scr[rows, hl], TN_DIMS,
                                            preferred_element_type=F32)

    def chunk(step, carry):
        k = jnp.where(fwd, step, n_chunks - 1 - step)
        dec = dec_scr[k]
        for h in range(GLA_HB):
            st = st_scr[h]
            sp_scr[k, h] = st.astype(BF16)
            st_scr[h] = st * dec[:, h * LANES:(h + 1) * LANES] + upd_scr[k, h]
        return carry

    lax.fori_loop(0, n_chunks, chunk, 0)

    for k in range(n_chunks):
        rows = slice(k * c, (k + 1) * c)
        for h in range(GLA_HB):
            hl = slice(h * LANES, (h + 1) * LANES)
            o_ref[0, rows, hl] += lax.dot_general(x2_scr[rows, hl], sp_scr[k, h], NT_DIMS,
                                                  preferred_element_type=F32)

    @pl.when(t == pl.num_programs(3) - 1)
    def _():
        st_ref[0, 0] = st_scr[...]


def _gla(proj, glr, wg, bg, init_t, out_prev, *, row0, batch, seq, tile):
    n_all = proj.shape[0]
    nt = seq // tile
    rb0 = row0 // tile
    hw = GLA_HB * LANES
    n_chunks = tile // GLA_CHUNK
    l = np.arange(LANES)
    swap = ((l[:, None] == l[None, :] + GLA_DK) & (l[None, :] < GLA_DK)).astype(np.float32)
    r = np.arange(tile)
    same = r[:, None] // GLA_CHUNK == r[None, :] // GLA_CHUNK
    mask = np.stack([same & (r[:, None] >= r[None, :]), same & (r[:, None] <= r[None, :])]).astype(np.float32)

    def rblk(b, t, d):
        return rb0 + b * nt + jnp.where(d == 0, t, nt - 1 - t)

    in_specs = [pl.BlockSpec((tile, hw), lambda b, g, d, t: (rblk(b, t, d), COL_QK // hw + g)),
                pl.BlockSpec((tile, hw), lambda b, g, d, t: (rblk(b, t, d), COL_V // hw + g)),
                pl.BlockSpec((tile, LANES), lambda b, g, d, t: (rblk(b, t, d), 0)),
                pl.BlockSpec((1, LANES, hw), lambda b, g, d, t: (d, 0, g)),
                pl.BlockSpec((1, 1, hw), lambda b, g, d, t: (d, 0, g)),
                pl.BlockSpec((LANES, LANES), lambda b, g, d, t: (0, 0)),
                pl.BlockSpec((1, tile, tile), lambda b, g, d, t: (d, 0, 0)),
                pl.BlockSpec((1, 1, GLA_HB, LANES, LANES), lambda b, g, d, t: (b, d, g, 0, 0))]
    args = [proj, proj, glr, wg, bg, jnp.asarray(swap, BF16), jnp.asarray(mask), init_t]
    aliases = {}
    if out_prev is not None:
        aliases = {len(args): 0}
        in_specs.append(pl.BlockSpec(memory_space=pl.ANY))
        args.append(out_prev)
    return pl.pallas_call(
        functools.partial(_gla_kernel, n_chunks=n_chunks, in_place=out_prev is not None),
        grid=(batch, GLA_HEADS // GLA_HB, 2, nt),
        in_specs=in_specs,
        out_specs=[pl.BlockSpec((1, tile, hw), lambda b, g, d, t: (d, rblk(b, t, d), g)),
                   pl.BlockSpec((1, 1, GLA_HB, LANES, LANES), lambda b, g, d, t: (b, d, g, 0, 0))],
        out_shape=[jax.ShapeDtypeStruct((2, n_all, GLA_HEADS * GLA_DV), F32),
                   jax.ShapeDtypeStruct((batch, 2, GLA_HEADS, LANES, LANES), F32)],
        scratch_shapes=[pltpu.VMEM((GLA_HB, LANES, LANES), F32)]
                       + [pltpu.VMEM((tile, hw), BF16)] * 5
                       + [pltpu.VMEM((n_chunks, 1, hw), F32),
                          pltpu.VMEM((n_chunks, GLA_HB, LANES, LANES), F32),
                          pltpu.VMEM((n_chunks, GLA_HB, LANES, LANES), BF16)],
        input_output_aliases=aliases,
        compiler_params=_params("parallel", "parallel", "arbitrary", "arbitrary"),
        name="gla",
    )(*args)


def _prep_kernel(*refs, rope, cache):
    qa_ref, kv_ref, qw_ref, kw_ref = refs[:4]
    refs = refs[4:]
    if rope:
        cos_ref, sa_ref, sb_ref = refs[:3]
        refs = refs[3:]
    qn_ref, kn_ref, vb_ref = refs[:3]
    if cache:
        ck_ref, cv_ref = refs[3:5]

    def rot(y):
        if not rope:
            return y
        return (y * cos_ref[...] + pltpu.roll(y, LANES - 32, 1) * sa_ref[...]
                + pltpu.roll(y, 32, 1) * sb_ref[...])

    for h in range(ATT_HEADS):
        hl = slice(h * HEAD_DIM, (h + 1) * HEAD_DIM)
        qn_ref[:, hl] = rot(_rms(qa_ref[:, hl]) * qw_ref[...]).astype(BF16)
    for h in range(ATT_KV_HEADS):
        hl = slice(h * HEAD_DIM, (h + 1) * HEAD_DIM)
        vl = slice((ATT_KV_HEADS + h) * HEAD_DIM, (ATT_KV_HEADS + h + 1) * HEAD_DIM)
        kn = _rms(kv_ref[:, hl]) * kw_ref[...]
        v = kv_ref[:, vl]
        kn_ref[:, hl] = rot(kn).astype(BF16)
        vb_ref[:, hl] = v.astype(BF16)
        if cache:
            ck_ref[0, h] = kn
            cv_ref[0, h] = v


def _prep(proj, qw, kw, rope_tabs, *, row0, batch, seq, cache):
    tm = 256
    n = batch * seq
    rb0 = row0 // tm
    per_req = seq // tm
    kvw = ATT_KV_HEADS * HEAD_DIM
    rope = rope_tabs is not None
    in_specs = [pl.BlockSpec((tm, ATT_HEADS * HEAD_DIM), lambda i: (rb0 + i, COL_QA // (ATT_HEADS * HEAD_DIM))),
                pl.BlockSpec((tm, 2 * kvw), lambda i: (rb0 + i, COL_KV // (2 * kvw))),
                pl.BlockSpec((1, HEAD_DIM), lambda i: (0, 0)),
                pl.BlockSpec((1, HEAD_DIM), lambda i: (0, 0))]
    args = [proj, proj, qw, kw]
    if rope:
        in_specs += [pl.BlockSpec((tm, HEAD_DIM), lambda i: (i % per_req, 0))] * 3
        args += list(rope_tabs)
    out_specs = [pl.BlockSpec((tm, ATT_HEADS * HEAD_DIM), lambda i: (i, 0)),
                 pl.BlockSpec((tm, kvw), lambda i: (i, 0)),
                 pl.BlockSpec((tm, kvw), lambda i: (i, 0))]
    out_shape = [jax.ShapeDtypeStruct((n, ATT_HEADS * HEAD_DIM), BF16),
                 jax.ShapeDtypeStruct((n, kvw), BF16),
                 jax.ShapeDtypeStruct((n, kvw), BF16)]
    if cache:
        assert seq == tm
        out_specs += [pl.BlockSpec((1, ATT_KV_HEADS, seq, HEAD_DIM), lambda i: (i, 0, 0, 0))] * 2
        out_shape += [jax.ShapeDtypeStruct((batch, ATT_KV_HEADS, seq, HEAD_DIM), F32)] * 2
    return pl.pallas_call(
        functools.partial(_prep_kernel, rope=rope, cache=cache),
        grid=(n // tm,),
        in_specs=in_specs, out_specs=out_specs, out_shape=out_shape,
        compiler_params=_params("parallel"),
        name="attn_prep",
    )(*args)


def _rope_tables(seq):
    half = HEAD_DIM // 2
    tok = jnp.arange(seq)
    row = (tok // GRID_W).astype(F32)
    col = (tok % GRID_W).astype(F32)
    inv_freq = ROPE_THETA ** (-jnp.arange(0, half, 2, dtype=F32) / half)
    ang_r = row[:, None] * inv_freq[None, :]
    ang_c = col[:, None] * inv_freq[None, :]
    ang = jnp.concatenate([ang_r, ang_r, ang_c, ang_c], axis=-1)
    first = (jnp.arange(HEAD_DIM) % half) < half // 2
    sin = jnp.sin(ang)
    return jnp.cos(ang), jnp.where(first, -sin, 0.0), jnp.where(first, 0.0, sin)


ATT_COL_BLOCK = 512

def _attn_kernel(*refs, ctx, in_place):
    o_ref, s_scr = refs[-2], refs[-1]
    refs = refs[:-3] if in_place else refs[:-2]
    if ctx:
        q_ref, k_ref, v_ref, ck_ref, cv_ref = refs
        ck = ck_ref[0, 0, 0].astype(BF16)
        cv = cv_ref[0, 0, 0].astype(BF16)
    else:
        q_ref, k_ref, v_ref = refs
    k = k_ref[...]
    v = v_ref[...]
    c2 = (HEAD_DIM ** -0.5) * LOG2_E

    n_keys = k_ref.shape[0]
    blocks = [(k, v, r, min(ATT_COL_BLOCK, n_keys - r), r) for r in range(0, n_keys, ATT_COL_BLOCK)]
    if ctx:
        past = ck.shape[0]
        blocks += [(ck, cv, r, min(ATT_COL_BLOCK, past - r), n_keys + r) for r in range(0, past, ATT_COL_BLOCK)]

    def scores(g):
        q = q_ref[:, g * HEAD_DIM:(g + 1) * HEAD_DIM]
        for kk, _, r, w, col in blocks:
            s_scr[g % 2, :, col:col + w] = lax.dot_general(q, kk[r:r + w, :], NT_DIMS, preferred_element_type=F32)

    def lane_fold(x, op):
        parts = [x[:, i:i + LANES] for i in range(0, x.shape[1], LANES)]
        while len(parts) > 1:
            parts = [op(parts[i], parts[i + 1]) for i in range(0, len(parts) - 1, 2)] + parts[len(parts) & ~1:]
        return parts[0]

    def finish(g):
        sg = s_scr.at[g % 2]
        mp = None
        for _, _, _, w, col in blocks:
            part = lane_fold(sg[:, col:col + w], jnp.maximum)
            mp = part if mp is None else jnp.maximum(mp, part)
        mc = jnp.max(mp, axis=-1, keepdims=True) * c2
        lp = jnp.zeros((q_ref.shape[0], LANES), F32)
        acc = jnp.zeros((q_ref.shape[0], HEAD_DIM), F32)
        for _, vv, r, w, col in blocks:
            p = jnp.exp2(sg[:, col:col + w] * c2 - mc)
            lp = lp + lane_fold(p, jnp.add)
            acc = acc + _dot(p.astype(BF16), vv[r:r + w, :])
        l = jnp.sum(lp, axis=-1, keepdims=True)
        o_ref[:, g * HEAD_DIM:(g + 1) * HEAD_DIM] = (acc / l).astype(BF16)

    scores(0)
    for g in range(ATT_GROUP):
        if g + 1 < ATT_GROUP:
            scores(g + 1)
        finish(g)


def _attention(qn, kn, vb, ctx_kv, out_prev, *, row0, n_all, batch, seq, tq):
    nq = seq // tq
    ob0 = row0 // tq
    past = ctx_kv[0].shape[3] if ctx_kv is not None else 0
    assert seq % LANES == 0 and past % LANES == 0
    gw = ATT_GROUP * HEAD_DIM
    in_specs = [pl.BlockSpec((tq, gw), lambda b, h, t: (b * nq + t, h)),
                pl.BlockSpec((seq, HEAD_DIM), lambda b, h, t: (b, h)),
                pl.BlockSpec((seq, HEAD_DIM), lambda b, h, t: (b, h))]
    args = [qn, kn, vb]
    if ctx_kv is not None:
        ck, cv, layer = ctx_kv
        spec = pl.BlockSpec((1, 1, 1, past, HEAD_DIM), lambda b, h, t: (b, layer, h, 0, 0))
        in_specs += [spec, spec]
        args += [ck, cv]
    aliases = {}
    if out_prev is not None:
        aliases = {len(args): 0}
        in_specs.append(pl.BlockSpec(memory_space=pl.ANY))
        args.append(out_prev)
    return pl.pallas_call(
        functools.partial(_attn_kernel, ctx=ctx_kv is not None, in_place=out_prev is not None),
        grid=(batch, ATT_KV_HEADS, nq),
        in_specs=in_specs,
        out_specs=pl.BlockSpec((tq, gw), lambda b, h, t: (ob0 + b * nq + t, h)),
        out_shape=jax.ShapeDtypeStruct((n_all, ATT_HEADS * HEAD_DIM), BF16),
        scratch_shapes=[pltpu.VMEM((2, tq, seq + past), F32)],
        input_output_aliases=aliases,
        compiler_params=_params("parallel", "parallel", "arbitrary"),
        name="attention",
    )(*args)


def _out_proj_kernel(x_ref, od_ref, og_ref, oa_ref, gn_ref, w_ref, mod_ref, nw_ref, x1_ref, h2_ref):
    o = od_ref[0] + od_ref[1]
    parts = []
    for h in range(GLA_HEADS):
        hl = slice(h * GLA_DV, (h + 1) * GLA_DV)
        parts.append((_rms(o[:, hl]) * gn_ref[...] * jax.nn.silu(og_ref[:, hl])).astype(BF16))
    og = jnp.concatenate(parts, axis=-1)
    gw = GLA_HEADS * GLA_DV
    mix = _dot(og, w_ref[0:gw, :]) + _dot(oa_ref[...], w_ref[gw:, :])
    x1 = x_ref[...] + mod_ref[0, 2:3, :] * mix
    x1_ref[...] = x1
    h2_ref[...] = _rms(x1) * nw_ref[...] * (1.0 + mod_ref[0, 4:5, :]) + mod_ref[0, 3:4, :]


def _out_proj(x, odir, proj, oa, gla_norm, w_out, mod, norm_w, row):
    n = x.shape[0]
    tm = 256
    gw = GLA_HEADS * GLA_DV
    return pl.pallas_call(
        _out_proj_kernel,
        grid=(n // tm,),
        in_specs=[pl.BlockSpec((tm, D_MODEL), lambda i: (i, 0)),
                  pl.BlockSpec((2, tm, gw), lambda i: (0, i, 0)),
                  pl.BlockSpec((tm, gw), lambda i: (i, COL_OG // gw)),
                  pl.BlockSpec((tm, ATT_HEADS * HEAD_DIM), lambda i: (i, 0)),
                  pl.BlockSpec((1, GLA_DV), lambda i: (0, 0)),
                  pl.BlockSpec((gw + ATT_HEADS * HEAD_DIM, D_MODEL), lambda i: (0, 0)),
                  pl.BlockSpec((1, 6, D_MODEL), lambda i: (row(i), 0, 0)),
                  pl.BlockSpec((1, D_MODEL), lambda i: (0, 0))],
        out_specs=[pl.BlockSpec((tm, D_MODEL), lambda i: (i, 0)),
                   pl.BlockSpec((tm, D_MODEL), lambda i: (i, 0))],
        out_shape=[jax.ShapeDtypeStruct((n, D_MODEL), F32),
                   jax.ShapeDtypeStruct((n, D_MODEL), F32)],
        compiler_params=_params("parallel"),
        name="out_proj",
    )(x, odir, proj, oa, gla_norm, w_out, mod, norm_w)


NEG_INF = float("-inf")
N_CAND = PEER_TOPK * PEER_TOPK


_CAND_GROUPS = ((0, 0), (0, 8), (1, 0), (2, 0), (3, 0), (4, 0), (5, 0), (6, 0), (7, 0))
N_CAND_ROWS = 8 * (len(_CAND_GROUPS) + 1)


def _cand_consts():
    pos = np.zeros((N_CAND_ROWS, 1), np.float32)
    off = np.zeros((N_CAND_ROWS, 1), np.float32)
    for g, (a, b0) in enumerate(_CAND_GROUPS):
        for j in range(8):
            pos[8 * g + j] = a * PEER_TOPK + b0 + j
            off[8 * g + j] = 0.0 if (a + 1) * (b0 + j + 1) <= PEER_TOPK else NEG_INF
    for j in range(8):
        pos[N_CAND_ROWS - 8 + j] = (8 + j) * PEER_TOPK
    return (jnp.asarray(np.broadcast_to(pos, (N_CAND_ROWS, LANES))),
            jnp.asarray(np.broadcast_to(off, (N_CAND_ROWS, LANES))))


def _pair_rows(first, second):
    rows = [first[a:a + 1, :] + second[b0:b0 + 8, :] for a, b0 in _CAND_GROUPS]
    rows.append(first[8:16, :] + second[0:1, :])
    return jnp.concatenate(rows, axis=0)


def _top16_rows(s, row_id):
    rank = lax.broadcasted_iota(jnp.int32, (PEER_TOPK, s.shape[1]), 0)
    vals = jnp.zeros((PEER_TOPK, s.shape[1]), F32)
    ids = jnp.zeros((PEER_TOPK, s.shape[1]), F32)
    for r in range(PEER_TOPK):
        m = jnp.max(s, axis=0, keepdims=True)
        i = jnp.min(jnp.where(s == m, row_id, float(N_CAND)), axis=0, keepdims=True)
        s = jnp.where(row_id == i, NEG_INF, s)
        vals = jnp.where(rank == r, m, vals)
        ids = jnp.where(rank == r, i, ids)
    return vals, ids


def _peer_topk_kernel(h_ref, wq_ref, keys_ref, pos_ref, off_ref, idx_ref, gate_ref, q_scr):
    tm = h_ref.shape[0]
    q = _dot(h_ref[...].astype(BF16), wq_ref[...]).astype(BF16)
    for a in range(2 * PEER_HEADS):
        q_scr[a] = q[:, a * LANES:(a + 1) * LANES]
    key_id = lax.broadcasted_iota(jnp.int32, (N_KEYS, LANES), 0).astype(F32)
    rank = lax.broadcasted_iota(jnp.int32, (PEER_TOPK, LANES), 0)
    pos = pos_ref[...]
    off = off_ref[...]

    def head(h, carry):
        out_rows = pl.ds(pl.multiple_of(h * PEER_TOPK, PEER_TOPK), PEER_TOPK)
        for c in range(tm // LANES):
            cols = slice(c * LANES, (c + 1) * LANES)
            s1 = lax.dot_general(keys_ref[h, 0], q_scr[2 * h, cols, :], NT_DIMS, preferred_element_type=F32)
            s2 = lax.dot_general(keys_ref[h, 1], q_scr[2 * h + 1, cols, :], NT_DIMS, preferred_element_type=F32)
            v1, i1 = _top16_rows(s1, key_id)
            v2, i2 = _top16_rows(s2, key_id)
            cand = _pair_rows(v1, v2) + off
            cidx = _pair_rows(i1 * float(N_KEYS), i2)
            best = jnp.zeros((PEER_TOPK, LANES), F32)
            eidx = jnp.zeros((PEER_TOPK, LANES), F32)
            for r in range(PEER_TOPK):
                m = jnp.max(cand, axis=0, keepdims=True)
                p = jnp.min(jnp.where(cand == m, pos, float(N_CAND)), axis=0, keepdims=True)
                sel = pos == p
                e = jnp.max(jnp.where(sel, cidx, -1.0), axis=0, keepdims=True)
                cand = jnp.where(sel, NEG_INF, cand)
                best = jnp.where(rank == r, m, best)
                eidx = jnp.where(rank == r, e, eidx)
            ex = jnp.exp(best - best[0:1, :])
            gate_ref[out_rows, cols] = ex / jnp.sum(ex, axis=0, keepdims=True)
            idx_ref[out_rows, cols] = eidx.astype(jnp.int32)
        return carry

    lax.fori_loop(0, PEER_HEADS, head, 0)


def _peer_topk(h2, wq, keys):
    n = h2.shape[0]
    tm = 256
    qw = PEER_HEADS * PEER_QDIM
    full = lambda shape: pl.BlockSpec(shape, lambda i: (0,) * len(shape))
    return pl.pallas_call(
        _peer_topk_kernel,
        grid=(n // tm,),
        in_specs=[pl.BlockSpec((tm, D_MODEL), lambda i: (i, 0)),
                  full((D_MODEL, qw)),
                  full((PEER_HEADS, 2, N_KEYS, PEER_QDIM // 2)),
                  full((N_CAND_ROWS, LANES)), full((N_CAND_ROWS, LANES))],
        out_specs=[pl.BlockSpec((PEER_SEL, tm), lambda i: (0, i)),
                   pl.BlockSpec((PEER_SEL, tm), lambda i: (0, i))],
        out_shape=[jax.ShapeDtypeStruct((PEER_SEL, n), jnp.int32),
                   jax.ShapeDtypeStruct((PEER_SEL, n), F32)],
        scratch_shapes=[pltpu.VMEM((2 * PEER_HEADS, tm, LANES), BF16)],
        compiler_params=_params("parallel"),
        name="peer_topk",
    )(h2, wq, keys, *_cand_consts())


U32 = jnp.uint32


def _pack_uv_kernel(u_ref, v_ref, o_ref):
    ub = lax.bitcast_convert_type(u_ref[0].astype(BF16).astype(F32), U32)
    vb = lax.bitcast_convert_type(v_ref[0].astype(BF16).astype(F32), U32)
    o_ref[:, 0, :] = ub | (vb >> 16)


def _pack_uv(u, v, layer):
    n, tr = u.shape[1], 512
    spec = pl.BlockSpec((1, tr, D_MODEL), lambda i: (layer, i, 0))
    return pl.pallas_call(
        _pack_uv_kernel, grid=(n // tr,), in_specs=[spec, spec],
        out_specs=pl.BlockSpec((tr, 1, D_MODEL), lambda i: (i, 0, 0)),
        out_shape=jax.ShapeDtypeStruct((n, 1, D_MODEL), U32),
        compiler_params=_params("parallel"), name="pack_uv",
    )(u, v)


PEER_TB = 256
PEER_NBUF = 8


def _peer_mix_kernel(idx_ref, gate_ref, h_ref, x_ref, mod_ref, uv_ref, o_ref, *scratch):
    bufs, sem = scratch[:PEER_NBUF], scratch[PEER_NBUF]
    tb = h_ref.shape[0]
    n_groups = tb // PEER_NBUF

    def issue(t, slot):
        for k in range(PEER_SEL):
            pltpu.make_async_copy(uv_ref.at[idx_ref[k, t]], bufs[slot].at[pl.ds(k, 1), :],
                                  sem.at[slot]).start(priority=k % 2)

    def wait(slot):
        pltpu.make_async_copy(bufs[slot], bufs[slot], sem.at[slot]).wait()

    gate_t = gate_ref[...]
    tok = lax.broadcasted_iota(jnp.int32, (1, tb), 1)
    g2 = mod_ref[0, 5:6, :]

    def token(t, slot, prefetch):
        wait(slot)
        if prefetch:
            issue(t + PEER_NBUF - 1, (slot + PEER_NBUF - 1) % PEER_NBUF)
        x = h_ref[pl.ds(t, 1), :]
        w = bufs[slot][...]
        u = lax.bitcast_convert_type(w & jnp.uint32(0xFFFF0000), F32)
        s = jnp.sum(u * x, axis=-1, keepdims=True)
        g = jnp.sum(jnp.where(tok == t, gate_t, 0.0), axis=-1, keepdims=True)
        act = jax.nn.gelu(s) * g
        v = lax.bitcast_convert_type(w << 16, F32)
        o = jnp.sum(act * v, axis=0, keepdims=True)
        o_ref[pl.ds(t, 1), :] = x_ref[pl.ds(t, 1), :] + g2 * o

    for t0 in range(PEER_NBUF - 1):
        issue(t0, t0)

    def group(g, carry):
        for j in range(PEER_NBUF):
            token(g * PEER_NBUF + j, j, True)
        return carry

    lax.fori_loop(0, n_groups - 1, group, 0)
    for j in range(PEER_NBUF):
        token((n_groups - 1) * PEER_NBUF + j, j, j == 0)


def _peer_mix(idx_t, gates_t, h2, x1, mod, uv, row):
    n = h2.shape[0]
    tb = PEER_TB
    return pl.pallas_call(
        _peer_mix_kernel,
        grid=(n // tb,),
        in_specs=[pl.BlockSpec((PEER_SEL, tb), lambda i: (0, i), memory_space=pltpu.SMEM),
                  pl.BlockSpec((PEER_SEL, tb), lambda i: (0, i)),
                  pl.BlockSpec((tb, D_MODEL), lambda i: (i, 0)),
                  pl.BlockSpec((tb, D_MODEL), lambda i: (i, 0)),
                  pl.BlockSpec((1, 6, D_MODEL), lambda i: (row(i), 0, 0)),
                  pl.BlockSpec(memory_space=pl.ANY)],
        out_specs=pl.BlockSpec((tb, D_MODEL), lambda i: (i, 0)),
        out_shape=jax.ShapeDtypeStruct((n, D_MODEL), F32),
        scratch_shapes=[pltpu.VMEM((PEER_SEL, D_MODEL), U32)] * PEER_NBUF
                       + [pltpu.SemaphoreType.DMA((PEER_NBUF,))],
        compiler_params=_params("arbitrary"),
        name="peer_mix",
    )(idx_t, gates_t, h2, x1, mod, uv)


def _layer_weights(w_in, w_gate, b_gate, w_out, peer_wq, peer_keys):
    o = np.cumsum((0, GLA_HEADS * GLA_DK, GLA_HEADS * GLA_DK, GLA_HEADS * GLA_DV, 2 * GLA_RANK,
                   GLA_HEADS * GLA_DV, ATT_HEADS * HEAD_DIM, ATT_KV_HEADS * HEAD_DIM, ATT_KV_HEADS * HEAD_DIM))
    wq_g = w_in[:, o[0]:o[1]].reshape(D_MODEL, GLA_HEADS, GLA_DK) * (GLA_DK ** -0.5)
    wk_g = w_in[:, o[1]:o[2]].reshape(D_MODEL, GLA_HEADS, GLA_DK)
    w_qk = jnp.concatenate([wq_g, wk_g], axis=-1).reshape(D_MODEL, GLA_HEADS * LANES)
    w_main = jnp.concatenate([w_qk, w_in[:, o[2]:o[3]], w_in[:, o[4]:o[5]], w_in[:, o[5]:o[6]],
                              w_in[:, o[6]:o[7]], w_in[:, o[7]:o[8]]], axis=-1).astype(BF16)
    w_glr = jnp.pad(w_in[:, o[3]:o[4]], ((0, 0), (0, LANES - 2 * GLA_RANK))).astype(BF16)
    wg = w_gate.reshape(2, GLA_RANK, GLA_HEADS, GLA_DK)
    wg = jnp.concatenate([wg, wg], axis=-1).reshape(2, GLA_RANK, GLA_HEADS * LANES)
    wg = jnp.stack([jnp.pad(wg[0], ((0, LANES - GLA_RANK), (0, 0))),
                    jnp.pad(wg[1], ((GLA_RANK, LANES - 2 * GLA_RANK), (0, 0)))]).astype(BF16)
    bg = b_gate.reshape(2, GLA_HEADS, GLA_DK)
    bg = jnp.concatenate([bg, bg], axis=-1).reshape(2, 1, GLA_HEADS * LANES)
    return dict(w_main=w_main, w_glr=w_glr, wg=wg, bg=bg, w_out=w_out.astype(BF16),
                wq=peer_wq.astype(BF16), keys=peer_keys.astype(BF16))


def _state_to_kernel(s):
    st = jnp.swapaxes(s, -1, -2)
    return jnp.pad(st, [(0, 0)] * (st.ndim - 1) + [(0, LANES - GLA_DK)])


def _state_from_kernel(st):
    return jnp.swapaxes(st[..., :GLA_DK], -1, -2)


def kernel(x_prompt, x_sample, cache_k, cache_v, state_gla, c, c_ctx, w_mod, b_mod, norm_mix, norm_ffn,
           w_in, w_gate, b_gate, gla_norm, q_norm, k_norm, w_out, peer_wq, peer_keys, peer_u, peer_v):
    bc, tc, _ = x_prompt.shape
    bl, tl, _ = x_sample.shape
    n_ctx, n_lat = bc * tc, bl * tl
    assert bl + 1 <= MOD_ROWS

    cvec = jnp.concatenate([c_ctx[None, :], c, jnp.zeros((MOD_ROWS - 1 - bl, D_MODEL), F32)], axis=0)
    mod_all = _modulation(cvec, w_mod, b_mod)
    rope_tabs = _rope_tables(tl)
    x = jnp.concatenate([x_prompt.reshape(n_ctx, D_MODEL), x_sample.reshape(n_lat, D_MODEL)], axis=0)
    zero_state = jnp.zeros((bc, 2, GLA_HEADS, LANES, LANES), F32)

    ks, vs, ss = [], [], []
    for l in range(DEPTH):
        w = _layer_weights(w_in[l], w_gate[l], b_gate[l], w_out[l], peer_wq[l], peer_keys[l])
        uv = _pack_uv(peer_u, peer_v, l)
        mod = mod_all[l]
        proj, glr = _proj_in(x, mod, norm_mix[l][None, :], w["w_main"], w["w_glr"], _mod_row(512, n_ctx, tl))

        odir, st_c = _gla(proj, glr, w["wg"], w["bg"], zero_state, None, row0=0, batch=bc, seq=tc, tile=tc)
        odir, _ = _gla(proj, glr, w["wg"], w["bg"], _state_to_kernel(state_gla[:, l]), odir,
                       row0=n_ctx, batch=bl, seq=tl, tile=512)
        ss.append(_state_from_kernel(st_c))

        qw, kw = q_norm[l][None, :], k_norm[l][None, :]
        qn_c, kn_c, vb_c, ck, cv = _prep(proj, qw, kw, None, row0=0, batch=bc, seq=tc, cache=True)
        qn_l, kn_l, vb_l = _prep(proj, qw, kw, rope_tabs, row0=n_ctx, batch=bl, seq=tl, cache=False)
        ks.append(ck)
        vs.append(cv)
        n_all = n_ctx + n_lat
        oa = _attention(qn_c, kn_c, vb_c, None, None, row0=0, n_all=n_all, batch=bc, seq=tc, tq=tc)
        oa = _attention(qn_l, kn_l, vb_l, (cache_k, cache_v, l), oa, row0=n_ctx, n_all=n_all,
                        batch=bl, seq=tl, tq=128)

        x1, h2 = _out_proj(x, odir, proj, oa, gla_norm[l][None, :], w["w_out"], mod, norm_ffn[l][None, :],
                           _mod_row(256, n_ctx, tl))
        idx, gates = _peer_topk(h2, w["wq"], w["keys"])
        x = _peer_mix(idx, gates, h2, x1, mod, uv, _mod_row(PEER_TB, n_ctx, tl))

    y_prompt = x[:n_ctx].reshape(bc, tc, D_MODEL)
    y_sample = x[n_ctx:].reshape(bl, tl, D_MODEL)
    return (y_prompt, y_sample, jnp.stack(ks, axis=1), jnp.stack(vs, axis=1), jnp.stack(ss, axis=1))
```

```python
import functools

import jax
import jax.numpy as jnp
import numpy as np
from jax import lax
from jax.experimental import pallas as pl
from jax.experimental.pallas import tpu as pltpu

F32 = jnp.float32
BF16 = jnp.bfloat16

D_MODEL = 2048
DEPTH = 2
GRID_W = 64
GLA_HEADS = 8
GLA_DK = 64
GLA_DV = 128
GLA_RANK = 16
GLA_TAU = 16.0
GLA_CHUNK = 64
ATT_HEADS = 8
ATT_KV_HEADS = 2
ATT_GROUP = ATT_HEADS // ATT_KV_HEADS
HEAD_DIM = 128
ROPE_THETA = 10000.0
PEER_HEADS = 8
PEER_QDIM = 256
N_KEYS = 128
PEER_TOPK = 16
PEER_SEL = PEER_HEADS * PEER_TOPK
EPS = 1e-6
LOG2_E = 1.4426950408889634

LANES = 128
MOD_ROWS = 8
VMEM_LIMIT = 56 * 1024 * 1024

COL_QK = 0
COL_V = COL_QK + GLA_HEADS * LANES
COL_OG = COL_V + GLA_HEADS * GLA_DV
COL_QA = COL_OG + GLA_HEADS * GLA_DV
COL_KV = COL_QA + ATT_HEADS * HEAD_DIM
PROJ_COLS = COL_KV + 2 * ATT_KV_HEADS * HEAD_DIM

NT_DIMS = (((1,), (1,)), ((), ()))
TN_DIMS = (((0,), (0,)), ((), ()))


def _params(*sem):
    return pltpu.CompilerParams(dimension_semantics=sem, vmem_limit_bytes=VMEM_LIMIT)


def _dot(a, b):
    return jnp.dot(a, b, preferred_element_type=F32)


def _dot_exact01(a01, x):
    hi = x.astype(BF16)
    r1 = x - hi.astype(F32)
    mid = r1.astype(BF16)
    lo = (r1 - mid.astype(F32)).astype(BF16)
    return _dot(a01, hi) + _dot(a01, mid) + _dot(a01, lo)


def _rms(x):
    return x * lax.rsqrt(jnp.mean(x * x, axis=-1, keepdims=True) + EPS)


def _mod_kernel(cv_ref, w_ref, b_ref, o_ref):
    a = jax.nn.silu(cv_ref[...]).astype(BF16)
    o_ref[0] = _dot(a, w_ref[0].astype(BF16)) + b_ref[0]


def _modulation(cvec, w_mod, b_mod):
    tn = 1536
    n_out = 6 * D_MODEL
    out = pl.pallas_call(
        _mod_kernel,
        grid=(DEPTH, n_out // tn),
        in_specs=[pl.BlockSpec((MOD_ROWS, D_MODEL), lambda l, j: (0, 0)),
                  pl.BlockSpec((1, D_MODEL, tn), lambda l, j: (l, 0, j)),
                  pl.BlockSpec((1, 1, tn), lambda l, j: (l, 0, j))],
        out_specs=pl.BlockSpec((1, MOD_ROWS, tn), lambda l, j: (l, 0, j)),
        out_shape=jax.ShapeDtypeStruct((DEPTH, MOD_ROWS, n_out), F32),
        compiler_params=_params("parallel", "parallel"),
        name="modulation",
    )(cvec, w_mod, b_mod.reshape(DEPTH, 1, n_out))
    return out.reshape(DEPTH, MOD_ROWS, 6, D_MODEL)


def _mod_row(tile_rows, n_ctx, t_lat):
    def row(i):
        start = i * tile_rows
        return jnp.where(start < n_ctx, 0, 1 + (start - n_ctx) // t_lat)
    return row


def _proj_in_kernel(x_ref, mod_ref, nw_ref, w_ref, wg_ref, o_ref, glr_ref, h_scr):
    @pl.when(pl.program_id(1) == 0)
    def _():
        y = _rms(x_ref[...]) * nw_ref[...]
        h = (y * (1.0 + mod_ref[0, 1:2, :]) + mod_ref[0, 0:1, :]).astype(BF16)
        h_scr[...] = h
        glr_ref[...] = _dot(h, wg_ref[...])

    o_ref[...] = _dot(h_scr[...], w_ref[...])


def _proj_in(x, mod, norm_w, w_main, w_glr, row):
    n = x.shape[0]
    tm, tn = 512, 1536
    return pl.pallas_call(
        _proj_in_kernel,
        grid=(n // tm, PROJ_COLS // tn),
        in_specs=[pl.BlockSpec((tm, D_MODEL), lambda i, j: (i, 0)),
                  pl.BlockSpec((1, 6, D_MODEL), lambda i, j: (row(i), 0, 0)),
                  pl.BlockSpec((1, D_MODEL), lambda i, j: (0, 0)),
                  pl.BlockSpec((D_MODEL, tn), lambda i, j: (0, j)),
                  pl.BlockSpec((D_MODEL, LANES), lambda i, j: (0, 0))],
        out_specs=[pl.BlockSpec((tm, tn), lambda i, j: (i, j)),
                   pl.BlockSpec((tm, LANES), lambda i, j: (i, 0))],
        out_shape=[jax.ShapeDtypeStruct((n, PROJ_COLS), F32),
                   jax.ShapeDtypeStruct((n, LANES), F32)],
        scratch_shapes=[pltpu.VMEM((tm, D_MODEL), BF16)],
        compiler_params=_params("parallel", "arbitrary"),
        name="proj_in",
    )(x, mod, norm_w, w_main, w_glr)


GLA_HB = 4


def _gla_kernel(qk_ref, v_ref, glr_ref, wg_ref, bg_ref, swap_ref, mask_ref, init_ref, out_hbm_ref, o_ref, st_ref,
                st_scr, x1_scr, k1_scr, x2_scr, k2_scr, vb_scr, dec_scr, upd_scr, sp_scr, *, n_chunks):
    del out_hbm_ref
    d = pl.program_id(2)
    t = pl.program_id(3)
    fwd = d == 0

    @pl.when(t == 0)
    def _():
        st_scr[...] = init_ref[0, 0]

    c = GLA_CHUNK
    hw = GLA_HB * LANES
    mask = mask_ref[0]
    qmask = lax.broadcasted_iota(jnp.int32, (1, hw), 1) % LANES < GLA_DK

    z = _dot(glr_ref[...].astype(BF16), wg_ref[0]) + bg_ref[0]
    la = jax.nn.log_sigmoid(z) / GLA_TAU
    blk = 2 * c
    m01 = mask[0:blk, 0:blk].astype(BF16)
    b_all = jnp.concatenate([_dot_exact01(m01, la[r:r + blk, :]) for r in range(0, n_chunks * c, blk)], axis=0)
    vb_scr[...] = v_ref[...].astype(BF16)
    for k in range(n_chunks):
        rows = slice(k * c, (k + 1) * c)
        b = b_all[rows, :]
        b_mid = jnp.where(fwd, b[c // 2 - 1:c // 2, :], b[c // 2:c // 2 + 1, :])
        b_end = jnp.where(fwd, b[c - 1:c, :], b[0:1, :])
        qk = qk_ref[rows, :]
        x1_scr[rows, :] = (qk * jnp.exp(jnp.where(qmask, b - b_mid, b_mid - b))).astype(BF16)
        x2_scr[rows, :] = (qk * jnp.exp(jnp.where(qmask, b, b_end - b))).astype(BF16)
        dec_scr[k] = jnp.exp(b_end)
    swap = swap_ref[...]
    for h in range(GLA_HB):
        hl = slice(h * LANES, (h + 1) * LANES)
        k1_scr[:, hl] = _dot(x1_scr[:, hl], swap).astype(BF16)
        k2_scr[:, hl] = _dot(x2_scr[:, hl], swap).astype(BF16)

    for h in range(GLA_HB):
        hl = slice(h * LANES, (h + 1) * LANES)
        s = lax.dot_general(x1_scr[:, hl], k1_scr[:, hl], NT_DIMS, preferred_element_type=F32)
        s = jnp.where(mask > 0.0, s, 0.0).astype(BF16)
        o_ref[0, :, hl] = _dot(s, vb_scr[:, hl])

    for k in range(n_chunks):
        rows = slice(k * c, (k + 1) * c)
        for h in range(GLA_HB):
            hl = slice(h * LANES, (h + 1) * LANES)
            upd_scr[k, h] = lax.dot_general(vb_scr[rows, hl], k2_scr[rows, hl], TN_DIMS,
                                            preferred_element_type=F32)

    def chunk(step, carry):
        k = jnp.where(fwd, step, n_chunks - 1 - step)
        dec = dec_scr[k]
        for h in range(GLA_HB):
            st = st_scr[h]
            sp_scr[k, h] = st.astype(BF16)
            st_scr[h] = st * dec[:, h * LANES:(h + 1) * LANES] + upd_scr[k, h]
        return carry

    lax.fori_loop(0, n_chunks, chunk, 0)

    for k in range(n_chunks):
        rows = slice(k * c, (k + 1) * c)
        for h in range(GLA_HB):
            hl = slice(h * LANES, (h + 1) * LANES)
            o_ref[0, rows, hl] += lax.dot_general(x2_scr[rows, hl], sp_scr[k, h], NT_DIMS,
                                                  preferred_element_type=F32)

    @pl.when(t == pl.num_programs(3) - 1)
    def _():
        st_ref[0, 0] = st_scr[...]


def _gla(proj, glr, wg, bg, init_t, out, *, row0, batch, seq, tile):
    nt = seq // tile
    rb0 = row0 // tile
    hw = GLA_HB * LANES
    n_chunks = tile // GLA_CHUNK
    l = np.arange(LANES)
    swap = ((l[:, None] == l[None, :] + GLA_DK) & (l[None, :] < GLA_DK)).astype(np.float32)
    r = np.arange(tile)
    same = r[:, None] // GLA_CHUNK == r[None, :] // GLA_CHUNK
    mask = np.stack([same & (r[:, None] >= r[None, :]), same & (r[:, None] <= r[None, :])]).astype(np.float32)

    def rblk(b, t, d):
        return rb0 + b * nt + jnp.where(d == 0, t, nt - 1 - t)

    in_specs = [pl.BlockSpec((tile, hw), lambda b, g, d, t: (rblk(b, t, d), COL_QK // hw + g)),
                pl.BlockSpec((tile, hw), lambda b, g, d, t: (rblk(b, t, d), COL_V // hw + g)),
                pl.BlockSpec((tile, LANES), lambda b, g, d, t: (rblk(b, t, d), 0)),
                pl.BlockSpec((1, LANES, hw), lambda b, g, d, t: (d, 0, g)),
                pl.BlockSpec((1, 1, hw), lambda b, g, d, t: (d, 0, g)),
                pl.BlockSpec((LANES, LANES), lambda b, g, d, t: (0, 0)),
                pl.BlockSpec((1, tile, tile), lambda b, g, d, t: (d, 0, 0)),
                pl.BlockSpec((1, 1, GLA_HB, LANES, LANES), lambda b, g, d, t: (b, d, g, 0, 0)),
                pl.BlockSpec(memory_space=pl.ANY)]
    args = [proj, proj, glr, wg, bg, jnp.asarray(swap, BF16), jnp.asarray(mask), init_t, out]
    return pl.pallas_call(
        functools.partial(_gla_kernel, n_chunks=n_chunks),
        grid=(batch, GLA_HEADS // GLA_HB, 2, nt),
        in_specs=in_specs,
        out_specs=[pl.BlockSpec((1, tile, hw), lambda b, g, d, t: (d, rblk(b, t, d), g)),
                   pl.BlockSpec((1, 1, GLA_HB, LANES, LANES), lambda b, g, d, t: (b, d, g, 0, 0))],
        out_shape=[jax.ShapeDtypeStruct(out.shape, out.dtype),
                   jax.ShapeDtypeStruct((batch, 2, GLA_HEADS, LANES, LANES), F32)],
        scratch_shapes=[pltpu.VMEM((GLA_HB, LANES, LANES), F32)]
                       + [pltpu.VMEM((tile, hw), BF16)] * 5
                       + [pltpu.VMEM((n_chunks, 1, hw), F32),
                          pltpu.VMEM((n_chunks, GLA_HB, LANES, LANES), F32),
                          pltpu.VMEM((n_chunks, GLA_HB, LANES, LANES), BF16)],
        input_output_aliases={len(args) - 1: 0},
        compiler_params=_params("parallel", "parallel", "arbitrary", "arbitrary"),
        name="gla",
    )(*args)


def _prep_kernel(*refs, rope, cache):
    qa_ref, kv_ref, qw_ref, kw_ref = refs[:4]
    refs = refs[4:]
    if rope:
        cos_ref, sa_ref, sb_ref = refs[:3]
        refs = refs[3:]
    qn_ref, kn_ref, vb_ref = refs[:3]
    if cache:
        ck_ref, cv_ref = refs[3:5]

    def rot(y):
        if not rope:
            return y
        return (y * cos_ref[...] + pltpu.roll(y, LANES - 32, 1) * sa_ref[...]
                + pltpu.roll(y, 32, 1) * sb_ref[...])

    for h in range(ATT_HEADS):
        hl = slice(h * HEAD_DIM, (h + 1) * HEAD_DIM)
        qn_ref[:, hl] = rot(_rms(qa_ref[:, hl]) * qw_ref[...]).astype(BF16)
    for h in range(ATT_KV_HEADS):
        hl = slice(h * HEAD_DIM, (h + 1) * HEAD_DIM)
        vl = slice((ATT_KV_HEADS + h) * HEAD_DIM, (ATT_KV_HEADS + h + 1) * HEAD_DIM)
        kn = _rms(kv_ref[:, hl]) * kw_ref[...]
        v = kv_ref[:, vl]
        kn_ref[:, hl] = rot(kn).astype(BF16)
        vb_ref[:, hl] = v.astype(BF16)
        if cache:
            ck_ref[0, h] = kn
            cv_ref[0, h] = v


def _prep(proj, qw, kw, rope_tabs, *, row0, batch, seq, cache):
    tm = 256
    n = batch * seq
    rb0 = row0 // tm
    per_req = seq // tm
    kvw = ATT_KV_HEADS * HEAD_DIM
    rope = rope_tabs is not None
    in_specs = [pl.BlockSpec((tm, ATT_HEADS * HEAD_DIM), lambda i: (rb0 + i, COL_QA // (ATT_HEADS * HEAD_DIM))),
                pl.BlockSpec((tm, 2 * kvw), lambda i: (rb0 + i, COL_KV // (2 * kvw))),
                pl.BlockSpec((1, HEAD_DIM), lambda i: (0, 0)),
                pl.BlockSpec((1, HEAD_DIM), lambda i: (0, 0))]
    args = [proj, proj, qw, kw]
    if rope:
        in_specs += [pl.BlockSpec((tm, HEAD_DIM), lambda i: (i % per_req, 0))] * 3
        args += list(rope_tabs)
    out_specs = [pl.BlockSpec((tm, ATT_HEADS * HEAD_DIM), lambda i: (i, 0)),
                 pl.BlockSpec((tm, kvw), lambda i: (i, 0)),
                 pl.BlockSpec((tm, kvw), lambda i: (i, 0))]
    out_shape = [jax.ShapeDtypeStruct((n, ATT_HEADS * HEAD_DIM), BF16),
                 jax.ShapeDtypeStruct((n, kvw), BF16),
                 jax.ShapeDtypeStruct((n, kvw), BF16)]
    if cache:
        assert seq == tm
        out_specs += [pl.BlockSpec((1, ATT_KV_HEADS, seq, HEAD_DIM), lambda i: (i, 0, 0, 0))] * 2
        out_shape += [jax.ShapeDtypeStruct((batch, ATT_KV_HEADS, seq, HEAD_DIM), F32)] * 2
    return pl.pallas_call(
        functools.partial(_prep_kernel, rope=rope, cache=cache),
        grid=(n // tm,),
        in_specs=in_specs, out_specs=out_specs, out_shape=out_shape,
        compiler_params=_params("parallel"),
        name="attn_prep",
    )(*args)


def _rope_tables(seq):
    half = HEAD_DIM // 2
    tok = jnp.arange(seq)
    row = (tok // GRID_W).astype(F32)
    col = (tok % GRID_W).astype(F32)
    inv_freq = ROPE_THETA ** (-jnp.arange(0, half, 2, dtype=F32) / half)
    ang_r = row[:, None] * inv_freq[None, :]
    ang_c = col[:, None] * inv_freq[None, :]
    ang = jnp.concatenate([ang_r, ang_r, ang_c, ang_c], axis=-1)
    first = (jnp.arange(HEAD_DIM) % half) < half // 2
    sin = jnp.sin(ang)
    return jnp.cos(ang), jnp.where(first, -sin, 0.0), jnp.where(first, 0.0, sin)


ATT_COL_BLOCK = 512

def _attn_kernel(*refs, ctx):
    o_ref, s_scr = refs[-2], refs[-1]
    refs = refs[:-3]
    if ctx:
        q_ref, k_ref, v_ref, ck_ref, cv_ref = refs
        ck = ck_ref[0, 0, 0].astype(BF16)
        cv = cv_ref[0, 0, 0].astype(BF16)
    else:
        q_ref, k_ref, v_ref = refs
    k = k_ref[...]
    v = v_ref[...]
    c2 = (HEAD_DIM ** -0.5) * LOG2_E

    n_keys = k_ref.shape[0]
    blocks = [(k, v, r, min(ATT_COL_BLOCK, n_keys - r), r) for r in range(0, n_keys, ATT_COL_BLOCK)]
    if ctx:
        past = ck.shape[0]
        blocks += [(ck, cv, r, min(ATT_COL_BLOCK, past - r), n_keys + r) for r in range(0, past, ATT_COL_BLOCK)]

    def scores(g):
        q = q_ref[:, g * HEAD_DIM:(g + 1) * HEAD_DIM]
        for kk, _, r, w, col in blocks:
            s_scr[g % 2, :, col:col + w] = lax.dot_general(q, kk[r:r + w, :], NT_DIMS, preferred_element_type=F32)

    def lane_fold(x, op):
        parts = [x[:, i:i + LANES] for i in range(0, x.shape[1], LANES)]
        while len(parts) > 1:
            parts = [op(parts[i], parts[i + 1]) for i in range(0, len(parts) - 1, 2)] + parts[len(parts) & ~1:]
        return parts[0]

    def finish(g):
        sg = s_scr.at[g % 2]
        mp = None
        for _, _, _, w, col in blocks:
            part = lane_fold(sg[:, col:col + w], jnp.maximum)
            mp = part if mp is None else jnp.maximum(mp, part)
        mc = jnp.max(mp, axis=-1, keepdims=True) * c2
        lp = jnp.zeros((q_ref.shape[0], LANES), F32)
        acc = jnp.zeros((q_ref.shape[0], HEAD_DIM), F32)
        for _, vv, r, w, col in blocks:
            p = jnp.exp2(sg[:, col:col + w] * c2 - mc)
            lp = lp + lane_fold(p, jnp.add)
            acc = acc + _dot(p.astype(BF16), vv[r:r + w, :])
        l = jnp.sum(lp, axis=-1, keepdims=True)
        o_ref[:, g * HEAD_DIM:(g + 1) * HEAD_DIM] = (acc / l).astype(BF16)

    scores(0)
    for g in range(ATT_GROUP):
        if g + 1 < ATT_GROUP:
            scores(g + 1)
        finish(g)


def _attention(qn, kn, vb, ctx_kv, out, *, row0, batch, seq, tq):
    nq = seq // tq
    ob0 = row0 // tq
    past = ctx_kv[0].shape[3] if ctx_kv is not None else 0
    assert seq % LANES == 0 and past % LANES == 0
    gw = ATT_GROUP * HEAD_DIM
    in_specs = [pl.BlockSpec((tq, gw), lambda b, h, t: (b * nq + t, h)),
                pl.BlockSpec((seq, HEAD_DIM), lambda b, h, t: (b, h)),
                pl.BlockSpec((seq, HEAD_DIM), lambda b, h, t: (b, h))]
    args = [qn, kn, vb]
    if ctx_kv is not None:
        ck, cv, layer = ctx_kv
        spec = pl.BlockSpec((1, 1, 1, past, HEAD_DIM), lambda b, h, t: (b, layer, h, 0, 0))
        in_specs += [spec, spec]
        args += [ck, cv]
    in_specs.append(pl.BlockSpec(memory_space=pl.ANY))
    args.append(out)
    return pl.pallas_call(
        functools.partial(_attn_kernel, ctx=ctx_kv is not None),
        grid=(batch, ATT_KV_HEADS, nq),
        in_specs=in_specs,
        out_specs=pl.BlockSpec((tq, gw), lambda b, h, t: (ob0 + b * nq + t, h)),
        out_shape=jax.ShapeDtypeStruct(out.shape, out.dtype),
        scratch_shapes=[pltpu.VMEM((2, tq, seq + past), F32)],
        input_output_aliases={len(args) - 1: 0},
        compiler_params=_params("parallel", "parallel", "arbitrary"),
        name="attention",
    )(*args)


def _out_proj_kernel(x_ref, od_ref, og_ref, oa_ref, gn_ref, w_ref, mod_ref, nw_ref, x1_ref, h2_ref):
    o = od_ref[0] + od_ref[1]
    parts = []
    for h in range(GLA_HEADS):
        hl = slice(h * GLA_DV, (h + 1) * GLA_DV)
        parts.append((_rms(o[:, hl]) * gn_ref[...] * jax.nn.silu(og_ref[:, hl])).astype(BF16))
    og = jnp.concatenate(parts, axis=-1)
    gw = GLA_HEADS * GLA_DV
    mix = _dot(og, w_ref[0:gw, :]) + _dot(oa_ref[...], w_ref[gw:, :])
    x1 = x_ref[...] + mod_ref[0, 2:3, :] * mix
    x1_ref[...] = x1
    h2_ref[...] = _rms(x1) * nw_ref[...] * (1.0 + mod_ref[0, 4:5, :]) + mod_ref[0, 3:4, :]


def _out_proj(x, odir, proj, oa, gla_norm, w_out, mod, norm_w, row):
    n = x.shape[0]
    tm = 256
    gw = GLA_HEADS * GLA_DV
    return pl.pallas_call(
        _out_proj_kernel,
        grid=(n // tm,),
        in_specs=[pl.BlockSpec((tm, D_MODEL), lambda i: (i, 0)),
                  pl.BlockSpec((2, tm, gw), lambda i: (0, i, 0)),
                  pl.BlockSpec((tm, gw), lambda i: (i, COL_OG // gw)),
                  pl.BlockSpec((tm, ATT_HEADS * HEAD_DIM), lambda i: (i, 0)),
                  pl.BlockSpec((1, GLA_DV), lambda i: (0, 0)),
                  pl.BlockSpec((gw + ATT_HEADS * HEAD_DIM, D_MODEL), lambda i: (0, 0)),
                  pl.BlockSpec((1, 6, D_MODEL), lambda i: (row(i), 0, 0)),
                  pl.BlockSpec((1, D_MODEL), lambda i: (0, 0))],
        out_specs=[pl.BlockSpec((tm, D_MODEL), lambda i: (i, 0)),
                   pl.BlockSpec((tm, D_MODEL), lambda i: (i, 0))],
        out_shape=[jax.ShapeDtypeStruct((n, D_MODEL), F32),
                   jax.ShapeDtypeStruct((n, D_MODEL), F32)],
        compiler_params=_params("parallel"),
        name="out_proj",
    )(x, odir, proj, oa, gla_norm, w_out, mod, norm_w)


NEG_INF = float("-inf")
N_CAND = PEER_TOPK * PEER_TOPK


_CAND_GROUPS = ((0, 0), (0, 8), (1, 0), (2, 0), (3, 0), (4, 0), (5, 0), (6, 0), (7, 0))
N_CAND_ROWS = 8 * (len(_CAND_GROUPS) + 1)


def _cand_consts():
    pos = np.zeros((N_CAND_ROWS, 1), np.float32)
    off = np.zeros((N_CAND_ROWS, 1), np.float32)
    for g, (a, b0) in enumerate(_CAND_GROUPS):
        for j in range(8):
            pos[8 * g + j] = a * PEER_TOPK + b0 + j
            off[8 * g + j] = 0.0 if (a + 1) * (b0 + j + 1) <= PEER_TOPK else NEG_INF
    for j in range(8):
        pos[N_CAND_ROWS - 8 + j] = (8 + j) * PEER_TOPK
    return (jnp.asarray(np.broadcast_to(pos, (N_CAND_ROWS, LANES))),
            jnp.asarray(np.broadcast_to(off, (N_CAND_ROWS, LANES))))


def _pair_rows(first, second):
    rows = [first[a:a + 1, :] + second[b0:b0 + 8, :] for a, b0 in _CAND_GROUPS]
    rows.append(first[8:16, :] + second[0:1, :])
    return jnp.concatenate(rows, axis=0)


def _top16_rows(s, row_id):
    rank = lax.broadcasted_iota(jnp.int32, (PEER_TOPK, s.shape[1]), 0)
    vals = jnp.zeros((PEER_TOPK, s.shape[1]), F32)
    ids = jnp.zeros((PEER_TOPK, s.shape[1]), F32)
    for r in range(PEER_TOPK):
        m = jnp.max(s, axis=0, keepdims=True)
        i = jnp.min(jnp.where(s == m, row_id, float(N_CAND)), axis=0, keepdims=True)
        s = jnp.where(row_id == i, NEG_INF, s)
        vals = jnp.where(rank == r, m, vals)
        ids = jnp.where(rank == r, i, ids)
    return vals, ids


U32 = jnp.uint32


def _pack_uv_kernel(u_ref, v_ref, o_ref):
    ub = lax.bitcast_convert_type(u_ref[0].astype(BF16).astype(F32), U32)
    vb = lax.bitcast_convert_type(v_ref[0].astype(BF16).astype(F32), U32)
    o_ref[:, 0, :] = ub | (vb >> 16)


def _pack_uv(u, v, layer):
    n, tr = u.shape[1], 512
    spec = pl.BlockSpec((1, tr, D_MODEL), lambda i: (layer, i, 0))
    return pl.pallas_call(
        _pack_uv_kernel, grid=(n // tr,), in_specs=[spec, spec],
        out_specs=pl.BlockSpec((tr, 1, D_MODEL), lambda i: (i, 0, 0)),
        out_shape=jax.ShapeDtypeStruct((n, 1, D_MODEL), U32),
        compiler_params=_params("parallel"), name="pack_uv",
    )(u, v)


PEER_TB = 256
PEER_NBUF = 8
PEER_GROUPS_PER_HEAD = 4


def _peer_kernel(hn_ref, hc_ref, x_ref, mod_ref, wq_ref, keys_ref, pos_ref, off_ref, uv_ref, o_ref,
                 q_scr, idx_v, gate_v, idx_s, *rest):
    bufs, sem, csem = rest[:PEER_NBUF], rest[PEER_NBUF], rest[PEER_NBUF + 1]
    i = pl.program_id(0)
    tb = hc_ref.shape[0]
    n_groups = tb // PEER_NBUF
    new = i % 2
    cur = 1 - new
    selecting = i < pl.num_programs(0) - 1
    mixing = i > 0

    @pl.when(selecting)
    def _():
        q = _dot(hn_ref[...].astype(BF16), wq_ref[...]).astype(BF16)
        for a in range(2 * PEER_HEADS):
            q_scr[a] = q[:, a * LANES:(a + 1) * LANES]

    def select_head(h):
        key_id = lax.broadcasted_iota(jnp.int32, (N_KEYS, LANES), 0).astype(F32)
        rank = lax.broadcasted_iota(jnp.int32, (PEER_TOPK, LANES), 0)
        pos = pos_ref[...]
        off = off_ref[...]
        out_rows = pl.ds(pl.multiple_of(h * PEER_TOPK, PEER_TOPK), PEER_TOPK)
        for c in range(tb // LANES):
            cols = slice(c * LANES, (c + 1) * LANES)
            s1 = lax.dot_general(keys_ref[h, 0], q_scr[2 * h, cols, :], NT_DIMS, preferred_element_type=F32)
            s2 = lax.dot_general(keys_ref[h, 1], q_scr[2 * h + 1, cols, :], NT_DIMS, preferred_element_type=F32)
            v1, i1 = _top16_rows(s1, key_id)
            v2, i2 = _top16_rows(s2, key_id)
            cand = _pair_rows(v1, v2) + off
            cidx = _pair_rows(i1 * float(N_KEYS), i2)
            best = jnp.zeros((PEER_TOPK, LANES), F32)
            eidx = jnp.zeros((PEER_TOPK, LANES), F32)
            for r in range(PEER_TOPK):
                m = jnp.max(cand, axis=0, keepdims=True)
                p = jnp.min(jnp.where(cand == m, pos, float(N_CAND)), axis=0, keepdims=True)
                sel = pos == p
                e = jnp.max(jnp.where(sel, cidx, -1.0), axis=0, keepdims=True)
                cand = jnp.where(sel, NEG_INF, cand)
                best = jnp.where(rank == r, m, best)
                eidx = jnp.where(rank == r, e, eidx)
            ex = jnp.exp(best - best[0:1, :])
            gate_v[new, out_rows, cols] = ex / jnp.sum(ex, axis=0, keepdims=True)
            idx_v[new, out_rows, cols] = eidx.astype(jnp.int32)

    def issue(t, slot):
        for k in range(PEER_SEL):
            pltpu.make_async_copy(uv_ref.at[idx_s[cur, k, t]], bufs[slot].at[pl.ds(k, 1), :],
                                  sem.at[slot]).start(priority=k % 2)

    def wait(slot):
        pltpu.make_async_copy(bufs[slot], bufs[slot], sem.at[slot]).wait()

    tok = lax.broadcasted_iota(jnp.int32, (1, tb), 1)

    def token(t, slot, prefetch):
        wait(slot)
        if prefetch:
            issue(t + PEER_NBUF - 1, (slot + PEER_NBUF - 1) % PEER_NBUF)
        x = hc_ref[pl.ds(t, 1), :]
        w = bufs[slot][...]
        u = lax.bitcast_convert_type(w & jnp.uint32(0xFFFF0000), F32)
        s = jnp.sum(u * x, axis=-1, keepdims=True)
        g = jnp.sum(jnp.where(tok == t, gate_v[cur], 0.0), axis=-1, keepdims=True)
        act = jax.nn.gelu(s) * g
        v = lax.bitcast_convert_type(w << 16, F32)
        o = jnp.sum(act * v, axis=0, keepdims=True)
        o_ref[pl.ds(t, 1), :] = x_ref[pl.ds(t, 1), :] + mod_ref[0, 5:6, :] * o

    @pl.when(mixing)
    def _():
        for t0 in range(PEER_NBUF - 1):
            issue(t0, t0)

    def group(g, carry):
        @pl.when(mixing)
        def _():
            for j in range(PEER_NBUF):
                token(g * PEER_NBUF + j, j, True)

        @pl.when(jnp.logical_and(selecting, g % PEER_GROUPS_PER_HEAD == 0))
        def _():
            select_head(g // PEER_GROUPS_PER_HEAD)

        return carry

    lax.fori_loop(0, n_groups - 1, group, 0)

    @pl.when(mixing)
    def _():
        for j in range(PEER_NBUF):
            token((n_groups - 1) * PEER_NBUF + j, j, j == 0)

    @pl.when(selecting)
    def _():
        publish = pltpu.make_async_copy(idx_v.at[new], idx_s.at[new], csem.at[0])
        publish.start()
        publish.wait()


def _peer(h2, x1, mod, wq, keys, uv, row):
    n = h2.shape[0]
    tb = PEER_TB
    nb = n // tb
    assert (PEER_HEADS - 1) * PEER_GROUPS_PER_HEAD <= tb // PEER_NBUF - 2
    qw = PEER_HEADS * PEER_QDIM
    prev = lambda i: jnp.maximum(i - 1, 0)
    nxt = lambda i: jnp.minimum(i, nb - 1)
    full = lambda shape: pl.BlockSpec(shape, lambda i: (0,) * len(shape))
    return pl.pallas_call(
        _peer_kernel,
        grid=(nb + 1,),
        in_specs=[pl.BlockSpec((tb, D_MODEL), lambda i: (nxt(i), 0)),
                  pl.BlockSpec((tb, D_MODEL), lambda i: (prev(i), 0)),
                  pl.BlockSpec((tb, D_MODEL), lambda i: (prev(i), 0)),
                  pl.BlockSpec((1, 6, D_MODEL), lambda i: (row(prev(i)), 0, 0)),
                  full((D_MODEL, qw)),
                  full((PEER_HEADS, 2, N_KEYS, PEER_QDIM // 2)),
                  full((N_CAND_ROWS, LANES)), full((N_CAND_ROWS, LANES)),
                  pl.BlockSpec(memory_space=pl.ANY)],
        out_specs=pl.BlockSpec((tb, D_MODEL), lambda i: (prev(i), 0)),
        out_shape=jax.ShapeDtypeStruct((n, D_MODEL), F32),
        scratch_shapes=[pltpu.VMEM((2 * PEER_HEADS, tb, LANES), BF16),
                        pltpu.VMEM((2, PEER_SEL, tb), jnp.int32),
                        pltpu.VMEM((2, PEER_SEL, tb), F32),
                        pltpu.SMEM((2, PEER_SEL, tb), jnp.int32)]
                       + [pltpu.VMEM((PEER_SEL, D_MODEL), U32)] * PEER_NBUF
                       + [pltpu.SemaphoreType.DMA((PEER_NBUF,)), pltpu.SemaphoreType.DMA((1,))],
        compiler_params=_params("arbitrary"),
        name="peer",
    )(h2, h2, x1, mod, wq, keys, *_cand_consts(), uv)


def _layer_weights(w_in, w_gate, b_gate, w_out, peer_wq, peer_keys):
    o = np.cumsum((0, GLA_HEADS * GLA_DK, GLA_HEADS * GLA_DK, GLA_HEADS * GLA_DV, 2 * GLA_RANK,
                   GLA_HEADS * GLA_DV, ATT_HEADS * HEAD_DIM, ATT_KV_HEADS * HEAD_DIM, ATT_KV_HEADS * HEAD_DIM))
    wq_g = w_in[:, o[0]:o[1]].reshape(D_MODEL, GLA_HEADS, GLA_DK) * (GLA_DK ** -0.5)
    wk_g = w_in[:, o[1]:o[2]].reshape(D_MODEL, GLA_HEADS, GLA_DK)
    w_qk = jnp.concatenate([wq_g, wk_g], axis=-1).reshape(D_MODEL, GLA_HEADS * LANES)
    w_main = jnp.concatenate([w_qk, w_in[:, o[2]:o[3]], w_in[:, o[4]:o[5]], w_in[:, o[5]:o[6]],
                              w_in[:, o[6]:o[7]], w_in[:, o[7]:o[8]]], axis=-1).astype(BF16)
    w_glr = jnp.pad(w_in[:, o[3]:o[4]], ((0, 0), (0, LANES - 2 * GLA_RANK))).astype(BF16)
    wg = w_gate.reshape(2, GLA_RANK, GLA_HEADS, GLA_DK)
    wg = jnp.concatenate([wg, wg], axis=-1).reshape(2, GLA_RANK, GLA_HEADS * LANES)
    wg = jnp.stack([jnp.pad(wg[0], ((0, LANES - GLA_RANK), (0, 0))),
                    jnp.pad(wg[1], ((GLA_RANK, LANES - 2 * GLA_RANK), (0, 0)))]).astype(BF16)
    bg = b_gate.reshape(2, GLA_HEADS, GLA_DK)
    bg = jnp.concatenate([bg, bg], axis=-1).reshape(2, 1, GLA_HEADS * LANES)
    return dict(w_main=w_main, w_glr=w_glr, wg=wg, bg=bg, w_out=w_out.astype(BF16),
                wq=peer_wq.astype(BF16), keys=peer_keys.astype(BF16))


def _state_to_kernel(s):
    st = jnp.swapaxes(s, -1, -2)
    return jnp.pad(st, [(0, 0)] * (st.ndim - 1) + [(0, LANES - GLA_DK)])


def _state_from_kernel(st):
    return jnp.swapaxes(st[..., :GLA_DK], -1, -2)


def kernel(x_prompt, x_sample, cache_k, cache_v, state_gla, c, c_ctx, w_mod, b_mod, norm_mix, norm_ffn,
           w_in, w_gate, b_gate, gla_norm, q_norm, k_norm, w_out, peer_wq, peer_keys, peer_u, peer_v):
    bc, tc, _ = x_prompt.shape
    bl, tl, _ = x_sample.shape
    n_ctx, n_lat = bc * tc, bl * tl
    assert bl + 1 <= MOD_ROWS

    cvec = jnp.concatenate([c_ctx[None, :], c, jnp.zeros((MOD_ROWS - 1 - bl, D_MODEL), F32)], axis=0)
    mod_all = _modulation(cvec, w_mod, b_mod)
    rope_tabs = _rope_tables(tl)
    x = jnp.concatenate([x_prompt.reshape(n_ctx, D_MODEL), x_sample.reshape(n_lat, D_MODEL)], axis=0)
    zero_state = jnp.zeros((bc, 2, GLA_HEADS, LANES, LANES), F32)

    ks, vs, ss = [], [], []
    for l in range(DEPTH):
        w = _layer_weights(w_in[l], w_gate[l], b_gate[l], w_out[l], peer_wq[l], peer_keys[l])
        uv = _pack_uv(peer_u, peer_v, l)
        mod = mod_all[l]
        proj, glr = _proj_in(x, mod, norm_mix[l][None, :], w["w_main"], w["w_glr"], _mod_row(512, n_ctx, tl))

        odir = jnp.zeros((2, n_ctx + n_lat, GLA_HEADS * GLA_DV), F32)
        odir, st_c = _gla(proj, glr, w["wg"], w["bg"], zero_state, odir, row0=0, batch=bc, seq=tc, tile=tc)
        odir, _ = _gla(proj, glr, w["wg"], w["bg"], _state_to_kernel(state_gla[:, l]), odir,
                       row0=n_ctx, batch=bl, seq=tl, tile=512)
        ss.append(_state_from_kernel(st_c))

        qw, kw = q_norm[l][None, :], k_norm[l][None, :]
        qn_c, kn_c, vb_c, ck, cv = _prep(proj, qw, kw, None, row0=0, batch=bc, seq=tc, cache=True)
        qn_l, kn_l, vb_l = _prep(proj, qw, kw, rope_tabs, row0=n_ctx, batch=bl, seq=tl, cache=False)
        ks.append(ck)
        vs.append(cv)
        oa = jnp.zeros((n_ctx + n_lat, ATT_HEADS * HEAD_DIM), BF16)
        oa = _attention(qn_c, kn_c, vb_c, None, oa, row0=0, batch=bc, seq=tc, tq=tc)
        oa = _attention(qn_l, kn_l, vb_l, (cache_k, cache_v, l), oa, row0=n_ctx, batch=bl, seq=tl, tq=128)

        x1, h2 = _out_proj(x, odir, proj, oa, gla_norm[l][None, :], w["w_out"], mod, norm_ffn[l][None, :],
                           _mod_row(256, n_ctx, tl))
        x = _peer(h2, x1, mod, w["wq"], w["keys"], uv, _mod_row(PEER_TB, n_ctx, tl))

    y_prompt = x[:n_ctx].reshape(bc, tc, D_MODEL)
    y_sample = x[n_ctx:].reshape(bl, tl, D_MODEL)
    return (y_prompt, y_sample, jnp.stack(ks, axis=1), jnp.stack(vs, axis=1), jnp.stack(ss, axis=1))
```

```python
import functools

import jax
import jax.numpy as jnp
import numpy as np
from jax import lax
from jax.experimental import pallas as pl
from jax.experimental.pallas import tpu as pltpu

F32 = jnp.float32
BF16 = jnp.bfloat16

D_MODEL = 2048
DEPTH = 2
GRID_W = 64
GLA_HEADS = 8
GLA_DK = 64
GLA_DV = 128
GLA_RANK = 16
GLA_TAU = 16.0
GLA_CHUNK = 64
ATT_HEADS = 8
ATT_KV_HEADS = 2
ATT_GROUP = ATT_HEADS // ATT_KV_HEADS
HEAD_DIM = 128
ROPE_THETA = 10000.0
PEER_HEADS = 8
PEER_QDIM = 256
N_KEYS = 128
PEER_TOPK = 16
PEER_SEL = PEER_HEADS * PEER_TOPK
EPS = 1e-6
LOG2_E = 1.4426950408889634

LANES = 128
MOD_ROWS = 8
VMEM_LIMIT = 56 * 1024 * 1024

COL_QK = 0
COL_V = COL_QK + GLA_HEADS * LANES
COL_OG = COL_V + GLA_HEADS * GLA_DV
COL_QA = COL_OG + GLA_HEADS * GLA_DV
COL_KV = COL_QA + ATT_HEADS * HEAD_DIM
PROJ_COLS = COL_KV + 2 * ATT_KV_HEADS * HEAD_DIM

NT_DIMS = (((1,), (1,)), ((), ()))
TN_DIMS = (((0,), (0,)), ((), ()))


def _params(*sem):
    return pltpu.CompilerParams(dimension_semantics=sem, vmem_limit_bytes=VMEM_LIMIT)


def _dot(a, b):
    return jnp.dot(a, b, preferred_element_type=F32)


def _dot_exact01(a01, x):
    hi = x.astype(BF16)
    r1 = x - hi.astype(F32)
    mid = r1.astype(BF16)
    lo = (r1 - mid.astype(F32)).astype(BF16)
    return _dot(a01, hi) + _dot(a01, mid) + _dot(a01, lo)


def _rms(x):
    return x * lax.rsqrt(jnp.mean(x * x, axis=-1, keepdims=True) + EPS)


def _mod_kernel(cv_ref, w_ref, b_ref, o_ref):
    a = jax.nn.silu(cv_ref[...]).astype(BF16)
    o_ref[0] = _dot(a, w_ref[0].astype(BF16)) + b_ref[0]


def _modulation(cvec, w_mod, b_mod):
    tn = 1536
    n_out = 6 * D_MODEL
    out = pl.pallas_call(
        _mod_kernel,
        grid=(DEPTH, n_out // tn),
        in_specs=[pl.BlockSpec((MOD_ROWS, D_MODEL), lambda l, j: (0, 0)),
                  pl.BlockSpec((1, D_MODEL, tn), lambda l, j: (l, 0, j)),
                  pl.BlockSpec((1, 1, tn), lambda l, j: (l, 0, j))],
        out_specs=pl.BlockSpec((1, MOD_ROWS, tn), lambda l, j: (l, 0, j)),
        out_shape=jax.ShapeDtypeStruct((DEPTH, MOD_ROWS, n_out), F32),
        compiler_params=_params("parallel", "parallel"),
        name="modulation",
    )(cvec, w_mod, b_mod.reshape(DEPTH, 1, n_out))
    return out.reshape(DEPTH, MOD_ROWS, 6, D_MODEL)


def _mod_row(tile_rows, n_ctx, t_lat):
    def row(i):
        start = i * tile_rows
        return jnp.where(start < n_ctx, 0, 1 + (start - n_ctx) // t_lat)
    return row


def _proj_in_kernel(x_ref, mod_ref, nw_ref, w_ref, wg_ref, o_ref, glr_ref, h_scr):
    @pl.when(pl.program_id(1) == 0)
    def _():
        y = _rms(x_ref[...]) * nw_ref[...]
        h = (y * (1.0 + mod_ref[0, 1:2, :]) + mod_ref[0, 0:1, :]).astype(BF16)
        h_scr[...] = h
        glr_ref[...] = _dot(h, wg_ref[...])

    o_ref[...] = _dot(h_scr[...], w_ref[...])


def _proj_in(x, mod, norm_w, w_main, w_glr, row):
    n = x.shape[0]
    tm, tn = 512, 1536
    return pl.pallas_call(
        _proj_in_kernel,
        grid=(n // tm, PROJ_COLS // tn),
        in_specs=[pl.BlockSpec((tm, D_MODEL), lambda i, j: (i, 0)),
                  pl.BlockSpec((1, 6, D_MODEL), lambda i, j: (row(i), 0, 0)),
                  pl.BlockSpec((1, D_MODEL), lambda i, j: (0, 0)),
                  pl.BlockSpec((D_MODEL, tn), lambda i, j: (0, j)),
                  pl.BlockSpec((D_MODEL, LANES), lambda i, j: (0, 0))],
        out_specs=[pl.BlockSpec((tm, tn), lambda i, j: (i, j)),
                   pl.BlockSpec((tm, LANES), lambda i, j: (i, 0))],
        out_shape=[jax.ShapeDtypeStruct((n, PROJ_COLS), F32),
                   jax.ShapeDtypeStruct((n, LANES), F32)],
        scratch_shapes=[pltpu.VMEM((tm, D_MODEL), BF16)],
        compiler_params=_params("parallel", "arbitrary"),
        name="proj_in",
    )(x, mod, norm_w, w_main, w_glr)


GLA_HB = 4


def _gla_kernel(qk_ref, v_ref, glr_ref, wg_ref, bg_ref, swap_ref, mask_ref, init_ref, out_hbm_ref, o_ref, st_ref,
                st_scr, x1_scr, k1_scr, x2_scr, k2_scr, vb_scr, dec_scr, upd_scr, sp_scr, *, n_chunks):
    del out_hbm_ref
    d = pl.program_id(2)
    t = pl.program_id(3)
    fwd = d == 0

    @pl.when(t == 0)
    def _():
        st_scr[...] = init_ref[0, 0]

    c = GLA_CHUNK
    hw = GLA_HB * LANES
    mask = mask_ref[0]
    qmask = lax.broadcasted_iota(jnp.int32, (1, hw), 1) % LANES < GLA_DK

    z = _dot(glr_ref[...].astype(BF16), wg_ref[0]) + bg_ref[0]
    la = jax.nn.log_sigmoid(z) / GLA_TAU
    blk = 2 * c
    m01 = mask[0:blk, 0:blk].astype(BF16)
    b_all = jnp.concatenate([_dot_exact01(m01, la[r:r + blk, :]) for r in range(0, n_chunks * c, blk)], axis=0)
    vb_scr[...] = v_ref[...].astype(BF16)
    for k in range(n_chunks):
        rows = slice(k * c, (k + 1) * c)
        b = b_all[rows, :]
        b_mid = jnp.where(fwd, b[c // 2 - 1:c // 2, :], b[c // 2:c // 2 + 1, :])
        b_end = jnp.where(fwd, b[c - 1:c, :], b[0:1, :])
        qk = qk_ref[rows, :]
        x1_scr[rows, :] = (qk * jnp.exp(jnp.where(qmask, b - b_mid, b_mid - b))).astype(BF16)
        x2_scr[rows, :] = (qk * jnp.exp(jnp.where(qmask, b, b_end - b))).astype(BF16)
        dec_scr[k] = jnp.exp(b_end)
    swap = swap_ref[...]
    for h in range(GLA_HB):
        hl = slice(h * LANES, (h + 1) * LANES)
        k1_scr[:, hl] = _dot(x1_scr[:, hl], swap).astype(BF16)
        k2_scr[:, hl] = _dot(x2_scr[:, hl], swap).astype(BF16)

    for h in range(GLA_HB):
        hl = slice(h * LANES, (h + 1) * LANES)
        s = lax.dot_general(x1_scr[:, hl], k1_scr[:, hl], NT_DIMS, preferred_element_type=F32)
        s = jnp.where(mask > 0.0, s, 0.0).astype(BF16)
        o_ref[0, :, hl] = _dot(s, vb_scr[:, hl])

    for k in range(n_chunks):
        rows = slice(k * c, (k + 1) * c)
        for h in range(GLA_HB):
            hl = slice(h * LANES, (h + 1) * LANES)
            upd_scr[k, h] = lax.dot_general(vb_scr[rows, hl], k2_scr[rows, hl], TN_DIMS,
                                            preferred_element_type=F32)

    def chunk(step, carry):
        k = jnp.where(fwd, step, n_chunks - 1 - step)
        dec = dec_scr[k]
        for h in range(GLA_HB):
            st = st_scr[h]
            sp_scr[k, h] = st.astype(BF16)
            st_scr[h] = st * dec[:, h * LANES:(h + 1) * LANES] + upd_scr[k, h]
        return carry

    lax.fori_loop(0, n_chunks, chunk, 0)

    for k in range(n_chunks):
        rows = slice(k * c, (k + 1) * c)
        for h in range(GLA_HB):
            hl = slice(h * LANES, (h + 1) * LANES)
            o_ref[0, rows, hl] += lax.dot_general(x2_scr[rows, hl], sp_scr[k, h], NT_DIMS,
                                                  preferred_element_type=F32)

    @pl.when(t == pl.num_programs(3) - 1)
    def _():
        st_ref[0, 0] = st_scr[...]


def _gla(proj, glr, wg, bg, init_t, out, *, row0, batch, seq, tile):
    nt = seq // tile
    rb0 = row0 // tile
    hw = GLA_HB * LANES
    n_chunks = tile // GLA_CHUNK
    l = np.arange(LANES)
    swap = ((l[:, None] == l[None, :] + GLA_DK) & (l[None, :] < GLA_DK)).astype(np.float32)
    r = np.arange(tile)
    same = r[:, None] // GLA_CHUNK == r[None, :] // GLA_CHUNK
    mask = np.stack([same & (r[:, None] >= r[None, :]), same & (r[:, None] <= r[None, :])]).astype(np.float32)

    def rblk(b, t, d):
        return rb0 + b * nt + jnp.where(d == 0, t, nt - 1 - t)

    in_specs = [pl.BlockSpec((tile, hw), lambda b, g, d, t: (rblk(b, t, d), COL_QK // hw + g)),
                pl.BlockSpec((tile, hw), lambda b, g, d, t: (rblk(b, t, d), COL_V // hw + g)),
                pl.BlockSpec((tile, LANES), lambda b, g, d, t: (rblk(b, t, d), 0)),
                pl.BlockSpec((1, LANES, hw), lambda b, g, d, t: (d, 0, g)),
                pl.BlockSpec((1, 1, hw), lambda b, g, d, t: (d, 0, g)),
                pl.BlockSpec((LANES, LANES), lambda b, g, d, t: (0, 0)),
                pl.BlockSpec((1, tile, tile), lambda b, g, d, t: (d, 0, 0)),
                pl.BlockSpec((1, 1, GLA_HB, LANES, LANES), lambda b, g, d, t: (b, d, g, 0, 0)),
                pl.BlockSpec(memory_space=pl.ANY)]
    args = [proj, proj, glr, wg, bg, jnp.asarray(swap, BF16), jnp.asarray(mask), init_t, out]
    return pl.pallas_call(
        functools.partial(_gla_kernel, n_chunks=n_chunks),
        grid=(batch, GLA_HEADS // GLA_HB, 2, nt),
        in_specs=in_specs,
        out_specs=[pl.BlockSpec((1, tile, hw), lambda b, g, d, t: (d, rblk(b, t, d), g)),
                   pl.BlockSpec((1, 1, GLA_HB, LANES, LANES), lambda b, g, d, t: (b, d, g, 0, 0))],
        out_shape=[jax.ShapeDtypeStruct(out.shape, out.dtype),
                   jax.ShapeDtypeStruct((batch, 2, GLA_HEADS, LANES, LANES), F32)],
        scratch_shapes=[pltpu.VMEM((GLA_HB, LANES, LANES), F32)]
                       + [pltpu.VMEM((tile, hw), BF16)] * 5
                       + [pltpu.VMEM((n_chunks, 1, hw), F32),
                          pltpu.VMEM((n_chunks, GLA_HB, LANES, LANES), F32),
                          pltpu.VMEM((n_chunks, GLA_HB, LANES, LANES), BF16)],
        input_output_aliases={len(args) - 1: 0},
        compiler_params=_params("parallel", "parallel", "arbitrary", "arbitrary"),
        name="gla",
    )(*args)


def _prep_kernel(*refs, rope, cache):
    qa_ref, kv_ref, qw_ref, kw_ref = refs[:4]
    refs = refs[4:]
    if rope:
        cos_ref, sa_ref, sb_ref = refs[:3]
        refs = refs[3:]
    qn_ref, kn_ref, vb_ref = refs[:3]
    if cache:
        ck_ref, cv_ref = refs[3:5]

    def rot(y):
        if not rope:
            return y
        return (y * cos_ref[...] + pltpu.roll(y, LANES - 32, 1) * sa_ref[...]
                + pltpu.roll(y, 32, 1) * sb_ref[...])

    for h in range(ATT_HEADS):
        hl = slice(h * HEAD_DIM, (h + 1) * HEAD_DIM)
        qn_ref[:, hl] = rot(_rms(qa_ref[:, hl]) * qw_ref[...]).astype(BF16)
    for h in range(ATT_KV_HEADS):
        hl = slice(h * HEAD_DIM, (h + 1) * HEAD_DIM)
        vl = slice((ATT_KV_HEADS + h) * HEAD_DIM, (ATT_KV_HEADS + h + 1) * HEAD_DIM)
        kn = _rms(kv_ref[:, hl]) * kw_ref[...]
        v = kv_ref[:, vl]
        kn_ref[:, hl] = rot(kn).astype(BF16)
        vb_ref[:, hl] = v.astype(BF16)
        if cache:
            ck_ref[0, h] = kn
            cv_ref[0, h] = v


def _prep(proj, qw, kw, rope_tabs, *, row0, batch, seq, cache):
    tm = 256
    n = batch * seq
    rb0 = row0 // tm
    per_req = seq // tm
    kvw = ATT_KV_HEADS * HEAD_DIM
    rope = rope_tabs is not None
    in_specs = [pl.BlockSpec((tm, ATT_HEADS * HEAD_DIM), lambda i: (rb0 + i, COL_QA // (ATT_HEADS * HEAD_DIM))),
                pl.BlockSpec((tm, 2 * kvw), lambda i: (rb0 + i, COL_KV // (2 * kvw))),
                pl.BlockSpec((1, HEAD_DIM), lambda i: (0, 0)),
                pl.BlockSpec((1, HEAD_DIM), lambda i: (0, 0))]
    args = [proj, proj, qw, kw]
    if rope:
        in_specs += [pl.BlockSpec((tm, HEAD_DIM), lambda i: (i % per_req, 0))] * 3
        args += list(rope_tabs)
    out_specs = [pl.BlockSpec((tm, ATT_HEADS * HEAD_DIM), lambda i: (i, 0)),
                 pl.BlockSpec((tm, kvw), lambda i: (i, 0)),
                 pl.BlockSpec((tm, kvw), lambda i: (i, 0))]
    out_shape = [jax.ShapeDtypeStruct((n, ATT_HEADS * HEAD_DIM), BF16),
                 jax.ShapeDtypeStruct((n, kvw), BF16),
                 jax.ShapeDtypeStruct((n, kvw), BF16)]
    if cache:
        assert seq == tm
        out_specs += [pl.BlockSpec((1, ATT_KV_HEADS, seq, HEAD_DIM), lambda i: (i, 0, 0, 0))] * 2
        out_shape += [jax.ShapeDtypeStruct((batch, ATT_KV_HEADS, seq, HEAD_DIM), F32)] * 2
    return pl.pallas_call(
        functools.partial(_prep_kernel, rope=rope, cache=cache),
        grid=(n // tm,),
        in_specs=in_specs, out_specs=out_specs, out_shape=out_shape,
        compiler_params=_params("parallel"),
        name="attn_prep",
    )(*args)


def _rope_tables(seq):
    half = HEAD_DIM // 2
    tok = jnp.arange(seq)
    row = (tok // GRID_W).astype(F32)
    col = (tok % GRID_W).astype(F32)
    inv_freq = ROPE_THETA ** (-jnp.arange(0, half, 2, dtype=F32) / half)
    ang_r = row[:, None] * inv_freq[None, :]
    ang_c = col[:, None] * inv_freq[None, :]
    ang = jnp.concatenate([ang_r, ang_r, ang_c, ang_c], axis=-1)
    first = (jnp.arange(HEAD_DIM) % half) < half // 2
    sin = jnp.sin(ang)
    return jnp.cos(ang), jnp.where(first, -sin, 0.0), jnp.where(first, 0.0, sin)


ATT_COL_BLOCK = 512

def _attn_kernel(*refs, ctx):
    o_ref, s_scr = refs[-2], refs[-1]
    refs = refs[:-3]
    if ctx:
        q_ref, k_ref, v_ref, ck_ref, cv_ref = refs
        ck = ck_ref[0, 0, 0].astype(BF16)
        cv = cv_ref[0, 0, 0].astype(BF16)
    else:
        q_ref, k_ref, v_ref = refs
    k = k_ref[...]
    v = v_ref[...]
    c2 = (HEAD_DIM ** -0.5) * LOG2_E

    n_keys = k_ref.shape[0]
    blocks = [(k, v, r, min(ATT_COL_BLOCK, n_keys - r), r) for r in range(0, n_keys, ATT_COL_BLOCK)]
    if ctx:
        past = ck.shape[0]
        blocks += [(ck, cv, r, min(ATT_COL_BLOCK, past - r), n_keys + r) for r in range(0, past, ATT_COL_BLOCK)]

    def scores(g):
        q = q_ref[:, g * HEAD_DIM:(g + 1) * HEAD_DIM]
        for kk, _, r, w, col in blocks:
            s_scr[g % 2, :, col:col + w] = lax.dot_general(q, kk[r:r + w, :], NT_DIMS, preferred_element_type=F32)

    def lane_fold(x, op):
        parts = [x[:, i:i + LANES] for i in range(0, x.shape[1], LANES)]
        while len(parts) > 1:
            parts = [op(parts[i], parts[i + 1]) for i in range(0, len(parts) - 1, 2)] + parts[len(parts) & ~1:]
        return parts[0]

    def finish(g):
        sg = s_scr.at[g % 2]
        mp = None
        for _, _, _, w, col in blocks:
            part = lane_fold(sg[:, col:col + w], jnp.maximum)
            mp = part if mp is None else jnp.maximum(mp, part)
        mc = jnp.max(mp, axis=-1, keepdims=True) * c2
        lp = jnp.zeros((q_ref.shape[0], LANES), F32)
        acc = jnp.zeros((q_ref.shape[0], HEAD_DIM), F32)
        for _, vv, r, w, col in blocks:
            p = jnp.exp2(sg[:, col:col + w] * c2 - mc)
            lp = lp + lane_fold(p, jnp.add)
            acc = acc + _dot(p.astype(BF16), vv[r:r + w, :])
        l = jnp.sum(lp, axis=-1, keepdims=True)
        o_ref[:, g * HEAD_DIM:(g + 1) * HEAD_DIM] = (acc / l).astype(BF16)

    scores(0)
    for g in range(ATT_GROUP):
        if g + 1 < ATT_GROUP:
            scores(g + 1)
        finish(g)


def _attention(qn, kn, vb, ctx_kv, out, *, row0, batch, seq, tq):
    nq = seq // tq
    ob0 = row0 // tq
    past = ctx_kv[0].shape[3] if ctx_kv is not None else 0
    assert seq % LANES == 0 and past % LANES == 0
    gw = ATT_GROUP * HEAD_DIM
    in_specs = [pl.BlockSpec((tq, gw), lambda b, h, t: (b * nq + t, h)),
                pl.BlockSpec((seq, HEAD_DIM), lambda b, h, t: (b, h)),
                pl.BlockSpec((seq, HEAD_DIM), lambda b, h, t: (b, h))]
    args = [qn, kn, vb]
    if ctx_kv is not None:
        ck, cv, layer = ctx_kv
        spec = pl.BlockSpec((1, 1, 1, past, HEAD_DIM), lambda b, h, t: (b, layer, h, 0, 0))
        in_specs += [spec, spec]
        args += [ck, cv]
    in_specs.append(pl.BlockSpec(memory_space=pl.ANY))
    args.append(out)
    return pl.pallas_call(
        functools.partial(_attn_kernel, ctx=ctx_kv is not None),
        grid=(batch, ATT_KV_HEADS, nq),
        in_specs=in_specs,
        out_specs=pl.BlockSpec((tq, gw), lambda b, h, t: (ob0 + b * nq + t, h)),
        out_shape=jax.ShapeDtypeStruct(out.shape, out.dtype),
        scratch_shapes=[pltpu.VMEM((2, tq, seq + past), F32)],
        input_output_aliases={len(args) - 1: 0},
        compiler_params=_params("parallel", "parallel", "arbitrary"),
        name="attention",
    )(*args)


def _out_proj_kernel(x_ref, od_ref, og_ref, oa_ref, gn_ref, w_ref, mod_ref, nw_ref, x1_ref, h2_ref):
    o = od_ref[0] + od_ref[1]
    parts = []
    for h in range(GLA_HEADS):
        hl = slice(h * GLA_DV, (h + 1) * GLA_DV)
        parts.append((_rms(o[:, hl]) * gn_ref[...] * jax.nn.silu(og_ref[:, hl])).astype(BF16))
    og = jnp.concatenate(parts, axis=-1)
    gw = GLA_HEADS * GLA_DV
    mix = _dot(og, w_ref[0:gw, :]) + _dot(oa_ref[...], w_ref[gw:, :])
    x1 = x_ref[...] + mod_ref[0, 2:3, :] * mix
    x1_ref[...] = x1
    h2_ref[...] = _rms(x1) * nw_ref[...] * (1.0 + mod_ref[0, 4:5, :]) + mod_ref[0, 3:4, :]


def _out_proj(x, odir, proj, oa, gla_norm, w_out, mod, norm_w, row):
    n = x.shape[0]
    tm = 256
    gw = GLA_HEADS * GLA_DV
    return pl.pallas_call(
        _out_proj_kernel,
        grid=(n // tm,),
        in_specs=[pl.BlockSpec((tm, D_MODEL), lambda i: (i, 0)),
                  pl.BlockSpec((2, tm, gw), lambda i: (0, i, 0)),
                  pl.BlockSpec((tm, gw), lambda i: (i, COL_OG // gw)),
                  pl.BlockSpec((tm, ATT_HEADS * HEAD_DIM), lambda i: (i, 0)),
                  pl.BlockSpec((1, GLA_DV), lambda i: (0, 0)),
                  pl.BlockSpec((gw + ATT_HEADS * HEAD_DIM, D_MODEL), lambda i: (0, 0)),
                  pl.BlockSpec((1, 6, D_MODEL), lambda i: (row(i), 0, 0)),
                  pl.BlockSpec((1, D_MODEL), lambda i: (0, 0))],
        out_specs=[pl.BlockSpec((tm, D_MODEL), lambda i: (i, 0)),
                   pl.BlockSpec((tm, D_MODEL), lambda i: (i, 0))],
        out_shape=[jax.ShapeDtypeStruct((n, D_MODEL), F32),
                   jax.ShapeDtypeStruct((n, D_MODEL), F32)],
        compiler_params=_params("parallel"),
        name="out_proj",
    )(x, odir, proj, oa, gla_norm, w_out, mod, norm_w)


NEG_INF = float("-inf")
N_CAND = PEER_TOPK * PEER_TOPK


_CAND_GROUPS = ((0, 0), (0, 8), (1, 0), (2, 0), (3, 0), (4, 0), (5, 0), (6, 0), (7, 0))
N_CAND_ROWS = 8 * (len(_CAND_GROUPS) + 1)


def _cand_consts():
    pos = np.zeros((N_CAND_ROWS, 1), np.float32)
    off = np.zeros((N_CAND_ROWS, 1), np.float32)
    for g, (a, b0) in enumerate(_CAND_GROUPS):
        for j in range(8):
            pos[8 * g + j] = a * PEER_TOPK + b0 + j
            off[8 * g + j] = 0.0 if (a + 1) * (b0 + j + 1) <= PEER_TOPK else NEG_INF
    for j in range(8):
        pos[N_CAND_ROWS - 8 + j] = (8 + j) * PEER_TOPK
    return (jnp.asarray(np.broadcast_to(pos, (N_CAND_ROWS, LANES))),
            jnp.asarray(np.broadcast_to(off, (N_CAND_ROWS, LANES))))


def _pair_rows(first, second):
    rows = [first[a:a + 1, :] + second[b0:b0 + 8, :] for a, b0 in _CAND_GROUPS]
    rows.append(first[8:16, :] + second[0:1, :])
    return jnp.concatenate(rows, axis=0)


def _top16_rows(s, row_id):
    rank = lax.broadcasted_iota(jnp.int32, (PEER_TOPK, s.shape[1]), 0)
    vals = jnp.zeros((PEER_TOPK, s.shape[1]), F32)
    ids = jnp.zeros((PEER_TOPK, s.shape[1]), F32)
    for r in range(PEER_TOPK):
        m = jnp.max(s, axis=0, keepdims=True)
        i = jnp.min(jnp.where(s == m, row_id, float(N_CAND)), axis=0, keepdims=True)
        s = jnp.where(row_id == i, NEG_INF, s)
        vals = jnp.where(rank == r, m, vals)
        ids = jnp.where(rank == r, i, ids)
    return vals, ids


def _peer_topk_kernel(h_ref, wq_ref, keys_ref, pos_ref, off_ref, idx_ref, gate_ref, q_scr):
    tm = h_ref.shape[0]
    q = _dot(h_ref[...].astype(BF16), wq_ref[...]).astype(BF16)
    for a in range(2 * PEER_HEADS):
        q_scr[a] = q[:, a * LANES:(a + 1) * LANES]
    key_id = lax.broadcasted_iota(jnp.int32, (N_KEYS, LANES), 0).astype(F32)
    rank = lax.broadcasted_iota(jnp.int32, (PEER_TOPK, LANES), 0)
    pos = pos_ref[...]
    off = off_ref[...]

    def head(h, carry):
        out_rows = pl.ds(pl.multiple_of(h * PEER_TOPK, PEER_TOPK), PEER_TOPK)
        for c in range(tm // LANES):
            cols = slice(c * LANES, (c + 1) * LANES)
            s1 = lax.dot_general(keys_ref[h, 0], q_scr[2 * h, cols, :], NT_DIMS, preferred_element_type=F32)
            s2 = lax.dot_general(keys_ref[h, 1], q_scr[2 * h + 1, cols, :], NT_DIMS, preferred_element_type=F32)
            v1, i1 = _top16_rows(s1, key_id)
            v2, i2 = _top16_rows(s2, key_id)
            cand = _pair_rows(v1, v2) + off
            cidx = _pair_rows(i1 * float(N_KEYS), i2)
            best = jnp.zeros((PEER_TOPK, LANES), F32)
            eidx = jnp.zeros((PEER_TOPK, LANES), F32)
            for r in range(PEER_TOPK):
                m = jnp.max(cand, axis=0, keepdims=True)
                p = jnp.min(jnp.where(cand == m, pos, float(N_CAND)), axis=0, keepdims=True)
                sel = pos == p
                e = jnp.max(jnp.where(sel, cidx, -1.0), axis=0, keepdims=True)
                cand = jnp.where(sel, NEG_INF, cand)
                best = jnp.where(rank == r, m, best)
                eidx = jnp.where(rank == r, e, eidx)
            ex = jnp.exp(best - best[0:1, :])
            gate_ref[out_rows, cols] = ex / jnp.sum(ex, axis=0, keepdims=True)
            idx_ref[out_rows, cols] = eidx.astype(jnp.int32)
        return carry

    lax.fori_loop(0, PEER_HEADS, head, 0)


def _peer_topk(h2, wq, keys):
    n = h2.shape[0]
    tm = 256
    qw = PEER_HEADS * PEER_QDIM
    full = lambda shape: pl.BlockSpec(shape, lambda i: (0,) * len(shape))
    return pl.pallas_call(
        _peer_topk_kernel,
        grid=(n // tm,),
        in_specs=[pl.BlockSpec((tm, D_MODEL), lambda i: (i, 0)),
                  full((D_MODEL, qw)),
                  full((PEER_HEADS, 2, N_KEYS, PEER_QDIM // 2)),
                  full((N_CAND_ROWS, LANES)), full((N_CAND_ROWS, LANES))],
        out_specs=[pl.BlockSpec((PEER_SEL, tm), lambda i: (0, i)),
                   pl.BlockSpec((PEER_SEL, tm), lambda i: (0, i))],
        out_shape=[jax.ShapeDtypeStruct((PEER_SEL, n), jnp.int32),
                   jax.ShapeDtypeStruct((PEER_SEL, n), F32)],
        scratch_shapes=[pltpu.VMEM((2 * PEER_HEADS, tm, LANES), BF16)],
        compiler_params=_params("parallel"),
        name="peer_topk",
    )(h2, wq, keys, *_cand_consts())


U32 = jnp.uint32


def _pack_uv_kernel(u_ref, v_ref, o_ref):
    ub = lax.bitcast_convert_type(u_ref[0].astype(BF16).astype(F32), U32)
    vb = lax.bitcast_convert_type(v_ref[0].astype(BF16).astype(F32), U32)
    o_ref[:, 0, :] = ub | (vb >> 16)


def _pack_uv(u, v, layer):
    n, tr = u.shape[1], 512
    spec = pl.BlockSpec((1, tr, D_MODEL), lambda i: (layer, i, 0))
    return pl.pallas_call(
        _pack_uv_kernel, grid=(n // tr,), in_specs=[spec, spec],
        out_specs=pl.BlockSpec((tr, 1, D_MODEL), lambda i: (i, 0, 0)),
        out_shape=jax.ShapeDtypeStruct((n, 1, D_MODEL), U32),
        compiler_params=_params("parallel"), name="pack_uv",
    )(u, v)


PEER_TB = 256
PEER_NBUF = 8


def _peer_mix_kernel(idx_ref, gate_ref, h_ref, x_ref, mod_ref, uv_ref, o_ref, *scratch):
    bufs, sem = scratch[:PEER_NBUF], scratch[PEER_NBUF]
    tb = h_ref.shape[0]
    n_groups = tb // PEER_NBUF

    def issue(t, slot):
        for k in range(PEER_SEL):
            pltpu.make_async_copy(uv_ref.at[idx_ref[k, t]], bufs[slot].at[pl.ds(k, 1), :],
                                  sem.at[slot]).start(priority=k % 2)

    def wait(slot):
        pltpu.make_async_copy(bufs[slot], bufs[slot], sem.at[slot]).wait()

    gate_t = gate_ref[...]
    tok = lax.broadcasted_iota(jnp.int32, (1, tb), 1)
    g2 = mod_ref[0, 5:6, :]

    def token(t, slot, prefetch):
        wait(slot)
        if prefetch:
            issue(t + PEER_NBUF - 1, (slot + PEER_NBUF - 1) % PEER_NBUF)
        x = h_ref[pl.ds(t, 1), :]
        w = bufs[slot][...]
        u = lax.bitcast_convert_type(w & jnp.uint32(0xFFFF0000), F32)
        s = jnp.sum(u * x, axis=-1, keepdims=True)
        g = jnp.sum(jnp.where(tok == t, gate_t, 0.0), axis=-1, keepdims=True)
        act = jax.nn.gelu(s) * g
        v = lax.bitcast_convert_type(w << 16, F32)
        o = jnp.sum(act * v, axis=0, keepdims=True)
        o_ref[pl.ds(t, 1), :] = x_ref[pl.ds(t, 1), :] + g2 * o

    for t0 in range(PEER_NBUF - 1):
        issue(t0, t0)

    def group(g, carry):
        for j in range(PEER_NBUF):
            token(g * PEER_NBUF + j, j, True)
        return carry

    lax.fori_loop(0, n_groups - 1, group, 0)
    for j in range(PEER_NBUF):
        token((n_groups - 1) * PEER_NBUF + j, j, j == 0)


def _peer_mix(idx_t, gates_t, h2, x1, mod, uv, row):
    n = h2.shape[0]
    tb = PEER_TB
    return pl.pallas_call(
        _peer_mix_kernel,
        grid=(n // tb,),
        in_specs=[pl.BlockSpec((PEER_SEL, tb), lambda i: (0, i), memory_space=pltpu.SMEM),
                  pl.BlockSpec((PEER_SEL, tb), lambda i: (0, i)),
                  pl.BlockSpec((tb, D_MODEL), lambda i: (i, 0)),
                  pl.BlockSpec((tb, D_MODEL), lambda i: (i, 0)),
                  pl.BlockSpec((1, 6, D_MODEL), lambda i: (row(i), 0, 0)),
                  pl.BlockSpec(memory_space=pl.ANY)],
        out_specs=pl.BlockSpec((tb, D_MODEL), lambda i: (i, 0)),
        out_shape=jax.ShapeDtypeStruct((n, D_MODEL), F32),
        scratch_shapes=[pltpu.VMEM((PEER_SEL, D_MODEL), U32)] * PEER_NBUF
                       + [pltpu.SemaphoreType.DMA((PEER_NBUF,))],
        compiler_params=_params("arbitrary"),
        name="peer_mix",
    )(idx_t, gates_t, h2, x1, mod, uv)


def _layer_weights(w_in, w_gate, b_gate, w_out, peer_wq, peer_keys):
    o = np.cumsum((0, GLA_HEADS * GLA_DK, GLA_HEADS * GLA_DK, GLA_HEADS * GLA_DV, 2 * GLA_RANK,
                   GLA_HEADS * GLA_DV, ATT_HEADS * HEAD_DIM, ATT_KV_HEADS * HEAD_DIM, ATT_KV_HEADS * HEAD_DIM))
    wq_g = w_in[:, o[0]:o[1]].reshape(D_MODEL, GLA_HEADS, GLA_DK) * (GLA_DK ** -0.5)
    wk_g = w_in[:, o[1]:o[2]].reshape(D_MODEL, GLA_HEADS, GLA_DK)
    w_qk = jnp.concatenate([wq_g, wk_g], axis=-1).reshape(D_MODEL, GLA_HEADS * LANES)
    w_main = jnp.concatenate([w_qk, w_in[:, o[2]:o[3]], w_in[:, o[4]:o[5]], w_in[:, o[5]:o[6]],
                              w_in[:, o[6]:o[7]], w_in[:, o[7]:o[8]]], axis=-1).astype(BF16)
    w_glr = jnp.pad(w_in[:, o[3]:o[4]], ((0, 0), (0, LANES - 2 * GLA_RANK))).astype(BF16)
    wg = w_gate.reshape(2, GLA_RANK, GLA_HEADS, GLA_DK)
    wg = jnp.concatenate([wg, wg], axis=-1).reshape(2, GLA_RANK, GLA_HEADS * LANES)
    wg = jnp.stack([jnp.pad(wg[0], ((0, LANES - GLA_RANK), (0, 0))),
                    jnp.pad(wg[1], ((GLA_RANK, LANES - 2 * GLA_RANK), (0, 0)))]).astype(BF16)
    bg = b_gate.reshape(2, GLA_HEADS, GLA_DK)
    bg = jnp.concatenate([bg, bg], axis=-1).reshape(2, 1, GLA_HEADS * LANES)
    return dict(w_main=w_main, w_glr=w_glr, wg=wg, bg=bg, w_out=w_out.astype(BF16),
                wq=peer_wq.astype(BF16), keys=peer_keys.astype(BF16))


def _state_to_kernel(s):
    st = jnp.swapaxes(s, -1, -2)
    return jnp.pad(st, [(0, 0)] * (st.ndim - 1) + [(0, LANES - GLA_DK)])


def _state_from_kernel(st):
    return jnp.swapaxes(st[..., :GLA_DK], -1, -2)


def kernel(x_prompt, x_sample, cache_k, cache_v, state_gla, c, c_ctx, w_mod, b_mod, norm_mix, norm_ffn,
           w_in, w_gate, b_gate, gla_norm, q_norm, k_norm, w_out, peer_wq, peer_keys, peer_u, peer_v):
    bc, tc, _ = x_prompt.shape
    bl, tl, _ = x_sample.shape
    n_ctx, n_lat = bc * tc, bl * tl
    assert bl + 1 <= MOD_ROWS

    cvec = jnp.concatenate([c_ctx[None, :], c, jnp.zeros((MOD_ROWS - 1 - bl, D_MODEL), F32)], axis=0)
    mod_all = _modulation(cvec, w_mod, b_mod)
    rope_tabs = _rope_tables(tl)
    x = jnp.concatenate([x_prompt.reshape(n_ctx, D_MODEL), x_sample.reshape(n_lat, D_MODEL)], axis=0)
    zero_state = jnp.zeros((bc, 2, GLA_HEADS, LANES, LANES), F32)

    ks, vs, ss = [], [], []
    for l in range(DEPTH):
        w = _layer_weights(w_in[l], w_gate[l], b_gate[l], w_out[l], peer_wq[l], peer_keys[l])
        uv = _pack_uv(peer_u, peer_v, l)
        mod = mod_all[l]
        proj, glr = _proj_in(x, mod, norm_mix[l][None, :], w["w_main"], w["w_glr"], _mod_row(512, n_ctx, tl))

        odir = jnp.zeros((2, n_ctx + n_lat, GLA_HEADS * GLA_DV), F32)
        odir, st_c = _gla(proj, glr, w["wg"], w["bg"], zero_state, odir, row0=0, batch=bc, seq=tc, tile=tc)
        odir, _ = _gla(proj, glr, w["wg"], w["bg"], _state_to_kernel(state_gla[:, l]), odir,
                       row0=n_ctx, batch=bl, seq=tl, tile=512)
        ss.append(_state_from_kernel(st_c))

        qw, kw = q_norm[l][None, :], k_norm[l][None, :]
        qn_c, kn_c, vb_c, ck, cv = _prep(proj, qw, kw, None, row0=0, batch=bc, seq=tc, cache=True)
        qn_l, kn_l, vb_l = _prep(proj, qw, kw, rope_tabs, row0=n_ctx, batch=bl, seq=tl, cache=False)
        ks.append(ck)
        vs.append(cv)
        oa = jnp.zeros((n_ctx + n_lat, ATT_HEADS * HEAD_DIM), BF16)
        oa = _attention(qn_c, kn_c, vb_c, None, oa, row0=0, batch=bc, seq=tc, tq=tc)
        oa = _attention(qn_l, kn_l, vb_l, (cache_k, cache_v, l), oa, row0=n_ctx, batch=bl, seq=tl, tq=128)

        x1, h2 = _out_proj(x, odir, proj, oa, gla_norm[l][None, :], w["w_out"], mod, norm_ffn[l][None, :],
                           _mod_row(256, n_ctx, tl))
        idx, gates = _peer_topk(h2, w["wq"], w["keys"])
        x = _peer_mix(idx, gates, h2, x1, mod, uv, _mod_row(PEER_TB, n_ctx, tl))

    y_prompt = x[:n_ctx].reshape(bc, tc, D_MODEL)
    y_sample = x[n_ctx:].reshape(bl, tl, D_MODEL)
    return (y_prompt, y_sample, jnp.stack(ks, axis=1), jnp.stack(vs, axis=1), jnp.stack(ss, axis=1))
```

```python
import functools

import jax
import jax.numpy as jnp
import numpy as np
from jax import lax
from jax.experimental import pallas as pl
from jax.experimental.pallas import tpu as pltpu

F32 = jnp.float32
BF16 = jnp.bfloat16

D_MODEL = 2048
DEPTH = 2
GRID_W = 64
GLA_HEADS = 8
GLA_DK = 64
GLA_DV = 128
GLA_RANK = 16
GLA_TAU = 16.0
GLA_CHUNK = 64
ATT_HEADS = 8
ATT_KV_HEADS = 2
ATT_GROUP = ATT_HEADS // ATT_KV_HEADS
HEAD_DIM = 128
ROPE_THETA = 10000.0
PEER_HEADS = 8
PEER_QDIM = 256
N_KEYS = 128
PEER_TOPK = 16
PEER_SEL = PEER_HEADS * PEER_TOPK
EPS = 1e-6
LOG2_E = 1.4426950408889634

LANES = 128
MOD_ROWS = 8
VMEM_LIMIT = 56 * 1024 * 1024

COL_QK = 0
COL_V = COL_QK + GLA_HEADS * LANES
COL_OG = COL_V + GLA_HEADS * GLA_DV
COL_QA = COL_OG + GLA_HEADS * GLA_DV
COL_KV = COL_QA + ATT_HEADS * HEAD_DIM
PROJ_COLS = COL_KV + 2 * ATT_KV_HEADS * HEAD_DIM

NT_DIMS = (((1,), (1,)), ((), ()))
TN_DIMS = (((0,), (0,)), ((), ()))


def _params(*sem):
    return pltpu.CompilerParams(dimension_semantics=sem, vmem_limit_bytes=VMEM_LIMIT)


def _dot(a, b):
    return jnp.dot(a, b, preferred_element_type=F32)


def _dot_exact01(a01, x):
    hi = x.astype(BF16)
    r1 = x - hi.astype(F32)
    mid = r1.astype(BF16)
    lo = (r1 - mid.astype(F32)).astype(BF16)
    return _dot(a01, hi) + _dot(a01, mid) + _dot(a01, lo)


def _rms(x):
    return x * lax.rsqrt(jnp.mean(x * x, axis=-1, keepdims=True) + EPS)


def _mod_kernel(cv_ref, w_ref, b_ref, o_ref):
    a = jax.nn.silu(cv_ref[...]).astype(BF16)
    o_ref[0] = _dot(a, w_ref[0].astype(BF16)) + b_ref[0]


def _modulation(cvec, w_mod, b_mod):
    tn = 1536
    n_out = 6 * D_MODEL
    out = pl.pallas_call(
        _mod_kernel,
        grid=(DEPTH, n_out // tn),
        in_specs=[pl.BlockSpec((MOD_ROWS, D_MODEL), lambda l, j: (0, 0)),
                  pl.BlockSpec((1, D_MODEL, tn), lambda l, j: (l, 0, j)),
                  pl.BlockSpec((1, 1, tn), lambda l, j: (l, 0, j))],
        out_specs=pl.BlockSpec((1, MOD_ROWS, tn), lambda l, j: (l, 0, j)),
        out_shape=jax.ShapeDtypeStruct((DEPTH, MOD_ROWS, n_out), F32),
        compiler_params=_params("parallel", "parallel"),
        name="modulation",
    )(cvec, w_mod, b_mod.reshape(DEPTH, 1, n_out))
    return out.reshape(DEPTH, MOD_ROWS, 6, D_MODEL)


def _mod_row(tile_rows, n_ctx, t_lat):
    def row(i):
        start = i * tile_rows
        return jnp.where(start < n_ctx, 0, 1 + (start - n_ctx) // t_lat)
    return row


def _proj_in_kernel(x_ref, mod_ref, nw_ref, w_ref, wg_ref, o_ref, glr_ref, h_scr):
    @pl.when(pl.program_id(1) == 0)
    def _():
        y = _rms(x_ref[...]) * nw_ref[...]
        h = (y * (1.0 + mod_ref[0, 1:2, :]) + mod_ref[0, 0:1, :]).astype(BF16)
        h_scr[...] = h
        glr_ref[...] = _dot(h, wg_ref[...])

    o_ref[...] = _dot(h_scr[...], w_ref[...])


def _proj_in(x, mod, norm_w, w_main, w_glr, row):
    n = x.shape[0]
    tm, tn = 512, 1536
    return pl.pallas_call(
        _proj_in_kernel,
        grid=(n // tm, PROJ_COLS // tn),
        in_specs=[pl.BlockSpec((tm, D_MODEL), lambda i, j: (i, 0)),
                  pl.BlockSpec((1, 6, D_MODEL), lambda i, j: (row(i), 0, 0)),
                  pl.BlockSpec((1, D_MODEL), lambda i, j: (0, 0)),
                  pl.BlockSpec((D_MODEL, tn), lambda i, j: (0, j)),
                  pl.BlockSpec((D_MODEL, LANES), lambda i, j: (0, 0))],
        out_specs=[pl.BlockSpec((tm, tn), lambda i, j: (i, j)),
                   pl.BlockSpec((tm, LANES), lambda i, j: (i, 0))],
        out_shape=[jax.ShapeDtypeStruct((n, PROJ_COLS), F32),
                   jax.ShapeDtypeStruct((n, LANES), F32)],
        scratch_shapes=[pltpu.VMEM((tm, D_MODEL), BF16)],
        compiler_params=_params("parallel", "arbitrary"),
        name="proj_in",
    )(x, mod, norm_w, w_main, w_glr)


GLA_HB = 4


def _gla_kernel(qk_ref, v_ref, glr_ref, wg_ref, bg_ref, swap_ref, mask_ref, init_ref, out_hbm_ref, o_ref, st_ref,
                st_scr, x1_scr, k1_scr, x2_scr, k2_scr, vb_scr, dec_scr, upd_scr, sp_scr, *, n_chunks):
    del out_hbm_ref
    d = pl.program_id(2)
    t = pl.program_id(3)
    fwd = d == 0

    @pl.when(t == 0)
    def _():
        st_scr[...] = init_ref[0, 0]

    c = GLA_CHUNK
    hw = GLA_HB * LANES
    mask = mask_ref[0]
    qmask = lax.broadcasted_iota(jnp.int32, (1, hw), 1) % LANES < GLA_DK

    z = _dot(glr_ref[...].astype(BF16), wg_ref[0]) + bg_ref[0]
    la = jax.nn.log_sigmoid(z) / GLA_TAU
    blk = 2 * c
    m01 = mask[0:blk, 0:blk].astype(BF16)
    b_all = jnp.concatenate([_dot_exact01(m01, la[r:r + blk, :]) for r in range(0, n_chunks * c, blk)], axis=0)
    vb_scr[...] = v_ref[...].astype(BF16)
    for k in range(n_chunks):
        rows = slice(k * c, (k + 1) * c)
        b = b_all[rows, :]
        b_mid = jnp.where(fwd, b[c // 2 - 1:c // 2, :], b[c // 2:c // 2 + 1, :])
        b_end = jnp.where(fwd, b[c - 1:c, :], b[0:1, :])
        qk = qk_ref[rows, :]
        x1_scr[rows, :] = (qk * jnp.exp(jnp.where(qmask, b - b_mid, b_mid - b))).astype(BF16)
        x2_scr[rows, :] = (qk * jnp.exp(jnp.where(qmask, b, b_end - b))).astype(BF16)
        dec_scr[k] = jnp.exp(b_end)
    swap = swap_ref[...]
    for h in range(GLA_HB):
        hl = slice(h * LANES, (h + 1) * LANES)
        k1_scr[:, hl] = _dot(x1_scr[:, hl], swap).astype(BF16)
        k2_scr[:, hl] = _dot(x2_scr[:, hl], swap).astype(BF16)

    for h in range(GLA_HB):
        hl = slice(h * LANES, (h + 1) * LANES)
        s = lax.dot_general(x1_scr[:, hl], k1_scr[:, hl], NT_DIMS, preferred_element_type=F32)
        s = jnp.where(mask > 0.0, s, 0.0).astype(BF16)
        o_ref[0, :, hl] = _dot(s, vb_scr[:, hl])

    for k in range(n_chunks):
        rows = slice(k * c, (k + 1) * c)
        for h in range(GLA_HB):
            hl = slice(h * LANES, (h + 1) * LANES)
            upd_scr[k, h] = lax.dot_general(vb_scr[rows, hl], k2_scr[rows, hl], TN_DIMS,
                                            preferred_element_type=F32)

    def chunk(step, carry):
        k = jnp.where(fwd, step, n_chunks - 1 - step)
        dec = dec_scr[k]
        for h in range(GLA_HB):
            st = st_scr[h]
            sp_scr[k, h] = st.astype(BF16)
            st_scr[h] = st * dec[:, h * LANES:(h + 1) * LANES] + upd_scr[k, h]
        return carry

    lax.fori_loop(0, n_chunks, chunk, 0)

    for k in range(n_chunks):
        rows = slice(k * c, (k + 1) * c)
        for h in range(GLA_HB):
            hl = slice(h * LANES, (h + 1) * LANES)
            o_ref[0, rows, hl] += lax.dot_general(x2_scr[rows, hl], sp_scr[k, h], NT_DIMS,
                                                  preferred_element_type=F32)

    @pl.when(t == pl.num_programs(3) - 1)
    def _():
        st_ref[0, 0] = st_scr[...]


def _gla(proj, glr, wg, bg, init_t, out, *, row0, batch, seq, tile):
    nt = seq // tile
    rb0 = row0 // tile
    hw = GLA_HB * LANES
    n_chunks = tile // GLA_CHUNK
    l = np.arange(LANES)
    swap = ((l[:, None] == l[None, :] + GLA_DK) & (l[None, :] < GLA_DK)).astype(np.float32)
    r = np.arange(tile)
    same = r[:, None] // GLA_CHUNK == r[None, :] // GLA_CHUNK
    mask = np.stack([same & (r[:, None] >= r[None, :]), same & (r[:, None] <= r[None, :])]).astype(np.float32)

    def rblk(b, t, d):
        return rb0 + b * nt + jnp.where(d == 0, t, nt - 1 - t)

    in_specs = [pl.BlockSpec((tile, hw), lambda b, g, d, t: (rblk(b, t, d), COL_QK // hw + g)),
                pl.BlockSpec((tile, hw), lambda b, g, d, t: (rblk(b, t, d), COL_V // hw + g)),
                pl.BlockSpec((tile, LANES), lambda b, g, d, t: (rblk(b, t, d), 0)),
                pl.BlockSpec((1, LANES, hw), lambda b, g, d, t: (d, 0, g)),
                pl.BlockSpec((1, 1, hw), lambda b, g, d, t: (d, 0, g)),
                pl.BlockSpec((LANES, LANES), lambda b, g, d, t: (0, 0)),
                pl.BlockSpec((1, tile, tile), lambda b, g, d, t: (d, 0, 0)),
                pl.BlockSpec((1, 1, GLA_HB, LANES, LANES), lambda b, g, d, t: (b, d, g, 0, 0)),
                pl.BlockSpec(memory_space=pl.ANY)]
    args = [proj, proj, glr, wg, bg, jnp.asarray(swap, BF16), jnp.asarray(mask), init_t, out]
    return pl.pallas_call(
        functools.partial(_gla_kernel, n_chunks=n_chunks),
        grid=(batch, GLA_HEADS // GLA_HB, 2, nt),
        in_specs=in_specs,
        out_specs=[pl.BlockSpec((1, tile, hw), lambda b, g, d, t: (d, rblk(b, t, d), g)),
                   pl.BlockSpec((1, 1, GLA_HB, LANES, LANES), lambda b, g, d, t: (b, d, g, 0, 0))],
        out_shape=[jax.ShapeDtypeStruct(out.shape, out.dtype),
                   jax.ShapeDtypeStruct((batch, 2, GLA_HEADS, LANES, LANES), F32)],
        scratch_shapes=[pltpu.VMEM((GLA_HB, LANES, LANES), F32)]
                       + [pltpu.VMEM((tile, hw), BF16)] * 5
                       + [pltpu.VMEM((n_chunks, 1, hw), F32),
                          pltpu.VMEM((n_chunks, GLA_HB, LANES, LANES), F32),
                          pltpu.VMEM((n_chunks, GLA_HB, LANES, LANES), BF16)],
        input_output_aliases={len(args) - 1: 0},
        compiler_params=_params("parallel", "parallel", "arbitrary", "arbitrary"),
        name="gla",
    )(*args)


def _prep_kernel(*refs, rope, cache):
    qa_ref, kv_ref, qw_ref, kw_ref = refs[:4]
    refs = refs[4:]
    if rope:
        cos_ref, sa_ref, sb_ref = refs[:3]
        refs = refs[3:]
    qn_ref, kn_ref, vb_ref = refs[:3]
    if cache:
        ck_ref, cv_ref = refs[3:5]

    def rot(y):
        if not rope:
            return y
        return (y * cos_ref[...] + pltpu.roll(y, LANES - 32, 1) * sa_ref[...]
                + pltpu.roll(y, 32, 1) * sb_ref[...])

    for h in range(ATT_HEADS):
        hl = slice(h * HEAD_DIM, (h + 1) * HEAD_DIM)
        qn_ref[:, hl] = rot(_rms(qa_ref[:, hl]) * qw_ref[...]).astype(BF16)
    for h in range(ATT_KV_HEADS):
        hl = slice(h * HEAD_DIM, (h + 1) * HEAD_DIM)
        vl = slice((ATT_KV_HEADS + h) * HEAD_DIM, (ATT_KV_HEADS + h + 1) * HEAD_DIM)
        kn = _rms(kv_ref[:, hl]) * kw_ref[...]
        v = kv_ref[:, vl]
        kn_ref[:, hl] = rot(kn).astype(BF16)
        vb_ref[:, hl] = v.astype(BF16)
        if cache:
            ck_ref[0, h] = kn
            cv_ref[0, h] = v


def _prep(proj, qw, kw, rope_tabs, *, row0, batch, seq, cache):
    tm = 256
    n = batch * seq
    rb0 = row0 // tm
    per_req = seq // tm
    kvw = ATT_KV_HEADS * HEAD_DIM
    rope = rope_tabs is not None
    in_specs = [pl.BlockSpec((tm, ATT_HEADS * HEAD_DIM), lambda i: (rb0 + i, COL_QA // (ATT_HEADS * HEAD_DIM))),
                pl.BlockSpec((tm, 2 * kvw), lambda i: (rb0 + i, COL_KV // (2 * kvw))),
                pl.BlockSpec((1, HEAD_DIM), lambda i: (0, 0)),
                pl.BlockSpec((1, HEAD_DIM), lambda i: (0, 0))]
    args = [proj, proj, qw, kw]
    if rope:
        in_specs += [pl.BlockSpec((tm, HEAD_DIM), lambda i: (i % per_req, 0))] * 3
        args += list(rope_tabs)
    out_specs = [pl.BlockSpec((tm, ATT_HEADS * HEAD_DIM), lambda i: (i, 0)),
                 pl.BlockSpec((tm, kvw), lambda i: (i, 0)),
                 pl.BlockSpec((tm, kvw), lambda i: (i, 0))]
    out_shape = [jax.ShapeDtypeStruct((n, ATT_HEADS * HEAD_DIM), BF16),
                 jax.ShapeDtypeStruct((n, kvw), BF16),
                 jax.ShapeDtypeStruct((n, kvw), BF16)]
    if cache:
        assert seq == tm
        out_specs += [pl.BlockSpec((1, ATT_KV_HEADS, seq, HEAD_DIM), lambda i: (i, 0, 0, 0))] * 2
        out_shape += [jax.ShapeDtypeStruct((batch, ATT_KV_HEADS, seq, HEAD_DIM), F32)] * 2
    return pl.pallas_call(
        functools.partial(_prep_kernel, rope=rope, cache=cache),
        grid=(n // tm,),
        in_specs=in_specs, out_specs=out_specs, out_shape=out_shape,
        compiler_params=_params("parallel"),
        name="attn_prep",
    )(*args)


def _rope_tables(seq):
    half = HEAD_DIM // 2
    tok = jnp.arange(seq)
    row = (tok // GRID_W).astype(F32)
    col = (tok % GRID_W).astype(F32)
    inv_freq = ROPE_THETA ** (-jnp.arange(0, half, 2, dtype=F32) / half)
    ang_r = row[:, None] * inv_freq[None, :]
    ang_c = col[:, None] * inv_freq[None, :]
    ang = jnp.concatenate([ang_r, ang_r, ang_c, ang_c], axis=-1)
    first = (jnp.arange(HEAD_DIM) % half) < half // 2
    sin = jnp.sin(ang)
    return jnp.cos(ang), jnp.where(first, -sin, 0.0), jnp.where(first, 0.0, sin)


ATT_COL_BLOCK = 512

def _attn_kernel(*refs, ctx):
    o_ref, s_scr = refs[-2], refs[-1]
    refs = refs[:-3]
    if ctx:
        q_ref, k_ref, v_ref, ck_ref, cv_ref = refs
        ck = ck_ref[0, 0, 0].astype(BF16)
        cv = cv_ref[0, 0, 0].astype(BF16)
    else:
        q_ref, k_ref, v_ref = refs
    k = k_ref[...]
    v = v_ref[...]
    c2 = (HEAD_DIM ** -0.5) * LOG2_E

    n_keys = k_ref.shape[0]
    blocks = [(k, v, r, min(ATT_COL_BLOCK, n_keys - r), r) for r in range(0, n_keys, ATT_COL_BLOCK)]
    if ctx:
        past = ck.shape[0]
        blocks += [(ck, cv, r, min(ATT_COL_BLOCK, past - r), n_keys + r) for r in range(0, past, ATT_COL_BLOCK)]

    def scores(g):
        q = q_ref[:, g * HEAD_DIM:(g + 1) * HEAD_DIM]
        for kk, _, r, w, col in blocks:
            s_scr[g % 2, :, col:col + w] = lax.dot_general(q, kk[r:r + w, :], NT_DIMS, preferred_element_type=F32)

    def lane_fold(x, op):
        parts = [x[:, i:i + LANES] for i in range(0, x.shape[1], LANES)]
        while len(parts) > 1:
            parts = [op(parts[i], parts[i + 1]) for i in range(0, len(parts) - 1, 2)] + parts[len(parts) & ~1:]
        return parts[0]

    def finish(g):
        sg = s_scr.at[g % 2]
        mp = None
        for _, _, _, w, col in blocks:
            part = lane_fold(sg[:, col:col + w], jnp.maximum)
            mp = part if mp is None else jnp.maximum(mp, part)
        mc = jnp.max(mp, axis=-1, keepdims=True) * c2
        lp = jnp.zeros((q_ref.shape[0], LANES), F32)
        acc = jnp.zeros((q_ref.shape[0], HEAD_DIM), F32)
        for _, vv, r, w, col in blocks:
            p = jnp.exp2(sg[:, col:col + w] * c2 - mc)
            lp = lp + lane_fold(p, jnp.add)
            acc = acc + _dot(p.astype(BF16), vv[r:r + w, :])
        l = jnp.sum(lp, axis=-1, keepdims=True)
        o_ref[:, g * HEAD_DIM:(g + 1) * HEAD_DIM] = (acc / l).astype(BF16)

    scores(0)
    for g in range(ATT_GROUP):
        if g + 1 < ATT_GROUP:
            scores(g + 1)
        finish(g)


def _attention(qn, kn, vb, ctx_kv, out, *, row0, batch, seq, tq):
    nq = seq // tq
    ob0 = row0 // tq
    past = ctx_kv[0].shape[3] if ctx_kv is not None else 0
    assert seq % LANES == 0 and past % LANES == 0
    gw = ATT_GROUP * HEAD_DIM
    in_specs = [pl.BlockSpec((tq, gw), lambda b, h, t: (b * nq + t, h)),
                pl.BlockSpec((seq, HEAD_DIM), lambda b, h, t: (b, h)),
                pl.BlockSpec((seq, HEAD_DIM), lambda b, h, t: (b, h))]
    args = [qn, kn, vb]
    if ctx_kv is not None:
        ck, cv, layer = ctx_kv
        spec = pl.BlockSpec((1, 1, 1, past, HEAD_DIM), lambda b, h, t: (b, layer, h, 0, 0))
        in_specs += [spec, spec]
        args += [ck, cv]
    in_specs.append(pl.BlockSpec(memory_space=pl.ANY))
    args.append(out)
    return pl.pallas_call(
        functools.partial(_attn_kernel, ctx=ctx_kv is not None),
        grid=(batch, ATT_KV_HEADS, nq),
        in_specs=in_specs,
        out_specs=pl.BlockSpec((tq, gw), lambda b, h, t: (ob0 + b * nq + t, h)),
        out_shape=jax.ShapeDtypeStruct(out.shape, out.dtype),
        scratch_shapes=[pltpu.VMEM((2, tq, seq + past), F32)],
        input_output_aliases={len(args) - 1: 0},
        compiler_params=_params("parallel", "parallel", "arbitrary"),
        name="attention",
    )(*args)


def _out_proj_kernel(x_ref, od_ref, og_ref, oa_ref, gn_ref, w_ref, mod_ref, nw_ref, x1_ref, h2_ref):
    o = od_ref[0] + od_ref[1]
    parts = []
    for h in range(GLA_HEADS):
        hl = slice(h * GLA_DV, (h + 1) * GLA_DV)
        parts.append((_rms(o[:, hl]) * gn_ref[...] * jax.nn.silu(og_ref[:, hl])).astype(BF16))
    og = jnp.concatenate(parts, axis=-1)
    gw = GLA_HEADS * GLA_DV
    mix = _dot(og, w_ref[0:gw, :]) + _dot(oa_ref[...], w_ref[gw:, :])
    x1 = x_ref[...] + mod_ref[0, 2:3, :] * mix
    x1_ref[...] = x1
    h2_ref[...] = _rms(x1) * nw_ref[...] * (1.0 + mod_ref[0, 4:5, :]) + mod_ref[0, 3:4, :]


def _out_proj(x, odir, proj, oa, gla_norm, w_out, mod, norm_w, row):
    n = x.shape[0]
    tm = 256
    gw = GLA_HEADS * GLA_DV
    return pl.pallas_call(
        _out_proj_kernel,
        grid=(n // tm,),
        in_specs=[pl.BlockSpec((tm, D_MODEL), lambda i: (i, 0)),
                  pl.BlockSpec((2, tm, gw), lambda i: (0, i, 0)),
                  pl.BlockSpec((tm, gw), lambda i: (i, COL_OG // gw)),
                  pl.BlockSpec((tm, ATT_HEADS * HEAD_DIM), lambda i: (i, 0)),
                  pl.BlockSpec((1, GLA_DV), lambda i: (0, 0)),
                  pl.BlockSpec((gw + ATT_HEADS * HEAD_DIM, D_MODEL), lambda i: (0, 0)),
                  pl.BlockSpec((1, 6, D_MODEL), lambda i: (row(i), 0, 0)),
                  pl.BlockSpec((1, D_MODEL), lambda i: (0, 0))],
        out_specs=[pl.BlockSpec((tm, D_MODEL), lambda i: (i, 0)),
                   pl.BlockSpec((tm, D_MODEL), lambda i: (i, 0))],
        out_shape=[jax.ShapeDtypeStruct((n, D_MODEL), F32),
                   jax.ShapeDtypeStruct((n, D_MODEL), F32)],
        compiler_params=_params("parallel"),
        name="out_proj",
    )(x, odir, proj, oa, gla_norm, w_out, mod, norm_w)


NEG_INF = float("-inf")
N_CAND = PEER_TOPK * PEER_TOPK


_CAND_GROUPS = ((0, 0), (0, 8), (1, 0), (2, 0), (3, 0), (4, 0), (5, 0), (6, 0), (7, 0))
N_CAND_ROWS = 8 * (len(_CAND_GROUPS) + 1)


def _cand_consts():
    pos = np.zeros((N_CAND_ROWS, 1), np.float32)
    off = np.zeros((N_CAND_ROWS, 1), np.float32)
    for g, (a, b0) in enumerate(_CAND_GROUPS):
        for j in range(8):
            pos[8 * g + j] = a * PEER_TOPK + b0 + j
            off[8 * g + j] = 0.0 if (a + 1) * (b0 + j + 1) <= PEER_TOPK else NEG_INF
    for j in range(8):
        pos[N_CAND_ROWS - 8 + j] = (8 + j) * PEER_TOPK
    return (jnp.asarray(np.broadcast_to(pos, (N_CAND_ROWS, LANES))),
            jnp.asarray(np.broadcast_to(off, (N_CAND_ROWS, LANES))))


def _pair_rows(first, second):
    rows = [first[a:a + 1, :] + second[b0:b0 + 8, :] for a, b0 in _CAND_GROUPS]
    rows.append(first[8:16, :] + second[0:1, :])
    return jnp.concatenate(rows, axis=0)


def _top16_rows(s, row_id):
    rank = lax.broadcasted_iota(jnp.int32, (PEER_TOPK, s.shape[1]), 0)
    vals = jnp.zeros((PEER_TOPK, s.shape[1]), F32)
    ids = jnp.zeros((PEER_TOPK, s.shape[1]), F32)
    for r in range(PEER_TOPK):
        m = jnp.max(s, axis=0, keepdims=True)
        i = jnp.min(jnp.where(s == m, row_id, float(N_CAND)), axis=0, keepdims=True)
        s = jnp.where(row_id == i, NEG_INF, s)
        vals = jnp.where(rank == r, m, vals)
        ids = jnp.where(rank == r, i, ids)
    return vals, ids


def _peer_topk_kernel(h_ref, wq_ref, keys_ref, pos_ref, off_ref, idx_ref, gate_ref, q_scr):
    tm = h_ref.shape[0]
    q = _dot(h_ref[...].astype(BF16), wq_ref[...]).astype(BF16)
    for a in range(2 * PEER_HEADS):
        q_scr[a] = q[:, a * LANES:(a + 1) * LANES]
    key_id = lax.broadcasted_iota(jnp.int32, (N_KEYS, LANES), 0).astype(F32)
    rank = lax.broadcasted_iota(jnp.int32, (PEER_TOPK, LANES), 0)
    pos = pos_ref[...]
    off = off_ref[...]

    def head(h, carry):
        out_rows = pl.ds(pl.multiple_of(h * PEER_TOPK, PEER_TOPK), PEER_TOPK)
        for c in range(tm // LANES):
            cols = slice(c * LANES, (c + 1) * LANES)
            s1 = lax.dot_general(keys_ref[h, 0], q_scr[2 * h, cols, :], NT_DIMS, preferred_element_type=F32)
            s2 = lax.dot_general(keys_ref[h, 1], q_scr[2 * h + 1, cols, :], NT_DIMS, preferred_element_type=F32)
            v1, i1 = _top16_rows(s1, key_id)
            v2, i2 = _top16_rows(s2, key_id)
            cand = _pair_rows(v1, v2) + off
            cidx = _pair_rows(i1 * float(N_KEYS), i2)
            best = jnp.zeros((PEER_TOPK, LANES), F32)
            eidx = jnp.zeros((PEER_TOPK, LANES), F32)
            for r in range(PEER_TOPK):
                m = jnp.max(cand, axis=0, keepdims=True)
                p = jnp.min(jnp.where(cand == m, pos, float(N_CAND)), axis=0, keepdims=True)
                sel = pos == p
                e = jnp.max(jnp.where(sel, cidx, -1.0), axis=0, keepdims=True)
                cand = jnp.where(sel, NEG_INF, cand)
                best = jnp.where(rank == r, m, best)
                eidx = jnp.where(rank == r, e, eidx)
            ex = jnp.exp(best - best[0:1, :])
            gate_ref[out_rows, cols] = ex / jnp.sum(ex, axis=0, keepdims=True)
            idx_ref[out_rows, cols] = eidx.astype(jnp.int32)
        return carry

    lax.fori_loop(0, PEER_HEADS, head, 0)


def _peer_topk(h2, wq, keys):
    n = h2.shape[0]
    tm = 1024
    qw = PEER_HEADS * PEER_QDIM
    full = lambda shape: pl.BlockSpec(shape, lambda i: (0,) * len(shape))
    return pl.pallas_call(
        _peer_topk_kernel,
        grid=(n // tm,),
        in_specs=[pl.BlockSpec((tm, D_MODEL), lambda i: (i, 0)),
                  full((D_MODEL, qw)),
                  full((PEER_HEADS, 2, N_KEYS, PEER_QDIM // 2)),
                  full((N_CAND_ROWS, LANES)), full((N_CAND_ROWS, LANES))],
        out_specs=[pl.BlockSpec((PEER_SEL, tm), lambda i: (0, i)),
                   pl.BlockSpec((PEER_SEL, tm), lambda i: (0, i))],
        out_shape=[jax.ShapeDtypeStruct((PEER_SEL, n), jnp.int32),
                   jax.ShapeDtypeStruct((PEER_SEL, n), F32)],
        scratch_shapes=[pltpu.VMEM((2 * PEER_HEADS, tm, LANES), BF16)],
        compiler_params=_params("parallel"),
        name="peer_topk",
    )(h2, wq, keys, *_cand_consts())


U32 = jnp.uint32


def _pack_uv_kernel(u_ref, v_ref, o_ref):
    ub = lax.bitcast_convert_type(u_ref[0].astype(BF16).astype(F32), U32)
    vb = lax.bitcast_convert_type(v_ref[0].astype(BF16).astype(F32), U32)
    o_ref[:, 0, :] = ub | (vb >> 16)


def _pack_uv(u, v, layer):
    n, tr = u.shape[1], 512
    spec = pl.BlockSpec((1, tr, D_MODEL), lambda i: (layer, i, 0))
    return pl.pallas_call(
        _pack_uv_kernel, grid=(n // tr,), in_specs=[spec, spec],
        out_specs=pl.BlockSpec((tr, 1, D_MODEL), lambda i: (i, 0, 0)),
        out_shape=jax.ShapeDtypeStruct((n, 1, D_MODEL), U32),
        compiler_params=_params("parallel"), name="pack_uv",
    )(u, v)


PEER_TB = 256
PEER_NBUF = 8


def _peer_mix_kernel(idx_ref, gate_ref, h_ref, x_ref, mod_ref, uv_ref, o_ref, *scratch):
    bufs, sem = scratch[:PEER_NBUF], scratch[PEER_NBUF]
    tb = h_ref.shape[0]
    n_groups = tb // PEER_NBUF

    def issue(t, slot):
        for k in range(PEER_SEL):
            pltpu.make_async_copy(uv_ref.at[idx_ref[k, t]], bufs[slot].at[pl.ds(k, 1), :],
                                  sem.at[slot]).start(priority=k % 2)

    def wait(slot):
        pltpu.make_async_copy(bufs[slot], bufs[slot], sem.at[slot]).wait()

    gate_t = gate_ref[...]
    tok = lax.broadcasted_iota(jnp.int32, (1, tb), 1)
    g2 = mod_ref[0, 5:6, :]

    def token(t, slot, prefetch):
        wait(slot)
        if prefetch:
            issue(t + PEER_NBUF - 1, (slot + PEER_NBUF - 1) % PEER_NBUF)
        x = h_ref[pl.ds(t, 1), :]
        w = bufs[slot][...]
        u = lax.bitcast_convert_type(w & jnp.uint32(0xFFFF0000), F32)
        s = jnp.sum(u * x, axis=-1, keepdims=True)
        g = jnp.sum(jnp.where(tok == t, gate_t, 0.0), axis=-1, keepdims=True)
        act = jax.nn.gelu(s) * g
        v = lax.bitcast_convert_type(w << 16, F32)
        o = jnp.sum(act * v, axis=0, keepdims=True)
        o_ref[pl.ds(t, 1), :] = x_ref[pl.ds(t, 1), :] + g2 * o

    for t0 in range(PEER_NBUF - 1):
        issue(t0, t0)

    def group(g, carry):
        for j in range(PEER_NBUF):
            token(g * PEER_NBUF + j, j, True)
        return carry

    lax.fori_loop(0, n_groups - 1, group, 0)
    for j in range(PEER_NBUF):
        token((n_groups - 1) * PEER_NBUF + j, j, j == 0)


def _peer_mix(idx_t, gates_t, h2, x1, mod, uv, row):
    n = h2.shape[0]
    tb = PEER_TB
    return pl.pallas_call(
        _peer_mix_kernel,
        grid=(n // tb,),
        in_specs=[pl.BlockSpec((PEER_SEL, tb), lambda i: (0, i), memory_space=pltpu.SMEM),
                  pl.BlockSpec((PEER_SEL, tb), lambda i: (0, i)),
                  pl.BlockSpec((tb, D_MODEL), lambda i: (i, 0)),
                  pl.BlockSpec((tb, D_MODEL), lambda i: (i, 0)),
                  pl.BlockSpec((1, 6, D_MODEL), lambda i: (row(i), 0, 0)),
                  pl.BlockSpec(memory_space=pl.ANY)],
        out_specs=pl.BlockSpec((tb, D_MODEL), lambda i: (i, 0)),
        out_shape=jax.ShapeDtypeStruct((n, D_MODEL), F32),
        scratch_shapes=[pltpu.VMEM((PEER_SEL, D_MODEL), U32)] * PEER_NBUF
                       + [pltpu.SemaphoreType.DMA((PEER_NBUF,))],
        compiler_params=_params("arbitrary"),
        name="peer_mix",
    )(idx_t, gates_t, h2, x1, mod, uv)


def _layer_weights(w_in, w_gate, b_gate, w_out, peer_wq, peer_keys):
    o = np.cumsum((0, GLA_HEADS * GLA_DK, GLA_HEADS * GLA_DK, GLA_HEADS * GLA_DV, 2 * GLA_RANK,
                   GLA_HEADS * GLA_DV, ATT_HEADS * HEAD_DIM, ATT_KV_HEADS * HEAD_DIM, ATT_KV_HEADS * HEAD_DIM))
    wq_g = w_in[:, o[0]:o[1]].reshape(D_MODEL, GLA_HEADS, GLA_DK) * (GLA_DK ** -0.5)
    wk_g = w_in[:, o[1]:o[2]].reshape(D_MODEL, GLA_HEADS, GLA_DK)
    w_qk = jnp.concatenate([wq_g, wk_g], axis=-1).reshape(D_MODEL, GLA_HEADS * LANES)
    w_main = jnp.concatenate([w_qk, w_in[:, o[2]:o[3]], w_in[:, o[4]:o[5]], w_in[:, o[5]:o[6]],
                              w_in[:, o[6]:o[7]], w_in[:, o[7]:o[8]]], axis=-1).astype(BF16)
    w_glr = jnp.pad(w_in[:, o[3]:o[4]], ((0, 0), (0, LANES - 2 * GLA_RANK))).astype(BF16)
    wg = w_gate.reshape(2, GLA_RANK, GLA_HEADS, GLA_DK)
    wg = jnp.concatenate([wg, wg], axis=-1).reshape(2, GLA_RANK, GLA_HEADS * LANES)
    wg = jnp.stack([jnp.pad(wg[0], ((0, LANES - GLA_RANK), (0, 0))),
                    jnp.pad(wg[1], ((GLA_RANK, LANES - 2 * GLA_RANK), (0, 0)))]).astype(BF16)
    bg = b_gate.reshape(2, GLA_HEADS, GLA_DK)
    bg = jnp.concatenate([bg, bg], axis=-1).reshape(2, 1, GLA_HEADS * LANES)
    return dict(w_main=w_main, w_glr=w_glr, wg=wg, bg=bg, w_out=w_out.astype(BF16),
                wq=peer_wq.astype(BF16), keys=peer_keys.astype(BF16))


def _state_to_kernel(s):
    st = jnp.swapaxes(s, -1, -2)
    return jnp.pad(st, [(0, 0)] * (st.ndim - 1) + [(0, LANES - GLA_DK)])


def _state_from_kernel(st):
    return jnp.swapaxes(st[..., :GLA_DK], -1, -2)


def kernel(x_prompt, x_sample, cache_k, cache_v, state_gla, c, c_ctx, w_mod, b_mod, norm_mix, norm_ffn,
           w_in, w_gate, b_gate, gla_norm, q_norm, k_norm, w_out, peer_wq, peer_keys, peer_u, peer_v):
    bc, tc, _ = x_prompt.shape
    bl, tl, _ = x_sample.shape
    n_ctx, n_lat = bc * tc, bl * tl
    assert bl + 1 <= MOD_ROWS

    cvec = jnp.concatenate([c_ctx[None, :], c, jnp.zeros((MOD_ROWS - 1 - bl, D_MODEL), F32)], axis=0)
    mod_all = _modulation(cvec, w_mod, b_mod)
    rope_tabs = _rope_tables(tl)
    x = jnp.concatenate([x_prompt.reshape(n_ctx, D_MODEL), x_sample.reshape(n_lat, D_MODEL)], axis=0)
    zero_state = jnp.zeros((bc, 2, GLA_HEADS, LANES, LANES), F32)

    ks, vs, ss = [], [], []
    for l in range(DEPTH):
        w = _layer_weights(w_in[l], w_gate[l], b_gate[l], w_out[l], peer_wq[l], peer_keys[l])
        uv = _pack_uv(peer_u, peer_v, l)
        mod = mod_all[l]
        proj, glr = _proj_in(x, mod, norm_mix[l][None, :], w["w_main"], w["w_glr"], _mod_row(512, n_ctx, tl))

        odir = jnp.zeros((2, n_ctx + n_lat, GLA_HEADS * GLA_DV), F32)
        odir, st_c = _gla(proj, glr, w["wg"], w["bg"], zero_state, odir, row0=0, batch=bc, seq=tc, tile=tc)
        odir, _ = _gla(proj, glr, w["wg"], w["bg"], _state_to_kernel(state_gla[:, l]), odir,
                       row0=n_ctx, batch=bl, seq=tl, tile=512)
        ss.append(_state_from_kernel(st_c))

        qw, kw = q_norm[l][None, :], k_norm[l][None, :]
        qn_c, kn_c, vb_c, ck, cv = _prep(proj, qw, kw, None, row0=0, batch=bc, seq=tc, cache=True)
        qn_l, kn_l, vb_l = _prep(proj, qw, kw, rope_tabs, row0=n_ctx, batch=bl, seq=tl, cache=False)
        ks.append(ck)
        vs.append(cv)
        oa = jnp.zeros((n_ctx + n_lat, ATT_HEADS * HEAD_DIM), BF16)
        oa = _attention(qn_c, kn_c, vb_c, None, oa, row0=0, batch=bc, seq=tc, tq=tc)
        oa = _attention(qn_l, kn_l, vb_l, (cache_k, cache_v, l), oa, row0=n_ctx, batch=bl, seq=tl, tq=256)

        x1, h2 = _out_proj(x, odir, proj, oa, gla_norm[l][None, :], w["w_out"], mod, norm_ffn[l][None, :],
                           _mod_row(256, n_ctx, tl))
        idx, gates = _peer_topk(h2, w["wq"], w["keys"])
        x = _peer_mix(idx, gates, h2, x1, mod, uv, _mod_row(PEER_TB, n_ctx, tl))

    y_prompt = x[:n_ctx].reshape(bc, tc, D_MODEL)
    y_sample = x[n_ctx:].reshape(bl, tl, D_MODEL)
    return (y_prompt, y_sample, jnp.stack(ks, axis=1), jnp.stack(vs, axis=1), jnp.stack(ss, axis=1))
```

```python
import functools

import jax
import jax.numpy as jnp
import numpy as np
from jax import lax
from jax.experimental import pallas as pl
from jax.experimental.pallas import tpu as pltpu

F32 = jnp.float32
BF16 = jnp.bfloat16

D_MODEL = 2048
DEPTH = 2
GRID_W = 64
GLA_HEADS = 8
GLA_DK = 64
GLA_DV = 128
GLA_RANK = 16
GLA_TAU = 16.0
GLA_CHUNK = 64
ATT_HEADS = 8
ATT_KV_HEADS = 2
ATT_GROUP = ATT_HEADS // ATT_KV_HEADS
HEAD_DIM = 128
ROPE_THETA = 10000.0
PEER_HEADS = 8
PEER_QDIM = 256
N_KEYS = 128
PEER_TOPK = 16
PEER_SEL = PEER_HEADS * PEER_TOPK
EPS = 1e-6
LOG2_E = 1.4426950408889634

LANES = 128
MOD_ROWS = 8
VMEM_LIMIT = 56 * 1024 * 1024

COL_QK = 0
COL_V = COL_QK + GLA_HEADS * LANES
COL_OG = COL_V + GLA_HEADS * GLA_DV
COL_QA = COL_OG + GLA_HEADS * GLA_DV
COL_KV = COL_QA + ATT_HEADS * HEAD_DIM
PROJ_COLS = COL_KV + 2 * ATT_KV_HEADS * HEAD_DIM

NT_DIMS = (((1,), (1,)), ((), ()))
TN_DIMS = (((0,), (0,)), ((), ()))


def _params(*sem):
    return pltpu.CompilerParams(dimension_semantics=sem, vmem_limit_bytes=VMEM_LIMIT)


def _dot(a, b):
    return jnp.dot(a, b, preferred_element_type=F32)


def _dot_exact01(a01, x):
    hi = x.astype(BF16)
    r1 = x - hi.astype(F32)
    mid = r1.astype(BF16)
    lo = (r1 - mid.astype(F32)).astype(BF16)
    return _dot(a01, hi) + _dot(a01, mid) + _dot(a01, lo)


def _rms(x):
    return x * lax.rsqrt(jnp.mean(x * x, axis=-1, keepdims=True) + EPS)


def _mod_kernel(cv_ref, w_ref, b_ref, o_ref):
    a = jax.nn.silu(cv_ref[...]).astype(BF16)
    o_ref[0] = _dot(a, w_ref[0].astype(BF16)) + b_ref[0]


def _modulation(cvec, w_mod, b_mod):
    tn = 1536
    n_out = 6 * D_MODEL
    out = pl.pallas_call(
        _mod_kernel,
        grid=(DEPTH, n_out // tn),
        in_specs=[pl.BlockSpec((MOD_ROWS, D_MODEL), lambda l, j: (0, 0)),
                  pl.BlockSpec((1, D_MODEL, tn), lambda l, j: (l, 0, j)),
                  pl.BlockSpec((1, 1, tn), lambda l, j: (l, 0, j))],
        out_specs=pl.BlockSpec((1, MOD_ROWS, tn), lambda l, j: (l, 0, j)),
        out_shape=jax.ShapeDtypeStruct((DEPTH, MOD_ROWS, n_out), F32),
        compiler_params=_params("parallel", "parallel"),
        name="modulation",
    )(cvec, w_mod, b_mod.reshape(DEPTH, 1, n_out))
    return out.reshape(DEPTH, MOD_ROWS, 6, D_MODEL)


def _mod_row(tile_rows, n_ctx, t_lat):
    def row(i):
        start = i * tile_rows
        return jnp.where(start < n_ctx, 0, 1 + (start - n_ctx) // t_lat)
    return row


def _proj_in_kernel(x_ref, mod_ref, nw_ref, w_ref, wg_ref, o_ref, glr_ref, h_scr):
    @pl.when(pl.program_id(1) == 0)
    def _():
        y = _rms(x_ref[...]) * nw_ref[...]
        h = (y * (1.0 + mod_ref[0, 1:2, :]) + mod_ref[0, 0:1, :]).astype(BF16)
        h_scr[...] = h
        glr_ref[...] = _dot(h, wg_ref[...])

    o_ref[...] = _dot(h_scr[...], w_ref[...])


def _proj_in(x, mod, norm_w, w_main, w_glr, row):
    n = x.shape[0]
    tm, tn = 512, 2304
    return pl.pallas_call(
        _proj_in_kernel,
        grid=(n // tm, PROJ_COLS // tn),
        in_specs=[pl.BlockSpec((tm, D_MODEL), lambda i, j: (i, 0)),
                  pl.BlockSpec((1, 6, D_MODEL), lambda i, j: (row(i), 0, 0)),
                  pl.BlockSpec((1, D_MODEL), lambda i, j: (0, 0)),
                  pl.BlockSpec((D_MODEL, tn), lambda i, j: (0, j)),
                  pl.BlockSpec((D_MODEL, LANES), lambda i, j: (0, 0))],
        out_specs=[pl.BlockSpec((tm, tn), lambda i, j: (i, j)),
                   pl.BlockSpec((tm, LANES), lambda i, j: (i, 0))],
        out_shape=[jax.ShapeDtypeStruct((n, PROJ_COLS), F32),
                   jax.ShapeDtypeStruct((n, LANES), F32)],
        scratch_shapes=[pltpu.VMEM((tm, D_MODEL), BF16)],
        compiler_params=_params("parallel", "arbitrary"),
        name="proj_in",
    )(x, mod, norm_w, w_main, w_glr)


GLA_HB = 8


def _gla_kernel(qk_ref, v_ref, glr_ref, wg_ref, bg_ref, swap_ref, mask_ref, init_ref, out_hbm_ref, o_ref, st_ref,
                st_scr, x1_scr, k1_scr, x2_scr, k2_scr, vb_scr, dec_scr, upd_scr, sp_scr, *, n_chunks):
    del out_hbm_ref
    d = pl.program_id(2)
    t = pl.program_id(3)
    fwd = d == 0

    @pl.when(t == 0)
    def _():
        st_scr[...] = init_ref[0, 0]

    c = GLA_CHUNK
    hw = GLA_HB * LANES
    mask = mask_ref[0]
    qmask = lax.broadcasted_iota(jnp.int32, (1, hw), 1) % LANES < GLA_DK

    z = _dot(glr_ref[...].astype(BF16), wg_ref[0]) + bg_ref[0]
    la = jax.nn.log_sigmoid(z) / GLA_TAU
    blk = 2 * c
    m01 = mask[0:blk, 0:blk].astype(BF16)
    b_all = jnp.concatenate([_dot_exact01(m01, la[r:r + blk, :]) for r in range(0, n_chunks * c, blk)], axis=0)
    vb_scr[...] = v_ref[...].astype(BF16)
    for k in range(n_chunks):
        rows = slice(k * c, (k + 1) * c)
        b = b_all[rows, :]
        b_mid = jnp.where(fwd, b[c // 2 - 1:c // 2, :], b[c // 2:c // 2 + 1, :])
        b_end = jnp.where(fwd, b[c - 1:c, :], b[0:1, :])
        qk = qk_ref[rows, :]
        x1_scr[rows, :] = (qk * jnp.exp(jnp.where(qmask, b - b_mid, b_mid - b))).astype(BF16)
        x2_scr[rows, :] = (qk * jnp.exp(jnp.where(qmask, b, b_end - b))).astype(BF16)
        dec_scr[k] = jnp.exp(b_end)
    swap = swap_ref[...]
    for h in range(GLA_HB):
        hl = slice(h * LANES, (h + 1) * LANES)
        k1_scr[:, hl] = _dot(x1_scr[:, hl], swap).astype(BF16)
        k2_scr[:, hl] = _dot(x2_scr[:, hl], swap).astype(BF16)

    for h in range(GLA_HB):
        hl = slice(h * LANES, (h + 1) * LANES)
        s = lax.dot_general(x1_scr[:, hl], k1_scr[:, hl], NT_DIMS, preferred_element_type=F32)
        s = jnp.where(mask > 0.0, s, 0.0).astype(BF16)
        o_ref[0, :, hl] = _dot(s, vb_scr[:, hl])

    for k in range(n_chunks):
        rows = slice(k * c, (k + 1) * c)
        for h in range(GLA_HB):
            hl = slice(h * LANES, (h + 1) * LANES)
            upd_scr[k, h] = lax.dot_general(vb_scr[rows, hl], k2_scr[rows, hl], TN_DIMS,
                                            preferred_element_type=F32)

    def chunk(step, carry):
        k = jnp.where(fwd, step, n_chunks - 1 - step)
        dec = dec_scr[k]
        for h in range(GLA_HB):
            st = st_scr[h]
            sp_scr[k, h] = st.astype(BF16)
            st_scr[h] = st * dec[:, h * LANES:(h + 1) * LANES] + upd_scr[k, h]
        return carry

    lax.fori_loop(0, n_chunks, chunk, 0)

    for k in range(n_chunks):
        rows = slice(k * c, (k + 1) * c)
        for h in range(GLA_HB):
            hl = slice(h * LANES, (h + 1) * LANES)
            o_ref[0, rows, hl] += lax.dot_general(x2_scr[rows, hl], sp_scr[k, h], NT_DIMS,
                                                  preferred_element_type=F32)

    @pl.when(t == pl.num_programs(3) - 1)
    def _():
        st_ref[0, 0] = st_scr[...]


def _gla(proj, glr, wg, bg, init_t, out, *, row0, batch, seq, tile):
    nt = seq // tile
    rb0 = row0 // tile
    hw = GLA_HB * LANES
    n_chunks = tile // GLA_CHUNK
    l = np.arange(LANES)
    swap = ((l[:, None] == l[None, :] + GLA_DK) & (l[None, :] < GLA_DK)).astype(np.float32)
    r = np.arange(tile)
    same = r[:, None] // GLA_CHUNK == r[None, :] // GLA_CHUNK
    mask = np.stack([same & (r[:, None] >= r[None, :]), same & (r[:, None] <= r[None, :])]).astype(np.float32)

    def rblk(b, t, d):
        return rb0 + b * nt + jnp.where(d == 0, t, nt - 1 - t)

    in_specs = [pl.BlockSpec((tile, hw), lambda b, g, d, t: (rblk(b, t, d), COL_QK // hw + g)),
                pl.BlockSpec((tile, hw), lambda b, g, d, t: (rblk(b, t, d), COL_V // hw + g)),
                pl.BlockSpec((tile, LANES), lambda b, g, d, t: (rblk(b, t, d), 0)),
                pl.BlockSpec((1, LANES, hw), lambda b, g, d, t: (d, 0, g)),
                pl.BlockSpec((1, 1, hw), lambda b, g, d, t: (d, 0, g)),
                pl.BlockSpec((LANES, LANES), lambda b, g, d, t: (0, 0)),
                pl.BlockSpec((1, tile, tile), lambda b, g, d, t: (d, 0, 0)),
                pl.BlockSpec((1, 1, GLA_HB, LANES, LANES), lambda b, g, d, t: (b, d, g, 0, 0)),
                pl.BlockSpec(memory_space=pl.ANY)]
    args = [proj, proj, glr, wg, bg, jnp.asarray(swap, BF16), jnp.asarray(mask), init_t, out]
    return pl.pallas_call(
        functools.partial(_gla_kernel, n_chunks=n_chunks),
        grid=(batch, GLA_HEADS // GLA_HB, 2, nt),
        in_specs=in_specs,
        out_specs=[pl.BlockSpec((1, tile, hw), lambda b, g, d, t: (d, rblk(b, t, d), g)),
                   pl.BlockSpec((1, 1, GLA_HB, LANES, LANES), lambda b, g, d, t: (b, d, g, 0, 0))],
        out_shape=[jax.ShapeDtypeStruct(out.shape, out.dtype),
                   jax.ShapeDtypeStruct((batch, 2, GLA_HEADS, LANES, LANES), F32)],
        scratch_shapes=[pltpu.VMEM((GLA_HB, LANES, LANES), F32)]
                       + [pltpu.VMEM((tile, hw), BF16)] * 5
                       + [pltpu.VMEM((n_chunks, 1, hw), F32),
                          pltpu.VMEM((n_chunks, GLA_HB, LANES, LANES), F32),
                          pltpu.VMEM((n_chunks, GLA_HB, LANES, LANES), BF16)],
        input_output_aliases={len(args) - 1: 0},
        compiler_params=_params("parallel", "parallel", "arbitrary", "arbitrary"),
        name="gla",
    )(*args)


def _prep_kernel(*refs, rope, cache):
    qa_ref, kv_ref, qw_ref, kw_ref = refs[:4]
    refs = refs[4:]
    if rope:
        cos_ref, sa_ref, sb_ref = refs[:3]
        refs = refs[3:]
    qn_ref, kn_ref, vb_ref = refs[:3]
    if cache:
        ck_ref, cv_ref = refs[3:5]

    def rot(y):
        if not rope:
            return y
        return (y * cos_ref[...] + pltpu.roll(y, LANES - 32, 1) * sa_ref[...]
                + pltpu.roll(y, 32, 1) * sb_ref[...])

    for h in range(ATT_HEADS):
        hl = slice(h * HEAD_DIM, (h + 1) * HEAD_DIM)
        qn_ref[:, hl] = rot(_rms(qa_ref[:, hl]) * qw_ref[...]).astype(BF16)
    for h in range(ATT_KV_HEADS):
        hl = slice(h * HEAD_DIM, (h + 1) * HEAD_DIM)
        vl = slice((ATT_KV_HEADS + h) * HEAD_DIM, (ATT_KV_HEADS + h + 1) * HEAD_DIM)
        kn = _rms(kv_ref[:, hl]) * kw_ref[...]
        v = kv_ref[:, vl]
        kn_ref[:, hl] = rot(kn).astype(BF16)
        vb_ref[:, hl] = v.astype(BF16)
        if cache:
            ck_ref[0, h] = kn
            cv_ref[0, h] = v


def _prep(proj, qw, kw, rope_tabs, *, row0, batch, seq, cache):
    tm = 256
    n = batch * seq
    rb0 = row0 // tm
    per_req = seq // tm
    kvw = ATT_KV_HEADS * HEAD_DIM
    rope = rope_tabs is not None
    in_specs = [pl.BlockSpec((tm, ATT_HEADS * HEAD_DIM), lambda i: (rb0 + i, COL_QA // (ATT_HEADS * HEAD_DIM))),
                pl.BlockSpec((tm, 2 * kvw), lambda i: (rb0 + i, COL_KV // (2 * kvw))),
                pl.BlockSpec((1, HEAD_DIM), lambda i: (0, 0)),
                pl.BlockSpec((1, HEAD_DIM), lambda i: (0, 0))]
    args = [proj, proj, qw, kw]
    if rope:
        in_specs += [pl.BlockSpec((tm, HEAD_DIM), lambda i: (i % per_req, 0))] * 3
        args += list(rope_tabs)
    out_specs = [pl.BlockSpec((tm, ATT_HEADS * HEAD_DIM), lambda i: (i, 0)),
                 pl.BlockSpec((tm, kvw), lambda i: (i, 0)),
                 pl.BlockSpec((tm, kvw), lambda i: (i, 0))]
    out_shape = [jax.ShapeDtypeStruct((n, ATT_HEADS * HEAD_DIM), BF16),
                 jax.ShapeDtypeStruct((n, kvw), BF16),
                 jax.ShapeDtypeStruct((n, kvw), BF16)]
    if cache:
        assert seq == tm
        out_specs += [pl.BlockSpec((1, ATT_KV_HEADS, seq, HEAD_DIM), lambda i: (i, 0, 0, 0))] * 2
        out_shape += [jax.ShapeDtypeStruct((batch, ATT_KV_HEADS, seq, HEAD_DIM), F32)] * 2
    return pl.pallas_call(
        functools.partial(_prep_kernel, rope=rope, cache=cache),
        grid=(n // tm,),
        in_specs=in_specs, out_specs=out_specs, out_shape=out_shape,
        compiler_params=_params("parallel"),
        name="attn_prep",
    )(*args)


def _rope_tables(seq):
    half = HEAD_DIM // 2
    tok = jnp.arange(seq)
    row = (tok // GRID_W).astype(F32)
    col = (tok % GRID_W).astype(F32)
    inv_freq = ROPE_THETA ** (-jnp.arange(0, half, 2, dtype=F32) / half)
    ang_r = row[:, None] * inv_freq[None, :]
    ang_c = col[:, None] * inv_freq[None, :]
    ang = jnp.concatenate([ang_r, ang_r, ang_c, ang_c], axis=-1)
    first = (jnp.arange(HEAD_DIM) % half) < half // 2
    sin = jnp.sin(ang)
    return jnp.cos(ang), jnp.where(first, -sin, 0.0), jnp.where(first, 0.0, sin)


ATT_COL_BLOCK = 512

def _attn_kernel(*refs, ctx):
    o_ref, s_scr = refs[-2], refs[-1]
    refs = refs[:-3]
    if ctx:
        q_ref, k_ref, v_ref, ck_ref, cv_ref = refs
        ck = ck_ref[0, 0, 0].astype(BF16)
        cv = cv_ref[0, 0, 0].astype(BF16)
    else:
        q_ref, k_ref, v_ref = refs
    k = k_ref[...]
    v = v_ref[...]
    c2 = (HEAD_DIM ** -0.5) * LOG2_E

    n_keys = k_ref.shape[0]
    blocks = [(k, v, r, min(ATT_COL_BLOCK, n_keys - r), r) for r in range(0, n_keys, ATT_COL_BLOCK)]
    if ctx:
        past = ck.shape[0]
        blocks += [(ck, cv, r, min(ATT_COL_BLOCK, past - r), n_keys + r) for r in range(0, past, ATT_COL_BLOCK)]

    def scores(g):
        q = q_ref[:, g * HEAD_DIM:(g + 1) * HEAD_DIM]
        for kk, _, r, w, col in blocks:
            s_scr[g % 2, :, col:col + w] = lax.dot_general(q, kk[r:r + w, :], NT_DIMS, preferred_element_type=F32)

    def lane_fold(x, op):
        parts = [x[:, i:i + LANES] for i in range(0, x.shape[1], LANES)]
        while len(parts) > 1:
            parts = [op(parts[i], parts[i + 1]) for i in range(0, len(parts) - 1, 2)] + parts[len(parts) & ~1:]
        return parts[0]

    def finish(g):
        sg = s_scr.at[g % 2]
        mp = None
        for _, _, _, w, col in blocks:
            part = lane_fold(sg[:, col:col + w], jnp.maximum)
            mp = part if mp is None else jnp.maximum(mp, part)
        mc = jnp.max(mp, axis=-1, keepdims=True) * c2
        lp = jnp.zeros((q_ref.shape[0], LANES), F32)
        acc = jnp.zeros((q_ref.shape[0], HEAD_DIM), F32)
        for _, vv, r, w, col in blocks:
            p = jnp.exp2(sg[:, col:col + w] * c2 - mc)
            lp = lp + lane_fold(p, jnp.add)
            acc = acc + _dot(p.astype(BF16), vv[r:r + w, :])
        l = jnp.sum(lp, axis=-1, keepdims=True)
        o_ref[:, g * HEAD_DIM:(g + 1) * HEAD_DIM] = (acc / l).astype(BF16)

    scores(0)
    for g in range(ATT_GROUP):
        if g + 1 < ATT_GROUP:
            scores(g + 1)
        finish(g)


def _attention(qn, kn, vb, ctx_kv, out, *, row0, batch, seq, tq):
    nq = seq // tq
    ob0 = row0 // tq
    past = ctx_kv[0].shape[3] if ctx_kv is not None else 0
    assert seq % LANES == 0 and past % LANES == 0
    gw = ATT_GROUP * HEAD_DIM
    in_specs = [pl.BlockSpec((tq, gw), lambda b, h, t: (b * nq + t, h)),
                pl.BlockSpec((seq, HEAD_DIM), lambda b, h, t: (b, h)),
                pl.BlockSpec((seq, HEAD_DIM), lambda b, h, t: (b, h))]
    args = [qn, kn, vb]
    if ctx_kv is not None:
        ck, cv, layer = ctx_kv
        spec = pl.BlockSpec((1, 1, 1, past, HEAD_DIM), lambda b, h, t: (b, layer, h, 0, 0))
        in_specs += [spec, spec]
        args += [ck, cv]
    in_specs.append(pl.BlockSpec(memory_space=pl.ANY))
    args.append(out)
    return pl.pallas_call(
        functools.partial(_attn_kernel, ctx=ctx_kv is not None),
        grid=(batch, ATT_KV_HEADS, nq),
        in_specs=in_specs,
        out_specs=pl.BlockSpec((tq, gw), lambda b, h, t: (ob0 + b * nq + t, h)),
        out_shape=jax.ShapeDtypeStruct(out.shape, out.dtype),
        scratch_shapes=[pltpu.VMEM((2, tq, seq + past), F32)],
        input_output_aliases={len(args) - 1: 0},
        compiler_params=_params("parallel", "parallel", "arbitrary"),
        name="attention",
    )(*args)


def _out_proj_kernel(x_ref, od_ref, og_ref, oa_ref, gn_ref, w_ref, mod_ref, nw_ref, x1_ref, h2_ref):
    o = od_ref[0] + od_ref[1]
    parts = []
    for h in range(GLA_HEADS):
        hl = slice(h * GLA_DV, (h + 1) * GLA_DV)
        parts.append((_rms(o[:, hl]) * gn_ref[...] * jax.nn.silu(og_ref[:, hl])).astype(BF16))
    og = jnp.concatenate(parts, axis=-1)
    gw = GLA_HEADS * GLA_DV
    mix = _dot(og, w_ref[0:gw, :]) + _dot(oa_ref[...], w_ref[gw:, :])
    x1 = x_ref[...] + mod_ref[0, 2:3, :] * mix
    x1_ref[...] = x1
    h2_ref[...] = _rms(x1) * nw_ref[...] * (1.0 + mod_ref[0, 4:5, :]) + mod_ref[0, 3:4, :]


def _out_proj(x, odir, proj, oa, gla_norm, w_out, mod, norm_w, row):
    n = x.shape[0]
    tm = 512
    gw = GLA_HEADS * GLA_DV
    return pl.pallas_call(
        _out_proj_kernel,
        grid=(n // tm,),
        in_specs=[pl.BlockSpec((tm, D_MODEL), lambda i: (i, 0)),
                  pl.BlockSpec((2, tm, gw), lambda i: (0, i, 0)),
                  pl.BlockSpec((tm, gw), lambda i: (i, COL_OG // gw)),
                  pl.BlockSpec((tm, ATT_HEADS * HEAD_DIM), lambda i: (i, 0)),
                  pl.BlockSpec((1, GLA_DV), lambda i: (0, 0)),
                  pl.BlockSpec((gw + ATT_HEADS * HEAD_DIM, D_MODEL), lambda i: (0, 0)),
                  pl.BlockSpec((1, 6, D_MODEL), lambda i: (row(i), 0, 0)),
                  pl.BlockSpec((1, D_MODEL), lambda i: (0, 0))],
        out_specs=[pl.BlockSpec((tm, D_MODEL), lambda i: (i, 0)),
                   pl.BlockSpec((tm, D_MODEL), lambda i: (i, 0))],
        out_shape=[jax.ShapeDtypeStruct((n, D_MODEL), F32),
                   jax.ShapeDtypeStruct((n, D_MODEL), F32)],
        compiler_params=_params("parallel"),
        name="out_proj",
    )(x, odir, proj, oa, gla_norm, w_out, mod, norm_w)


NEG_INF = float("-inf")
N_CAND = PEER_TOPK * PEER_TOPK


_CAND_GROUPS = ((0, 0), (0, 8), (1, 0), (2, 0), (3, 0), (4, 0), (5, 0), (6, 0), (7, 0))
N_CAND_ROWS = 8 * (len(_CAND_GROUPS) + 1)


def _cand_consts():
    pos = np.zeros((N_CAND_ROWS, 1), np.float32)
    off = np.zeros((N_CAND_ROWS, 1), np.float32)
    for g, (a, b0) in enumerate(_CAND_GROUPS):
        for j in range(8):
            pos[8 * g + j] = a * PEER_TOPK + b0 + j
            off[8 * g + j] = 0.0 if (a + 1) * (b0 + j + 1) <= PEER_TOPK else NEG_INF
    for j in range(8):
        pos[N_CAND_ROWS - 8 + j] = (8 + j) * PEER_TOPK
    return (jnp.asarray(np.broadcast_to(pos, (N_CAND_ROWS, LANES))),
            jnp.asarray(np.broadcast_to(off, (N_CAND_ROWS, LANES))))


def _pair_rows(first, second):
    rows = [first[a:a + 1, :] + second[b0:b0 + 8, :] for a, b0 in _CAND_GROUPS]
    rows.append(first[8:16, :] + second[0:1, :])
    return jnp.concatenate(rows, axis=0)


def _top16_rows(s, row_id):
    rank = lax.broadcasted_iota(jnp.int32, (PEER_TOPK, s.shape[1]), 0)
    vals = jnp.zeros((PEER_TOPK, s.shape[1]), F32)
    ids = jnp.zeros((PEER_TOPK, s.shape[1]), F32)
    for r in range(PEER_TOPK):
        m = jnp.max(s, axis=0, keepdims=True)
        i = jnp.min(jnp.where(s == m, row_id, float(N_CAND)), axis=0, keepdims=True)
        s = jnp.where(row_id == i, NEG_INF, s)
        vals = jnp.where(rank == r, m, vals)
        ids = jnp.where(rank == r, i, ids)
    return vals, ids


def _peer_topk_kernel(h_ref, wq_ref, keys_ref, pos_ref, off_ref, idx_ref, gate_ref, q_scr):
    tm = h_ref.shape[0]
    q = _dot(h_ref[...].astype(BF16), wq_ref[...]).astype(BF16)
    for a in range(2 * PEER_HEADS):
        q_scr[a] = q[:, a * LANES:(a + 1) * LANES]
    key_id = lax.broadcasted_iota(jnp.int32, (N_KEYS, LANES), 0).astype(F32)
    rank = lax.broadcasted_iota(jnp.int32, (PEER_TOPK, LANES), 0)
    pos = pos_ref[...]
    off = off_ref[...]

    def head(h, carry):
        out_rows = pl.ds(pl.multiple_of(h * PEER_TOPK, PEER_TOPK), PEER_TOPK)
        for c in range(tm // LANES):
            cols = slice(c * LANES, (c + 1) * LANES)
            s1 = lax.dot_general(keys_ref[h, 0], q_scr[2 * h, cols, :], NT_DIMS, preferred_element_type=F32)
            s2 = lax.dot_general(keys_ref[h, 1], q_scr[2 * h + 1, cols, :], NT_DIMS, preferred_element_type=F32)
            v1, i1 = _top16_rows(s1, key_id)
            v2, i2 = _top16_rows(s2, key_id)
            cand = _pair_rows(v1, v2) + off
            cidx = _pair_rows(i1 * float(N_KEYS), i2)
            best = jnp.zeros((PEER_TOPK, LANES), F32)
            eidx = jnp.zeros((PEER_TOPK, LANES), F32)
            for r in range(PEER_TOPK):
                m = jnp.max(cand, axis=0, keepdims=True)
                p = jnp.min(jnp.where(cand == m, pos, float(N_CAND)), axis=0, keepdims=True)
                sel = pos == p
                e = jnp.max(jnp.where(sel, cidx, -1.0), axis=0, keepdims=True)
                cand = jnp.where(sel, NEG_INF, cand)
                best = jnp.where(rank == r, m, best)
                eidx = jnp.where(rank == r, e, eidx)
            ex = jnp.exp(best - best[0:1, :])
            gate_ref[out_rows, cols] = ex / jnp.sum(ex, axis=0, keepdims=True)
            idx_ref[out_rows, cols] = eidx.astype(jnp.int32)
        return carry

    lax.fori_loop(0, PEER_HEADS, head, 0)


def _peer_topk(h2, wq, keys):
    n = h2.shape[0]
    tm = 1024
    qw = PEER_HEADS * PEER_QDIM
    full = lambda shape: pl.BlockSpec(shape, lambda i: (0,) * len(shape))
    return pl.pallas_call(
        _peer_topk_kernel,
        grid=(n // tm,),
        in_specs=[pl.BlockSpec((tm, D_MODEL), lambda i: (i, 0)),
                  full((D_MODEL, qw)),
                  full((PEER_HEADS, 2, N_KEYS, PEER_QDIM // 2)),
                  full((N_CAND_ROWS, LANES)), full((N_CAND_ROWS, LANES))],
        out_specs=[pl.BlockSpec((PEER_SEL, tm), lambda i: (0, i)),
                   pl.BlockSpec((PEER_SEL, tm), lambda i: (0, i))],
        out_shape=[jax.ShapeDtypeStruct((PEER_SEL, n), jnp.int32),
                   jax.ShapeDtypeStruct((PEER_SEL, n), F32)],
        scratch_shapes=[pltpu.VMEM((2 * PEER_HEADS, tm, LANES), BF16)],
        compiler_params=_params("parallel"),
        name="peer_topk",
    )(h2, wq, keys, *_cand_consts())


U32 = jnp.uint32


def _pack_uv_kernel(u_ref, v_ref, o_ref):
    ub = lax.bitcast_convert_type(u_ref[0].astype(BF16).astype(F32), U32)
    vb = lax.bitcast_convert_type(v_ref[0].astype(BF16).astype(F32), U32)
    o_ref[:, 0, :] = ub | (vb >> 16)


def _pack_uv(u, v, layer):
    n, tr = u.shape[1], 512
    spec = pl.BlockSpec((1, tr, D_MODEL), lambda i: (layer, i, 0))
    return pl.pallas_call(
        _pack_uv_kernel, grid=(n // tr,), in_specs=[spec, spec],
        out_specs=pl.BlockSpec((tr, 1, D_MODEL), lambda i: (i, 0, 0)),
        out_shape=jax.ShapeDtypeStruct((n, 1, D_MODEL), U32),
        compiler_params=_params("parallel"), name="pack_uv",
    )(u, v)


PEER_TB = 256
PEER_NBUF = 8


def _peer_mix_kernel(idx_ref, gate_ref, h_ref, x_ref, mod_ref, uv_ref, o_ref, *scratch):
    bufs, sem = scratch[:PEER_NBUF], scratch[PEER_NBUF]
    tb = h_ref.shape[0]
    n_groups = tb // PEER_NBUF

    def issue(t, slot):
        for k in range(PEER_SEL):
            pltpu.make_async_copy(uv_ref.at[idx_ref[k, t]], bufs[slot].at[pl.ds(k, 1), :],
                                  sem.at[slot]).start(priority=k % 2)

    def wait(slot):
        pltpu.make_async_copy(bufs[slot], bufs[slot], sem.at[slot]).wait()

    gate_t = gate_ref[...]
    tok = lax.broadcasted_iota(jnp.int32, (1, tb), 1)
    g2 = mod_ref[0, 5:6, :]

    def token(t, slot, prefetch):
        wait(slot)
        if prefetch:
            issue(t + PEER_NBUF - 1, (slot + PEER_NBUF - 1) % PEER_NBUF)
        x = h_ref[pl.ds(t, 1), :]
        w = bufs[slot][...]
        u = lax.bitcast_convert_type(w & jnp.uint32(0xFFFF0000), F32)
        s = jnp.sum(u * x, axis=-1, keepdims=True)
        g = jnp.sum(jnp.where(tok == t, gate_t, 0.0), axis=-1, keepdims=True)
        act = jax.nn.gelu(s) * g
        v = lax.bitcast_convert_type(w << 16, F32)
        o = jnp.sum(act * v, axis=0, keepdims=True)
        o_ref[pl.ds(t, 1), :] = x_ref[pl.ds(t, 1), :] + g2 * o

    for t0 in range(PEER_NBUF - 1):
        issue(t0, t0)

    def group(g, carry):
        for j in range(PEER_NBUF):
            token(g * PEER_NBUF + j, j, True)
        return carry

    lax.fori_loop(0, n_groups - 1, group, 0)
    for j in range(PEER_NBUF):
        token((n_groups - 1) * PEER_NBUF + j, j, j == 0)


def _peer_mix(idx_t, gates_t, h2, x1, mod, uv, row):
    n = h2.shape[0]
    tb = PEER_TB
    return pl.pallas_call(
        _peer_mix_kernel,
        grid=(n // tb,),
        in_specs=[pl.BlockSpec((PEER_SEL, tb), lambda i: (0, i), memory_space=pltpu.SMEM),
                  pl.BlockSpec((PEER_SEL, tb), lambda i: (0, i)),
                  pl.BlockSpec((tb, D_MODEL), lambda i: (i, 0)),
                  pl.BlockSpec((tb, D_MODEL), lambda i: (i, 0)),
                  pl.BlockSpec((1, 6, D_MODEL), lambda i: (row(i), 0, 0)),
                  pl.BlockSpec(memory_space=pl.ANY)],
        out_specs=pl.BlockSpec((tb, D_MODEL), lambda i: (i, 0)),
        out_shape=jax.ShapeDtypeStruct((n, D_MODEL), F32),
        scratch_shapes=[pltpu.VMEM((PEER_SEL, D_MODEL), U32)] * PEER_NBUF
                       + [pltpu.SemaphoreType.DMA((PEER_NBUF,))],
        compiler_params=_params("arbitrary"),
        name="peer_mix",
    )(idx_t, gates_t, h2, x1, mod, uv)


def _layer_weights(w_in, w_gate, b_gate, w_out, peer_wq, peer_keys):
    o = np.cumsum((0, GLA_HEADS * GLA_DK, GLA_HEADS * GLA_DK, GLA_HEADS * GLA_DV, 2 * GLA_RANK,
                   GLA_HEADS * GLA_DV, ATT_HEADS * HEAD_DIM, ATT_KV_HEADS * HEAD_DIM, ATT_KV_HEADS * HEAD_DIM))
    wq_g = w_in[:, o[0]:o[1]].reshape(D_MODEL, GLA_HEADS, GLA_DK) * (GLA_DK ** -0.5)
    wk_g = w_in[:, o[1]:o[2]].reshape(D_MODEL, GLA_HEADS, GLA_DK)
    w_qk = jnp.concatenate([wq_g, wk_g], axis=-1).reshape(D_MODEL, GLA_HEADS * LANES)
    w_main = jnp.concatenate([w_qk, w_in[:, o[2]:o[3]], w_in[:, o[4]:o[5]], w_in[:, o[5]:o[6]],
                              w_in[:, o[6]:o[7]], w_in[:, o[7]:o[8]]], axis=-1).astype(BF16)
    w_glr = jnp.pad(w_in[:, o[3]:o[4]], ((0, 0), (0, LANES - 2 * GLA_RANK))).astype(BF16)
    wg = w_gate.reshape(2, GLA_RANK, GLA_HEADS, GLA_DK)
    wg = jnp.concatenate([wg, wg], axis=-1).reshape(2, GLA_RANK, GLA_HEADS * LANES)
    wg = jnp.stack([jnp.pad(wg[0], ((0, LANES - GLA_RANK), (0, 0))),
                    jnp.pad(wg[1], ((GLA_RANK, LANES - 2 * GLA_RANK), (0, 0)))]).astype(BF16)
    bg = b_gate.reshape(2, GLA_HEADS, GLA_DK)
    bg = jnp.concatenate([bg, bg], axis=-1).reshape(2, 1, GLA_HEADS * LANES)
    return dict(w_main=w_main, w_glr=w_glr, wg=wg, bg=bg, w_out=w_out.astype(BF16),
                wq=peer_wq.astype(BF16), keys=peer_keys.astype(BF16))


def _state_to_kernel(s):
    st = jnp.swapaxes(s, -1, -2)
    return jnp.pad(st, [(0, 0)] * (st.ndim - 1) + [(0, LANES - GLA_DK)])


def _state_from_kernel(st):
    return jnp.swapaxes(st[..., :GLA_DK], -1, -2)


def kernel(x_prompt, x_sample, cache_k, cache_v, state_gla, c, c_ctx, w_mod, b_mod, norm_mix, norm_ffn,
           w_in, w_gate, b_gate, gla_norm, q_norm, k_norm, w_out, peer_wq, peer_keys, peer_u, peer_v):
    bc, tc, _ = x_prompt.shape
    bl, tl, _ = x_sample.shape
    n_ctx, n_lat = bc * tc, bl * tl
    assert bl + 1 <= MOD_ROWS

    cvec = jnp.concatenate([c_ctx[None, :], c, jnp.zeros((MOD_ROWS - 1 - bl, D_MODEL), F32)], axis=0)
    mod_all = _modulation(cvec, w_mod, b_mod)
    rope_tabs = _rope_tables(tl)
    x = jnp.concatenate([x_prompt.reshape(n_ctx, D_MODEL), x_sample.reshape(n_lat, D_MODEL)], axis=0)
    zero_state = jnp.zeros((bc, 2, GLA_HEADS, LANES, LANES), F32)

    ks, vs, ss = [], [], []
    for l in range(DEPTH):
        w = _layer_weights(w_in[l], w_gate[l], b_gate[l], w_out[l], peer_wq[l], peer_keys[l])
        uv = _pack_uv(peer_u, peer_v, l)
        mod = mod_all[l]
        proj, glr = _proj_in(x, mod, norm_mix[l][None, :], w["w_main"], w["w_glr"], _mod_row(512, n_ctx, tl))

        odir = jnp.zeros((2, n_ctx + n_lat, GLA_HEADS * GLA_DV), F32)
        odir, st_c = _gla(proj, glr, w["wg"], w["bg"], zero_state, odir, row0=0, batch=bc, seq=tc, tile=tc)
        odir, _ = _gla(proj, glr, w["wg"], w["bg"], _state_to_kernel(state_gla[:, l]), odir,
                       row0=n_ctx, batch=bl, seq=tl, tile=512)
        ss.append(_state_from_kernel(st_c))

        qw, kw = q_norm[l][None, :], k_norm[l][None, :]
        qn_c, kn_c, vb_c, ck, cv = _prep(proj, qw, kw, None, row0=0, batch=bc, seq=tc, cache=True)
        qn_l, kn_l, vb_l = _prep(proj, qw, kw, rope_tabs, row0=n_ctx, batch=bl, seq=tl, cache=False)
        ks.append(ck)
        vs.append(cv)
        oa = jnp.zeros((n_ctx + n_lat, ATT_HEADS * HEAD_DIM), BF16)
        oa = _attention(qn_c, kn_c, vb_c, None, oa, row0=0, batch=bc, seq=tc, tq=tc)
        oa = _attention(qn_l, kn_l, vb_l, (cache_k, cache_v, l), oa, row0=n_ctx, batch=bl, seq=tl, tq=256)

        x1, h2 = _out_proj(x, odir, proj, oa, gla_norm[l][None, :], w["w_out"], mod, norm_ffn[l][None, :],
                           _mod_row(512, n_ctx, tl))
        idx, gates = _peer_topk(h2, w["wq"], w["keys"])
        x = _peer_mix(idx, gates, h2, x1, mod, uv, _mod_row(PEER_TB, n_ctx, tl))

    y_prompt = x[:n_ctx].reshape(bc, tc, D_MODEL)
    y_sample = x[n_ctx:].reshape(bl, tl, D_MODEL)
    return (y_prompt, y_sample, jnp.stack(ks, axis=1), jnp.stack(vs, axis=1), jnp.stack(ss, axis=1))
```

```python
import functools

import jax
import jax.numpy as jnp
import numpy as np
from jax import lax
from jax.experimental import pallas as pl
from jax.experimental.pallas import tpu as pltpu

F32 = jnp.float32
BF16 = jnp.bfloat16

D_MODEL = 2048
DEPTH = 2
GRID_W = 64
GLA_HEADS = 8
GLA_DK = 64
GLA_DV = 128
GLA_RANK = 16
GLA_TAU = 16.0
GLA_CHUNK = 64
ATT_HEADS = 8
ATT_KV_HEADS = 2
ATT_GROUP = ATT_HEADS // ATT_KV_HEADS
HEAD_DIM = 128
ROPE_THETA = 10000.0
PEER_HEADS = 8
PEER_QDIM = 256
N_KEYS = 128
PEER_TOPK = 16
PEER_SEL = PEER_HEADS * PEER_TOPK
EPS = 1e-6
LOG2_E = 1.4426950408889634

LANES = 128
MOD_ROWS = 8
VMEM_LIMIT = 56 * 1024 * 1024

COL_QK = 0
COL_V = COL_QK + GLA_HEADS * LANES
COL_OG = COL_V + GLA_HEADS * GLA_DV
COL_QA = COL_OG + GLA_HEADS * GLA_DV
COL_KV = COL_QA + ATT_HEADS * HEAD_DIM
PROJ_COLS = COL_KV + 2 * ATT_KV_HEADS * HEAD_DIM

NT_DIMS = (((1,), (1,)), ((), ()))
TN_DIMS = (((0,), (0,)), ((), ()))


def _params(*sem):
    return pltpu.CompilerParams(dimension_semantics=sem, vmem_limit_bytes=VMEM_LIMIT)


def _dot(a, b):
    return jnp.dot(a, b, preferred_element_type=F32)


def _dot_exact01(a01, x):
    hi = x.astype(BF16)
    r1 = x - hi.astype(F32)
    mid = r1.astype(BF16)
    lo = (r1 - mid.astype(F32)).astype(BF16)
    return _dot(a01, hi) + _dot(a01, mid) + _dot(a01, lo)


def _rms(x):
    return x * lax.rsqrt(jnp.mean(x * x, axis=-1, keepdims=True) + EPS)


def _mod_kernel(cv_ref, w_ref, b_ref, o_ref):
    a = jax.nn.silu(cv_ref[...]).astype(BF16)
    o_ref[0] = _dot(a, w_ref[0].astype(BF16)) + b_ref[0]


def _modulation(cvec, w_mod, b_mod):
    tn = 1536
    n_out = 6 * D_MODEL
    out = pl.pallas_call(
        _mod_kernel,
        grid=(DEPTH, n_out // tn),
        in_specs=[pl.BlockSpec((MOD_ROWS, D_MODEL), lambda l, j: (0, 0)),
                  pl.BlockSpec((1, D_MODEL, tn), lambda l, j: (l, 0, j)),
                  pl.BlockSpec((1, 1, tn), lambda l, j: (l, 0, j))],
        out_specs=pl.BlockSpec((1, MOD_ROWS, tn), lambda l, j: (l, 0, j)),
        out_shape=jax.ShapeDtypeStruct((DEPTH, MOD_ROWS, n_out), F32),
        compiler_params=_params("parallel", "parallel"),
        name="modulation",
    )(cvec, w_mod, b_mod.reshape(DEPTH, 1, n_out))
    return out.reshape(DEPTH, MOD_ROWS, 6, D_MODEL)


def _mod_row(tile_rows, n_ctx, t_lat):
    def row(i):
        start = i * tile_rows
        return jnp.where(start < n_ctx, 0, 1 + (start - n_ctx) // t_lat)
    return row


def _proj_in_kernel(x_ref, mod_ref, nw_ref, w_ref, wg_ref, o_ref, glr_ref, h_scr):
    @pl.when(pl.program_id(1) == 0)
    def _():
        y = _rms(x_ref[...]) * nw_ref[...]
        h = (y * (1.0 + mod_ref[0, 1:2, :]) + mod_ref[0, 0:1, :]).astype(BF16)
        h_scr[...] = h
        glr_ref[...] = _dot(h, wg_ref[...])

    o_ref[...] = _dot(h_scr[...], w_ref[...])


def _proj_in(x, mod, norm_w, w_main, w_glr, row):
    n = x.shape[0]
    tm, tn = 512, 2304
    return pl.pallas_call(
        _proj_in_kernel,
        grid=(n // tm, PROJ_COLS // tn),
        in_specs=[pl.BlockSpec((tm, D_MODEL), lambda i, j: (i, 0)),
                  pl.BlockSpec((1, 6, D_MODEL), lambda i, j: (row(i), 0, 0)),
                  pl.BlockSpec((1, D_MODEL), lambda i, j: (0, 0)),
                  pl.BlockSpec((D_MODEL, tn), lambda i, j: (0, j)),
                  pl.BlockSpec((D_MODEL, LANES), lambda i, j: (0, 0))],
        out_specs=[pl.BlockSpec((tm, tn), lambda i, j: (i, j)),
                   pl.BlockSpec((tm, LANES), lambda i, j: (i, 0))],
        out_shape=[jax.ShapeDtypeStruct((n, PROJ_COLS), F32),
                   jax.ShapeDtypeStruct((n, LANES), F32)],
        scratch_shapes=[pltpu.VMEM((tm, D_MODEL), BF16)],
        compiler_params=_params("parallel", "arbitrary"),
        name="proj_in",
    )(x, mod, norm_w, w_main, w_glr)


GLA_HB = 8


def _gla_kernel(qk_ref, v_ref, glr_ref, wg_ref, bg_ref, swap_ref, mask_ref, init_ref, out_hbm_ref, o_ref, st_ref,
                st_scr, x1_scr, k1_scr, x2_scr, k2_scr, vb_scr, dec_scr, upd_scr, sp_scr, *, n_chunks):
    del out_hbm_ref
    d = pl.program_id(2)
    t = pl.program_id(3)
    fwd = d == 0

    @pl.when(t == 0)
    def _():
        st_scr[...] = init_ref[0, 0]

    c = GLA_CHUNK
    hw = GLA_HB * LANES
    mask = mask_ref[0]
    qmask = lax.broadcasted_iota(jnp.int32, (1, hw), 1) % LANES < GLA_DK

    z = _dot(glr_ref[...].astype(BF16), wg_ref[0]) + bg_ref[0]
    la = jax.nn.log_sigmoid(z) / GLA_TAU
    blk = 2 * c
    m01 = mask[0:blk, 0:blk].astype(BF16)
    b_all = jnp.concatenate([_dot_exact01(m01, la[r:r + blk, :]) for r in range(0, n_chunks * c, blk)], axis=0)
    vb_scr[...] = v_ref[...].astype(BF16)
    for k in range(n_chunks):
        rows = slice(k * c, (k + 1) * c)
        b = b_all[rows, :]
        b_mid = jnp.where(fwd, b[c // 2 - 1:c // 2, :], b[c // 2:c // 2 + 1, :])
        b_end = jnp.where(fwd, b[c - 1:c, :], b[0:1, :])
        qk = qk_ref[rows, :]
        x1_scr[rows, :] = (qk * jnp.exp(jnp.where(qmask, b - b_mid, b_mid - b))).astype(BF16)
        x2_scr[rows, :] = (qk * jnp.exp(jnp.where(qmask, b, b_end - b))).astype(BF16)
        dec_scr[k] = jnp.exp(b_end)
    swap = swap_ref[...]
    for h in range(GLA_HB):
        hl = slice(h * LANES, (h + 1) * LANES)
        k1_scr[:, hl] = _dot(x1_scr[:, hl], swap).astype(BF16)
        k2_scr[:, hl] = _dot(x2_scr[:, hl], swap).astype(BF16)

    for h in range(GLA_HB):
        hl = slice(h * LANES, (h + 1) * LANES)
        s = lax.dot_general(x1_scr[:, hl], k1_scr[:, hl], NT_DIMS, preferred_element_type=F32)
        s = jnp.where(mask > 0.0, s, 0.0).astype(BF16)
        o_ref[0, :, hl] = _dot(s, vb_scr[:, hl])

    for k in range(n_chunks):
        rows = slice(k * c, (k + 1) * c)
        for h in range(GLA_HB):
            hl = slice(h * LANES, (h + 1) * LANES)
            upd_scr[k, h] = lax.dot_general(vb_scr[rows, hl], k2_scr[rows, hl], TN_DIMS,
                                            preferred_element_type=F32)

    def chunk(step, carry):
        k = jnp.where(fwd, step, n_chunks - 1 - step)
        dec = dec_scr[k]
        for h in range(GLA_HB):
            st = st_scr[h]
            sp_scr[k, h] = st.astype(BF16)
            st_scr[h] = st * dec[:, h * LANES:(h + 1) * LANES] + upd_scr[k, h]
        return carry

    lax.fori_loop(0, n_chunks, chunk, 0)

    for k in range(n_chunks):
        rows = slice(k * c, (k + 1) * c)
        for h in range(GLA_HB):
            hl = slice(h * LANES, (h + 1) * LANES)
            o_ref[0, rows, hl] += lax.dot_general(x2_scr[rows, hl], sp_scr[k, h], NT_DIMS,
                                                  preferred_element_type=F32)

    @pl.when(t == pl.num_programs(3) - 1)
    def _():
        st_ref[0, 0] = st_scr[...]


def _gla(proj, glr, wg, bg, init_t, out, *, row0, batch, seq, tile):
    nt = seq // tile
    rb0 = row0 // tile
    hw = GLA_HB * LANES
    n_chunks = tile // GLA_CHUNK
    l = np.arange(LANES)
    swap = ((l[:, None] == l[None, :] + GLA_DK) & (l[None, :] < GLA_DK)).astype(np.float32)
    r = np.arange(tile)
    same = r[:, None] // GLA_CHUNK == r[None, :] // GLA_CHUNK
    mask = np.stack([same & (r[:, None] >= r[None, :]), same & (r[:, None] <= r[None, :])]).astype(np.float32)

    def rblk(b, t, d):
        return rb0 + b * nt + jnp.where(d == 0, t, nt - 1 - t)

    in_specs = [pl.BlockSpec((tile, hw), lambda b, g, d, t: (rblk(b, t, d), COL_QK // hw + g)),
                pl.BlockSpec((tile, hw), lambda b, g, d, t: (rblk(b, t, d), COL_V // hw + g)),
                pl.BlockSpec((tile, LANES), lambda b, g, d, t: (rblk(b, t, d), 0)),
                pl.BlockSpec((1, LANES, hw), lambda b, g, d, t: (d, 0, g)),
                pl.BlockSpec((1, 1, hw), lambda b, g, d, t: (d, 0, g)),
                pl.BlockSpec((LANES, LANES), lambda b, g, d, t: (0, 0)),
                pl.BlockSpec((1, tile, tile), lambda b, g, d, t: (d, 0, 0)),
                pl.BlockSpec((1, 1, GLA_HB, LANES, LANES), lambda b, g, d, t: (b, d, g, 0, 0)),
                pl.BlockSpec(memory_space=pl.ANY)]
    args = [proj, proj, glr, wg, bg, jnp.asarray(swap, BF16), jnp.asarray(mask), init_t, out]
    return pl.pallas_call(
        functools.partial(_gla_kernel, n_chunks=n_chunks),
        grid=(batch, GLA_HEADS // GLA_HB, 2, nt),
        in_specs=in_specs,
        out_specs=[pl.BlockSpec((1, tile, hw), lambda b, g, d, t: (d, rblk(b, t, d), g)),
                   pl.BlockSpec((1, 1, GLA_HB, LANES, LANES), lambda b, g, d, t: (b, d, g, 0, 0))],
        out_shape=[jax.ShapeDtypeStruct(out.shape, out.dtype),
                   jax.ShapeDtypeStruct((batch, 2, GLA_HEADS, LANES, LANES), F32)],
        scratch_shapes=[pltpu.VMEM((GLA_HB, LANES, LANES), F32)]
                       + [pltpu.VMEM((tile, hw), BF16)] * 5
                       + [pltpu.VMEM((n_chunks, 1, hw), F32),
                          pltpu.VMEM((n_chunks, GLA_HB, LANES, LANES), F32),
                          pltpu.VMEM((n_chunks, GLA_HB, LANES, LANES), BF16)],
        input_output_aliases={len(args) - 1: 0},
        compiler_params=_params("parallel", "parallel", "arbitrary", "arbitrary"),
        name="gla",
    )(*args)


def _prep_kernel(*refs, rope, cache):
    qa_ref, kv_ref, qw_ref, kw_ref = refs[:4]
    refs = refs[4:]
    if rope:
        cos_ref, sa_ref, sb_ref = refs[:3]
        refs = refs[3:]
    qn_ref, kn_ref, vb_ref = refs[:3]
    if cache:
        ck_ref, cv_ref = refs[3:5]

    def rot(y):
        if not rope:
            return y
        return (y * cos_ref[...] + pltpu.roll(y, LANES - 32, 1) * sa_ref[...]
                + pltpu.roll(y, 32, 1) * sb_ref[...])

    for h in range(ATT_HEADS):
        hl = slice(h * HEAD_DIM, (h + 1) * HEAD_DIM)
        qn_ref[:, hl] = rot(_rms(qa_ref[:, hl]) * qw_ref[...]).astype(BF16)
    for h in range(ATT_KV_HEADS):
        hl = slice(h * HEAD_DIM, (h + 1) * HEAD_DIM)
        vl = slice((ATT_KV_HEADS + h) * HEAD_DIM, (ATT_KV_HEADS + h + 1) * HEAD_DIM)
        kn = _rms(kv_ref[:, hl]) * kw_ref[...]
        v = kv_ref[:, vl]
        kn_ref[:, hl] = rot(kn).astype(BF16)
        vb_ref[:, hl] = v.astype(BF16)
        if cache:
            ck_ref[0, h] = kn
            cv_ref[0, h] = v


def _prep(proj, qw, kw, rope_tabs, *, row0, batch, seq, cache):
    tm = 256
    n = batch * seq
    rb0 = row0 // tm
    per_req = seq // tm
    kvw = ATT_KV_HEADS * HEAD_DIM
    rope = rope_tabs is not None
    in_specs = [pl.BlockSpec((tm, ATT_HEADS * HEAD_DIM), lambda i: (rb0 + i, COL_QA // (ATT_HEADS * HEAD_DIM))),
                pl.BlockSpec((tm, 2 * kvw), lambda i: (rb0 + i, COL_KV // (2 * kvw))),
                pl.BlockSpec((1, HEAD_DIM), lambda i: (0, 0)),
                pl.BlockSpec((1, HEAD_DIM), lambda i: (0, 0))]
    args = [proj, proj, qw, kw]
    if rope:
        in_specs += [pl.BlockSpec((tm, HEAD_DIM), lambda i: (i % per_req, 0))] * 3
        args += list(rope_tabs)
    out_specs = [pl.BlockSpec((tm, ATT_HEADS * HEAD_DIM), lambda i: (i, 0)),
                 pl.BlockSpec((tm, kvw), lambda i: (i, 0)),
                 pl.BlockSpec((tm, kvw), lambda i: (i, 0))]
    out_shape = [jax.ShapeDtypeStruct((n, ATT_HEADS * HEAD_DIM), BF16),
                 jax.ShapeDtypeStruct((n, kvw), BF16),
                 jax.ShapeDtypeStruct((n, kvw), BF16)]
    if cache:
        assert seq == tm
        out_specs += [pl.BlockSpec((1, ATT_KV_HEADS, seq, HEAD_DIM), lambda i: (i, 0, 0, 0))] * 2
        out_shape += [jax.ShapeDtypeStruct((batch, ATT_KV_HEADS, seq, HEAD_DIM), F32)] * 2
    return pl.pallas_call(
        functools.partial(_prep_kernel, rope=rope, cache=cache),
        grid=(n // tm,),
        in_specs=in_specs, out_specs=out_specs, out_shape=out_shape,
        compiler_params=_params("parallel"),
        name="attn_prep",
    )(*args)


def _rope_tables(seq):
    half = HEAD_DIM // 2
    tok = jnp.arange(seq)
    row = (tok // GRID_W).astype(F32)
    col = (tok % GRID_W).astype(F32)
    inv_freq = ROPE_THETA ** (-jnp.arange(0, half, 2, dtype=F32) / half)
    ang_r = row[:, None] * inv_freq[None, :]
    ang_c = col[:, None] * inv_freq[None, :]
    ang = jnp.concatenate([ang_r, ang_r, ang_c, ang_c], axis=-1)
    first = (jnp.arange(HEAD_DIM) % half) < half // 2
    sin = jnp.sin(ang)
    return jnp.cos(ang), jnp.where(first, -sin, 0.0), jnp.where(first, 0.0, sin)


ATT_COL_BLOCK = 512

def _attn_kernel(*refs, ctx):
    o_ref, s_scr = refs[-2], refs[-1]
    refs = refs[:-3]
    if ctx:
        q_ref, k_ref, v_ref, ck_ref, cv_ref = refs
        ck = ck_ref[0, 0, 0].astype(BF16)
        cv = cv_ref[0, 0, 0].astype(BF16)
    else:
        q_ref, k_ref, v_ref = refs
    k = k_ref[...]
    v = v_ref[...]
    c2 = (HEAD_DIM ** -0.5) * LOG2_E

    n_keys = k_ref.shape[0]
    blocks = [(k, v, r, min(ATT_COL_BLOCK, n_keys - r), r) for r in range(0, n_keys, ATT_COL_BLOCK)]
    if ctx:
        past = ck.shape[0]
        blocks += [(ck, cv, r, min(ATT_COL_BLOCK, past - r), n_keys + r) for r in range(0, past, ATT_COL_BLOCK)]

    def scores(g):
        q = q_ref[:, g * HEAD_DIM:(g + 1) * HEAD_DIM]
        for kk, _, r, w, col in blocks:
            s_scr[g % 2, :, col:col + w] = lax.dot_general(q, kk[r:r + w, :], NT_DIMS, preferred_element_type=F32)

    def lane_fold(x, op):
        parts = [x[:, i:i + LANES] for i in range(0, x.shape[1], LANES)]
        while len(parts) > 1:
            parts = [op(parts[i], parts[i + 1]) for i in range(0, len(parts) - 1, 2)] + parts[len(parts) & ~1:]
        return parts[0]

    def finish(g):
        sg = s_scr.at[g % 2]
        mp = None
        for _, _, _, w, col in blocks:
            part = lane_fold(sg[:, col:col + w], jnp.maximum)
            mp = part if mp is None else jnp.maximum(mp, part)
        mc = jnp.max(mp, axis=-1, keepdims=True) * c2
        lp = jnp.zeros((q_ref.shape[0], LANES), F32)
        acc = jnp.zeros((q_ref.shape[0], HEAD_DIM), F32)
        for _, vv, r, w, col in blocks:
            p = jnp.exp2(sg[:, col:col + w] * c2 - mc)
            lp = lp + lane_fold(p, jnp.add)
            acc = acc + _dot(p.astype(BF16), vv[r:r + w, :])
        l = jnp.sum(lp, axis=-1, keepdims=True)
        o_ref[:, g * HEAD_DIM:(g + 1) * HEAD_DIM] = (acc / l).astype(BF16)

    scores(0)
    for g in range(ATT_GROUP):
        if g + 1 < ATT_GROUP:
            scores(g + 1)
        finish(g)


def _attention(qn, kn, vb, ctx_kv, out, *, row0, batch, seq, tq):
    nq = seq // tq
    ob0 = row0 // tq
    past = ctx_kv[0].shape[3] if ctx_kv is not None else 0
    assert seq % LANES == 0 and past % LANES == 0
    gw = ATT_GROUP * HEAD_DIM
    in_specs = [pl.BlockSpec((tq, gw), lambda b, h, t: (b * nq + t, h)),
                pl.BlockSpec((seq, HEAD_DIM), lambda b, h, t: (b, h)),
                pl.BlockSpec((seq, HEAD_DIM), lambda b, h, t: (b, h))]
    args = [qn, kn, vb]
    if ctx_kv is not None:
        ck, cv, layer = ctx_kv
        spec = pl.BlockSpec((1, 1, 1, past, HEAD_DIM), lambda b, h, t: (b, layer, h, 0, 0))
        in_specs += [spec, spec]
        args += [ck, cv]
    in_specs.append(pl.BlockSpec(memory_space=pl.ANY))
    args.append(out)
    return pl.pallas_call(
        functools.partial(_attn_kernel, ctx=ctx_kv is not None),
        grid=(batch, ATT_KV_HEADS, nq),
        in_specs=in_specs,
        out_specs=pl.BlockSpec((tq, gw), lambda b, h, t: (ob0 + b * nq + t, h)),
        out_shape=jax.ShapeDtypeStruct(out.shape, out.dtype),
        scratch_shapes=[pltpu.VMEM((2, tq, seq + past), F32)],
        input_output_aliases={len(args) - 1: 0},
        compiler_params=_params("parallel", "parallel", "arbitrary"),
        name="attention",
    )(*args)


def _out_proj_kernel(x_ref, od_ref, og_ref, oa_ref, gn_ref, w_ref, mod_ref, nw_ref, x1_ref, h2_ref):
    o = od_ref[0] + od_ref[1]
    parts = []
    for h in range(GLA_HEADS):
        hl = slice(h * GLA_DV, (h + 1) * GLA_DV)
        parts.append((_rms(o[:, hl]) * gn_ref[...] * jax.nn.silu(og_ref[:, hl])).astype(BF16))
    og = jnp.concatenate(parts, axis=-1)
    gw = GLA_HEADS * GLA_DV
    mix = _dot(og, w_ref[0:gw, :]) + _dot(oa_ref[...], w_ref[gw:, :])
    x1 = x_ref[...] + mod_ref[0, 2:3, :] * mix
    x1_ref[...] = x1
    h2_ref[...] = _rms(x1) * nw_ref[...] * (1.0 + mod_ref[0, 4:5, :]) + mod_ref[0, 3:4, :]


def _out_proj(x, odir, proj, oa, gla_norm, w_out, mod, norm_w, row):
    n = x.shape[0]
    tm = 512
    gw = GLA_HEADS * GLA_DV
    return pl.pallas_call(
        _out_proj_kernel,
        grid=(n // tm,),
        in_specs=[pl.BlockSpec((tm, D_MODEL), lambda i: (i, 0)),
                  pl.BlockSpec((2, tm, gw), lambda i: (0, i, 0)),
                  pl.BlockSpec((tm, gw), lambda i: (i, COL_OG // gw)),
                  pl.BlockSpec((tm, ATT_HEADS * HEAD_DIM), lambda i: (i, 0)),
                  pl.BlockSpec((1, GLA_DV), lambda i: (0, 0)),
                  pl.BlockSpec((gw + ATT_HEADS * HEAD_DIM, D_MODEL), lambda i: (0, 0)),
                  pl.BlockSpec((1, 6, D_MODEL), lambda i: (row(i), 0, 0)),
                  pl.BlockSpec((1, D_MODEL), lambda i: (0, 0))],
        out_specs=[pl.BlockSpec((tm, D_MODEL), lambda i: (i, 0)),
                   pl.BlockSpec((tm, D_MODEL), lambda i: (i, 0))],
        out_shape=[jax.ShapeDtypeStruct((n, D_MODEL), F32),
                   jax.ShapeDtypeStruct((n, D_MODEL), F32)],
        compiler_params=_params("parallel"),
        name="out_proj",
    )(x, odir, proj, oa, gla_norm, w_out, mod, norm_w)


NEG_INF = float("-inf")
N_CAND = PEER_TOPK * PEER_TOPK


_CAND_GROUPS = ((0, 0), (0, 8), (1, 0), (2, 0), (3, 0), (4, 0), (5, 0), (6, 0), (7, 0))
N_CAND_ROWS = 8 * (len(_CAND_GROUPS) + 1)


def _cand_consts():
    pos = np.zeros((N_CAND_ROWS, 1), np.float32)
    off = np.zeros((N_CAND_ROWS, 1), np.float32)
    for g, (a, b0) in enumerate(_CAND_GROUPS):
        for j in range(8):
            pos[8 * g + j] = a * PEER_TOPK + b0 + j
            off[8 * g + j] = 0.0 if (a + 1) * (b0 + j + 1) <= PEER_TOPK else NEG_INF
    for j in range(8):
        pos[N_CAND_ROWS - 8 + j] = (8 + j) * PEER_TOPK
    return (jnp.asarray(np.broadcast_to(pos, (N_CAND_ROWS, LANES))),
            jnp.asarray(np.broadcast_to(off, (N_CAND_ROWS, LANES))))


def _pair_rows(first, second):
    rows = [first[a:a + 1, :] + second[b0:b0 + 8, :] for a, b0 in _CAND_GROUPS]
    rows.append(first[8:16, :] + second[0:1, :])
    return jnp.concatenate(rows, axis=0)


def _top16_rows(s, row_id):
    rank = lax.broadcasted_iota(jnp.int32, (PEER_TOPK, s.shape[1]), 0)
    vals = jnp.zeros((PEER_TOPK, s.shape[1]), F32)
    ids = jnp.zeros((PEER_TOPK, s.shape[1]), F32)
    for r in range(PEER_TOPK):
        m = jnp.max(s, axis=0, keepdims=True)
        i = jnp.min(jnp.where(s == m, row_id, float(N_CAND)), axis=0, keepdims=True)
        s = jnp.where(row_id == i, NEG_INF, s)
        vals = jnp.where(rank == r, m, vals)
        ids = jnp.where(rank == r, i, ids)
    return vals, ids


def _peer_topk_kernel(h_ref, wq_ref, keys_ref, pos_ref, off_ref, idx_ref, gate_ref, q_scr):
    tm = h_ref.shape[0]
    q = _dot(h_ref[...].astype(BF16), wq_ref[...]).astype(BF16)
    for a in range(2 * PEER_HEADS):
        q_scr[a] = q[:, a * LANES:(a + 1) * LANES]
    key_id = lax.broadcasted_iota(jnp.int32, (N_KEYS, LANES), 0).astype(F32)
    rank = lax.broadcasted_iota(jnp.int32, (PEER_TOPK, LANES), 0)
    pos = pos_ref[...]
    off = off_ref[...]

    def head(h, carry):
        out_rows = pl.ds(pl.multiple_of(h * PEER_TOPK, PEER_TOPK), PEER_TOPK)
        for c in range(tm // LANES):
            cols = slice(c * LANES, (c + 1) * LANES)
            s1 = lax.dot_general(keys_ref[h, 0], q_scr[2 * h, cols, :], NT_DIMS, preferred_element_type=F32)
            s2 = lax.dot_general(keys_ref[h, 1], q_scr[2 * h + 1, cols, :], NT_DIMS, preferred_element_type=F32)
            v1, i1 = _top16_rows(s1, key_id)
            v2, i2 = _top16_rows(s2, key_id)
            cand = _pair_rows(v1, v2) + off
            cidx = _pair_rows(i1 * float(N_KEYS), i2)
            best = jnp.zeros((PEER_TOPK, LANES), F32)
            eidx = jnp.zeros((PEER_TOPK, LANES), F32)
            for r in range(PEER_TOPK):
                m = jnp.max(cand, axis=0, keepdims=True)
                p = jnp.min(jnp.where(cand == m, pos, float(N_CAND)), axis=0, keepdims=True)
                sel = pos == p
                e = jnp.max(jnp.where(sel, cidx, -1.0), axis=0, keepdims=True)
                cand = jnp.where(sel, NEG_INF, cand)
                best = jnp.where(rank == r, m, best)
                eidx = jnp.where(rank == r, e, eidx)
            ex = jnp.exp(best - best[0:1, :])
            gate_ref[out_rows, cols] = ex / jnp.sum(ex, axis=0, keepdims=True)
            idx_ref[out_rows, cols] = eidx.astype(jnp.int32)
        return carry

    lax.fori_loop(0, PEER_HEADS, head, 0)


def _peer_topk(h2, wq, keys):
    n = h2.shape[0]
    tm = 1024
    qw = PEER_HEADS * PEER_QDIM
    full = lambda shape: pl.BlockSpec(shape, lambda i: (0,) * len(shape))
    return pl.pallas_call(
        _peer_topk_kernel,
        grid=(n // tm,),
        in_specs=[pl.BlockSpec((tm, D_MODEL), lambda i: (i, 0)),
                  full((D_MODEL, qw)),
                  full((PEER_HEADS, 2, N_KEYS, PEER_QDIM // 2)),
                  full((N_CAND_ROWS, LANES)), full((N_CAND_ROWS, LANES))],
        out_specs=[pl.BlockSpec((PEER_SEL, tm), lambda i: (0, i)),
                   pl.BlockSpec((PEER_SEL, tm), lambda i: (0, i))],
        out_shape=[jax.ShapeDtypeStruct((PEER_SEL, n), jnp.int32),
                   jax.ShapeDtypeStruct((PEER_SEL, n), F32)],
        scratch_shapes=[pltpu.VMEM((2 * PEER_HEADS, tm, LANES), BF16)],
        compiler_params=_params("parallel"),
        name="peer_topk",
    )(h2, wq, keys, *_cand_consts())


U32 = jnp.uint32


def _pack_uv_kernel(u_ref, v_ref, o_ref):
    ub = lax.bitcast_convert_type(u_ref[0].astype(BF16).astype(F32), U32)
    vb = lax.bitcast_convert_type(v_ref[0].astype(BF16).astype(F32), U32)
    o_ref[:, 0, :] = ub | (vb >> 16)


def _pack_uv(u, v, layer):
    n, tr = u.shape[1], 512
    spec = pl.BlockSpec((1, tr, D_MODEL), lambda i: (layer, i, 0))
    return pl.pallas_call(
        _pack_uv_kernel, grid=(n // tr,), in_specs=[spec, spec],
        out_specs=pl.BlockSpec((tr, 1, D_MODEL), lambda i: (i, 0, 0)),
        out_shape=jax.ShapeDtypeStruct((n, 1, D_MODEL), U32),
        compiler_params=_params("parallel"), name="pack_uv",
    )(u, v)


PEER_TB = 256
PEER_NBUF = 8


def _peer_mix_kernel(idx_ref, idxn_ref, gate_ref, h_ref, x_ref, mod_ref, uv_ref, o_ref, *scratch):
    step = pl.program_id(0)
    last_step = pl.num_programs(0) - 1
    bufs, sem = scratch[:PEER_NBUF], scratch[PEER_NBUF]
    tb = h_ref.shape[0]
    n_groups = tb // PEER_NBUF

    def issue(t, slot, ids=idx_ref):
        for k in range(PEER_SEL):
            pltpu.make_async_copy(uv_ref.at[ids[k, t]], bufs[slot].at[pl.ds(k, 1), :],
                                  sem.at[slot]).start(priority=k % 2)

    def wait(slot):
        pltpu.make_async_copy(bufs[slot], bufs[slot], sem.at[slot]).wait()

    gate_t = gate_ref[...]
    tok = lax.broadcasted_iota(jnp.int32, (1, tb), 1)
    g2 = mod_ref[0, 5:6, :]

    def token(t, slot, prefetch):
        wait(slot)
        if prefetch:
            issue(t + PEER_NBUF - 1, (slot + PEER_NBUF - 1) % PEER_NBUF)
        x = h_ref[pl.ds(t, 1), :]
        w = bufs[slot][...]
        u = lax.bitcast_convert_type(w & jnp.uint32(0xFFFF0000), F32)
        s = jnp.sum(u * x, axis=-1, keepdims=True)
        g = jnp.sum(jnp.where(tok == t, gate_t, 0.0), axis=-1, keepdims=True)
        act = jax.nn.gelu(s) * g
        v = lax.bitcast_convert_type(w << 16, F32)
        o = jnp.sum(act * v, axis=0, keepdims=True)
        o_ref[pl.ds(t, 1), :] = x_ref[pl.ds(t, 1), :] + g2 * o

    @pl.when(step == 0)
    def _():
        for t0 in range(PEER_NBUF - 1):
            issue(t0, t0)

    def group(g, carry):
        for j in range(PEER_NBUF):
            token(g * PEER_NBUF + j, j, True)
        return carry

    lax.fori_loop(0, n_groups - 1, group, 0)
    for j in range(PEER_NBUF):
        token((n_groups - 1) * PEER_NBUF + j, j, j == 0)
        if j > 0:
            @pl.when(step < last_step)
            def _():
                issue(j - 1, j - 1, idxn_ref)


def _peer_mix(idx_t, gates_t, h2, x1, mod, uv, row):
    n = h2.shape[0]
    tb = PEER_TB
    return pl.pallas_call(
        _peer_mix_kernel,
        grid=(n // tb,),
        in_specs=[pl.BlockSpec((PEER_SEL, tb), lambda i: (0, i), memory_space=pltpu.SMEM),
                  pl.BlockSpec((PEER_SEL, tb), lambda i: (0, jnp.minimum(i + 1, n // tb - 1)), memory_space=pltpu.SMEM),
                  pl.BlockSpec((PEER_SEL, tb), lambda i: (0, i)),
                  pl.BlockSpec((tb, D_MODEL), lambda i: (i, 0)),
                  pl.BlockSpec((tb, D_MODEL), lambda i: (i, 0)),
                  pl.BlockSpec((1, 6, D_MODEL), lambda i: (row(i), 0, 0)),
                  pl.BlockSpec(memory_space=pl.ANY)],
        out_specs=pl.BlockSpec((tb, D_MODEL), lambda i: (i, 0)),
        out_shape=jax.ShapeDtypeStruct((n, D_MODEL), F32),
        scratch_shapes=[pltpu.VMEM((PEER_SEL, D_MODEL), U32)] * PEER_NBUF
                       + [pltpu.SemaphoreType.DMA((PEER_NBUF,))],
        compiler_params=_params("arbitrary"),
        name="peer_mix",
    )(idx_t, idx_t, gates_t, h2, x1, mod, uv)


def _layer_weights(w_in, w_gate, b_gate, w_out, peer_wq, peer_keys):
    o = np.cumsum((0, GLA_HEADS * GLA_DK, GLA_HEADS * GLA_DK, GLA_HEADS * GLA_DV, 2 * GLA_RANK,
                   GLA_HEADS * GLA_DV, ATT_HEADS * HEAD_DIM, ATT_KV_HEADS * HEAD_DIM, ATT_KV_HEADS * HEAD_DIM))
    wq_g = w_in[:, o[0]:o[1]].reshape(D_MODEL, GLA_HEADS, GLA_DK) * (GLA_DK ** -0.5)
    wk_g = w_in[:, o[1]:o[2]].reshape(D_MODEL, GLA_HEADS, GLA_DK)
    w_qk = jnp.concatenate([wq_g, wk_g], axis=-1).reshape(D_MODEL, GLA_HEADS * LANES)
    w_main = jnp.concatenate([w_qk, w_in[:, o[2]:o[3]], w_in[:, o[4]:o[5]], w_in[:, o[5]:o[6]],
                              w_in[:, o[6]:o[7]], w_in[:, o[7]:o[8]]], axis=-1).astype(BF16)
    w_glr = jnp.pad(w_in[:, o[3]:o[4]], ((0, 0), (0, LANES - 2 * GLA_RANK))).astype(BF16)
    wg = w_gate.reshape(2, GLA_RANK, GLA_HEADS, GLA_DK)
    wg = jnp.concatenate([wg, wg], axis=-1).reshape(2, GLA_RANK, GLA_HEADS * LANES)
    wg = jnp.stack([jnp.pad(wg[0], ((0, LANES - GLA_RANK), (0, 0))),
                    jnp.pad(wg[1], ((GLA_RANK, LANES - 2 * GLA_RANK), (0, 0)))]).astype(BF16)
    bg = b_gate.reshape(2, GLA_HEADS, GLA_DK)
    bg = jnp.concatenate([bg, bg], axis=-1).reshape(2, 1, GLA_HEADS * LANES)
    return dict(w_main=w_main, w_glr=w_glr, wg=wg, bg=bg, w_out=w_out.astype(BF16),
                wq=peer_wq.astype(BF16), keys=peer_keys.astype(BF16))


def _state_to_kernel(s):
    st = jnp.swapaxes(s, -1, -2)
    return jnp.pad(st, [(0, 0)] * (st.ndim - 1) + [(0, LANES - GLA_DK)])


def _state_from_kernel(st):
    return jnp.swapaxes(st[..., :GLA_DK], -1, -2)


def kernel(x_prompt, x_sample, cache_k, cache_v, state_gla, c, c_ctx, w_mod, b_mod, norm_mix, norm_ffn,
           w_in, w_gate, b_gate, gla_norm, q_norm, k_norm, w_out, peer_wq, peer_keys, peer_u, peer_v):
    bc, tc, _ = x_prompt.shape
    bl, tl, _ = x_sample.shape
    n_ctx, n_lat = bc * tc, bl * tl
    assert bl + 1 <= MOD_ROWS

    cvec = jnp.concatenate([c_ctx[None, :], c, jnp.zeros((MOD_ROWS - 1 - bl, D_MODEL), F32)], axis=0)
    mod_all = _modulation(cvec, w_mod, b_mod)
    rope_tabs = _rope_tables(tl)
    x = jnp.concatenate([x_prompt.reshape(n_ctx, D_MODEL), x_sample.reshape(n_lat, D_MODEL)], axis=0)
    zero_state = jnp.zeros((bc, 2, GLA_HEADS, LANES, LANES), F32)

    ks, vs, ss = [], [], []
    for l in range(DEPTH):
        w = _layer_weights(w_in[l], w_gate[l], b_gate[l], w_out[l], peer_wq[l], peer_keys[l])
        uv = _pack_uv(peer_u, peer_v, l)
        mod = mod_all[l]
        proj, glr = _proj_in(x, mod, norm_mix[l][None, :], w["w_main"], w["w_glr"], _mod_row(512, n_ctx, tl))

        odir = jnp.zeros((2, n_ctx + n_lat, GLA_HEADS * GLA_DV), F32)
        odir, st_c = _gla(proj, glr, w["wg"], w["bg"], zero_state, odir, row0=0, batch=bc, seq=tc, tile=tc)
        odir, _ = _gla(proj, glr, w["wg"], w["bg"], _state_to_kernel(state_gla[:, l]), odir,
                       row0=n_ctx, batch=bl, seq=tl, tile=512)
        ss.append(_state_from_kernel(st_c))

        qw, kw = q_norm[l][None, :], k_norm[l][None, :]
        qn_c, kn_c, vb_c, ck, cv = _prep(proj, qw, kw, None, row0=0, batch=bc, seq=tc, cache=True)
        qn_l, kn_l, vb_l = _prep(proj, qw, kw, rope_tabs, row0=n_ctx, batch=bl, seq=tl, cache=False)
        ks.append(ck)
        vs.append(cv)
        oa = jnp.zeros((n_ctx + n_lat, ATT_HEADS * HEAD_DIM), BF16)
        oa = _attention(qn_c, kn_c, vb_c, None, oa, row0=0, batch=bc, seq=tc, tq=tc)
        oa = _attention(qn_l, kn_l, vb_l, (cache_k, cache_v, l), oa, row0=n_ctx, batch=bl, seq=tl, tq=256)

        x1, h2 = _out_proj(x, odir, proj, oa, gla_norm[l][None, :], w["w_out"], mod, norm_ffn[l][None, :],
                           _mod_row(512, n_ctx, tl))
        idx, gates = _peer_topk(h2, w["wq"], w["keys"])
        x = _peer_mix(idx, gates, h2, x1, mod, uv, _mod_row(PEER_TB, n_ctx, tl))

    y_prompt = x[:n_ctx].reshape(bc, tc, D_MODEL)
    y_sample = x[n_ctx:].reshape(bl, tl, D_MODEL)
    return (y_prompt, y_sample, jnp.stack(ks, axis=1), jnp.stack(vs, axis=1), jnp.stack(ss, axis=1))
```
